```python
import math
import jax, jax.numpy as jnp
from jax import lax
import numpy as np

D_MODEL = 1024
BATCH = 8
SEQ = 4096
DEPTH = 1

D_MIX = D_MODEL
ATTN_WIDTH = D_MIX // 2
N_DIFF_HEADS = 4
DIFF_HEAD_DIM = ATTN_WIDTH // (2 * N_DIFF_HEADS)
DIFF_V_DIM = 2 * DIFF_HEAD_DIM
ATTN_QK = N_DIFF_HEADS * 2 * DIFF_HEAD_DIM
POOL_WIDTH = D_MIX - ATTN_WIDTH
POOL_WINDOWS = (2, 4, 8, 16)
N_POOL_GROUPS = len(POOL_WINDOWS)
POOL_GROUP_DIM = POOL_WIDTH // N_POOL_GROUPS
D_IN_PROJ = 2 * ATTN_QK + ATTN_WIDTH + POOL_WIDTH

ROPE_THETA = 500000.0
ROPE_DIM = DIFF_HEAD_DIM // 4
Q_BLOCK = 128

N_EXPERTS = 32
TOP_K = 4
D_EXPERT = D_MODEL
SWIGLU_LIMIT = 7.0
SWIGLU_ALPHA = 1.702
MOE_BLOCK = 128

PLE_DIM = 256
EPS = 1e-6

kernel_name = "hybrid_diffattn_pool_moe_ple"


def rms_norm(t, g):
    tf = t.astype(jnp.float32)
    y = tf * lax.rsqrt(jnp.mean(tf * tf, axis=-1, keepdims=True) + EPS)
    return (y * g.astype(jnp.float32)).astype(t.dtype)


def apply_partial_rope(t, cos, sin):
    tf = t.astype(jnp.float32)
    half = ROPE_DIM // 2
    t1 = tf[..., :half]
    t2 = tf[..., half:ROPE_DIM]
    out = jnp.concatenate([t1 * cos - t2 * sin, t2 * cos + t1 * sin, tf[..., ROPE_DIM:]], axis=-1)
    return out.astype(t.dtype)


def diff_attention(q, k, v, positions, q_norm, k_norm, lam_q1, lam_k1, lam_q2, lam_k2, subln, lambda_init):
    B, S, _ = q.shape
    H, d = N_DIFF_HEADS, DIFF_HEAD_DIM
    q = rms_norm(q.reshape(B, S, H, 2, d), q_norm)
    k = rms_norm(k.reshape(B, S, H, 2, d), k_norm)
    v = v.reshape(B, S, H, DIFF_V_DIM)
    freqs = ROPE_THETA ** (-jnp.arange(0, ROPE_DIM, 2, dtype=jnp.float32) / ROPE_DIM)
    ang = positions.astype(jnp.float32)[..., None] * freqs
    cos = jnp.cos(ang)[:, :, None, None, :]
    sin = jnp.sin(ang)[:, :, None, None, :]
    q = apply_partial_rope(q, cos, sin)
    k = apply_partial_rope(k, cos, sin)
    lam = (jnp.exp(jnp.sum(lam_q1.astype(jnp.float32) * lam_k1.astype(jnp.float32)))
           - jnp.exp(jnp.sum(lam_q2.astype(jnp.float32) * lam_k2.astype(jnp.float32)))
           + lambda_init)
    scale = 1.0 / math.sqrt(d)
    nb = S // Q_BLOCK
    qb = q.transpose(0, 2, 3, 1, 4).reshape(B, H, 2, nb, Q_BLOCK, d)
    qb = jnp.moveaxis(qb, 3, 0)
    kt = k.transpose(0, 2, 3, 1, 4)
    vt = v.transpose(0, 2, 1, 3)
    key_idx = jnp.arange(S)

    def block(args):
        qblk, start = args
        s = jnp.einsum('bhcqd,bhckd->bhcqk', qblk, kt).astype(jnp.float32) * scale
        q_idx = start + jnp.arange(Q_BLOCK)
        mask = key_idx[None, :] <= q_idx[:, None]
        s = jnp.where(mask, s, -jnp.inf)
        a = jax.nn.softmax(s, axis=-1)
        w = a[:, :, 0] - lam * a[:, :, 1]
        return jnp.einsum('bhqk,bhkv->bhqv', w.astype(vt.dtype), vt)

    starts = jnp.arange(nb) * Q_BLOCK
    o = lax.map(block, (qb, starts))
    o = o.transpose(1, 0, 3, 2, 4).reshape(B, S, H, DIFF_V_DIM)
    o = rms_norm(o, subln) * (1.0 - lambda_init)
    return o.reshape(B, S, H * DIFF_V_DIM)


def pool_mixer(u, w_pool, pool_scale):
    B, S, _ = u.shape
    ug = u.reshape(B, S, N_POOL_GROUPS, POOL_GROUP_DIM)
    t_idx = jnp.arange(S)
    outs = []
    for g, w in enumerate(POOL_WINDOWS):
        uf = ug[:, :, g].astype(jnp.float32)
        cs = jnp.cumsum(uf, axis=1)
        shifted = jnp.pad(cs, ((0, 0), (w, 0), (0, 0)))[:, :S]
        count = jnp.minimum(t_idx + 1, w).astype(jnp.float32)
        outs.append((cs - shifted) / count[None, :, None] - uf)
    pooled = jnp.stack(outs, axis=2).astype(u.dtype)
    y = jnp.einsum('bsgc,gcd->bsgd', pooled, w_pool)
    return y.reshape(B, S, POOL_WIDTH) * pool_scale


def moe(xn, w_router, b_router, w_gate, b_gate, w_up, b_up, w_down, b_down):
    B, S, D = xn.shape
    N = B * S
    xt = xn.reshape(N, D)
    logits = (xt @ w_router).astype(jnp.float32) + b_router.astype(jnp.float32)
    top_vals, top_idx = lax.top_k(logits, TOP_K)
    gates = jax.nn.softmax(top_vals, axis=-1).astype(xn.dtype)
    NK = N * TOP_K
    expert_flat = top_idx.reshape(NK)
    token_flat = jnp.repeat(jnp.arange(N), TOP_K)
    gate_flat = gates.reshape(NK)
    order = jnp.argsort(expert_flat, stable=True)
    se = expert_flat[order]
    st = token_flat[order]
    sg = gate_flat[order]
    counts = jnp.bincount(expert_flat, length=N_EXPERTS)
    padded = ((counts + MOE_BLOCK - 1) // MOE_BLOCK) * MOE_BLOCK
    pend = jnp.cumsum(padded)
    pstart = pend - padded
    start = jnp.cumsum(counts) - counts
    dest = pstart[se] + (jnp.arange(NK) - start[se])
    n_blocks = -(-NK // MOE_BLOCK) + N_EXPERTS
    P = n_blocks * MOE_BLOCK
    xbuf = jnp.zeros((P, D), xn.dtype).at[dest].set(xt[st])
    block_expert = jnp.clip(jnp.searchsorted(pend, jnp.arange(n_blocks) * MOE_BLOCK, side='right'),
                            0, N_EXPERTS - 1)

    def expert_block(args):
        xb, e = args
        gt = xb @ w_gate[e] + b_gate[e]
        up = xb @ w_up[e] + b_up[e]
        gt = jnp.minimum(gt, SWIGLU_LIMIT)
        up = jnp.clip(up, -SWIGLU_LIMIT, SWIGLU_LIMIT)
        hdn = (up + 1.0) * (gt * jax.nn.sigmoid(SWIGLU_ALPHA * gt))
        return hdn @ w_down[e] + b_down[e]

    ybuf = lax.map(expert_block, (xbuf.reshape(n_blocks, MOE_BLOCK, D), block_expert)).reshape(P, D)
    y = ybuf[dest] * sg[:, None]
    out = jnp.zeros((N, D), xn.dtype).at[st].add(y)
    return out.reshape(B, S, D)


def setup_inputs(seed: int = 0) -> dict:
    key = jax.random.key(seed)
    ks = jax.random.split(key, 32)
    f32 = jnp.float32
    L = DEPTH

    def nrm(k, shape, scale):
        return jax.random.normal(k, shape, f32) * scale

    def gain(k, shape):
        return 1.0 + 0.02 * jax.random.normal(k, shape, f32)

    x = jax.random.normal(ks[0], (BATCH, SEQ, D_MODEL), f32)
    p = jax.random.normal(ks[1], (DEPTH, BATCH, SEQ, PLE_DIM), f32)
    offsets = jax.random.randint(ks[2], (BATCH, 1), 0, SEQ, dtype=jnp.int32)
    positions = (offsets + jnp.arange(SEQ, dtype=jnp.int32)[None, :]).astype(jnp.int32)
    return {
        "x": x,
        "p": p,
        "positions": positions,
        "attn_norm": gain(ks[3], (L, D_MODEL)),
        "w_in": nrm(ks[4], (L, D_MODEL, D_IN_PROJ), D_MODEL ** -0.5),
        "q_norm": gain(ks[5], (L, DIFF_HEAD_DIM)),
        "k_norm": gain(ks[6], (L, DIFF_HEAD_DIM)),
        "lam_q1": nrm(ks[7], (L, DIFF_HEAD_DIM), 0.1),
        "lam_k1": nrm(ks[8], (L, DIFF_HEAD_DIM), 0.1),
        "lam_q2": nrm(ks[9], (L, DIFF_HEAD_DIM), 0.1),
        "lam_k2": nrm(ks[10], (L, DIFF_HEAD_DIM), 0.1),
        "subln": gain(ks[11], (L, DIFF_V_DIM)),
        "w_pool": nrm(ks[12], (L, N_POOL_GROUPS, POOL_GROUP_DIM, POOL_GROUP_DIM), POOL_GROUP_DIM ** -0.5),
        "pool_scale": gain(ks[13], (L, POOL_WIDTH)),
        "w_out": nrm(ks[14], (L, D_MIX, D_MODEL), D_MIX ** -0.5),
        "ffn_norm": gain(ks[15], (L, D_MODEL)),
        "w_router": nrm(ks[16], (L, D_MODEL, N_EXPERTS), D_MODEL ** -0.5),
        "b_router": nrm(ks[17], (L, N_EXPERTS), 0.01),
        "w_gate": nrm(ks[18], (L, N_EXPERTS, D_MODEL, D_EXPERT), D_MODEL ** -0.5),
        "b_gate": nrm(ks[19], (L, N_EXPERTS, D_EXPERT), 0.02),
        "w_up": nrm(ks[20], (L, N_EXPERTS, D_MODEL, D_EXPERT), D_MODEL ** -0.5),
        "b_up": nrm(ks[21], (L, N_EXPERTS, D_EXPERT), 0.02),
        "w_down": nrm(ks[22], (L, N_EXPERTS, D_EXPERT, D_MODEL), D_EXPERT ** -0.5),
        "b_down": nrm(ks[23], (L, N_EXPERTS, D_MODEL), 0.02),
        "ple_gate_norm": gain(ks[24], (L, D_MODEL)),
        "w_ple_gate": nrm(ks[25], (L, D_MODEL, D_MODEL), D_MODEL ** -0.5),
        "w_ple_proj": nrm(ks[26], (L, PLE_DIM, D_MODEL), PLE_DIM ** -0.5),
        "ple_post_norm": gain(ks[27], (L, D_MODEL)),
    }


def reference(x, p, positions, attn_norm, w_in, q_norm, k_norm, lam_q1, lam_k1, lam_q2, lam_k2,
              subln, w_pool, pool_scale, w_out, ffn_norm, w_router, b_router, w_gate, b_gate,
              w_up, b_up, w_down, b_down, ple_gate_norm, w_ple_gate, w_ple_proj, ple_post_norm):
    h = x
    for i in range(DEPTH):
        lambda_init = 0.8 - 0.6 * math.exp(-0.3 * i)
        hn = rms_norm(h, attn_norm[i])
        z = hn @ w_in[i]
        q = z[..., :ATTN_QK]
        k = z[..., ATTN_QK:2 * ATTN_QK]
        v = z[..., 2 * ATTN_QK:2 * ATTN_QK + ATTN_WIDTH]
        u = z[..., 2 * ATTN_QK + ATTN_WIDTH:]
        a_out = diff_attention(q, k, v, positions, q_norm[i], k_norm[i], lam_q1[i], lam_k1[i],
                               lam_q2[i], lam_k2[i], subln[i], lambda_init)
        p_out = pool_mixer(u, w_pool[i], pool_scale[i])
        h = h + jnp.concatenate([a_out, p_out], axis=-1) @ w_out[i]
        h = h + moe(rms_norm(h, ffn_norm[i]), w_router[i], b_router[i], w_gate[i], b_gate[i],
                    w_up[i], b_up[i], w_down[i], b_down[i])
        gate = jax.nn.sigmoid(rms_norm(h, ple_gate_norm[i]) @ w_ple_gate[i])
        e = rms_norm(p[i].astype(h.dtype) @ w_ple_proj[i], ple_post_norm[i])
        h = h + gate * e
    return h
```

```python
import functools
import math

import jax
import jax.numpy as jnp
from jax import lax
from jax.experimental import pallas as pl
from jax.experimental.pallas import tpu as pltpu

F32 = jnp.float32
BF16 = jnp.bfloat16
I32 = jnp.int32
U32 = jnp.uint32

N_HEADS = 4
HEAD_DIM = 64
V_DIM = 2 * HEAD_DIM
ATTN_W = N_HEADS * V_DIM
POOL_WINDOWS = (2, 4, 8, 16)
POOL_GROUP = 128
POOL_W = len(POOL_WINDOWS) * POOL_GROUP
POOL_HALO = 16
ROPE_DIM = HEAD_DIM // 4
ROPE_HALF = ROPE_DIM // 2
ROPE_THETA = 500000.0
N_EXPERTS = 32
TOP_K = 4
SWIGLU_LIMIT = 7.0
SWIGLU_ALPHA = 1.702
EPS = 1e-6
LANES = 128
VMEM_LIMIT = 56 * 1024 * 1024

ROW_TILE = 512
Q_TILE = 256
EXPERT_TILE = 512
TOKEN_TILE = 256


def _pack_bf16_pair(lo, hi):
    lo_bits = lax.bitcast_convert_type(lo.astype(BF16).astype(F32), U32)
    hi_bits = lax.bitcast_convert_type(hi.astype(BF16).astype(F32), U32)
    return (lo_bits >> 16) | (hi_bits & jnp.uint32(0xFFFF0000))


def _unpack_bf16_pair(w):
    lo = lax.bitcast_convert_type(w << 16, F32)
    hi = lax.bitcast_convert_type(w & jnp.uint32(0xFFFF0000), F32)
    return lo, hi


def _inproj_kernel(x_ref, pos_ref, freq_ref, an_ref, w_ref, qn_ref, kn_ref, g_ref, z_ref):
    x = x_ref[...]
    ms = jnp.mean(x * x, axis=-1, keepdims=True)
    hn = (x * lax.rsqrt(ms + EPS) * an_ref[...]).astype(BF16)

    ang = pos_ref[...].astype(F32) * freq_ref[...]
    cos = jnp.cos(ang)
    sin = jnp.sin(ang)
    lane = lax.broadcasted_iota(I32, ang.shape, 1) & (HEAD_DIM - 1)
    c = jnp.where(lane < ROPE_DIM, cos, 1.0)
    s_dn = jnp.where((lane >= ROPE_HALF) & (lane < ROPE_DIM), sin, 0.0)
    s_up = jnp.where(lane < ROPE_HALF, -sin, 0.0)

    for part, (nrm_ref, scale) in enumerate(((qn_ref, 1.0 / math.sqrt(HEAD_DIM)), (kn_ref, 1.0))):
        zc = jnp.dot(hn, w_ref[:, part * ATTN_W:(part + 1) * ATTN_W], preferred_element_type=F32)
        for half in range(ATTN_W // 256):
            zh = zc[:, half * 256:(half + 1) * 256]
            ss = jnp.dot((zh * zh).astype(BF16), g_ref[...], preferred_element_type=F32)
            y = zh * lax.rsqrt(ss * (1.0 / HEAD_DIM) + EPS) * nrm_ref[...]
            for blk in range(2):
                yb = y[:, blk * LANES:(blk + 1) * LANES]
                r = yb * c + pltpu.roll(yb, ROPE_HALF, 1) * s_dn + pltpu.roll(yb, LANES - ROPE_HALF, 1) * s_up
                col = part * ATTN_W + half * 256 + blk * LANES
                z_ref[:, col:col + LANES] = (r * scale).astype(BF16)

    zc = jnp.dot(hn, w_ref[:, 2 * ATTN_W:], preferred_element_type=F32)
    z_ref[:, 2 * ATTN_W:] = zc.astype(BF16)


def _in_proj(x2, pos2, freq, attn_norm, w_in, q_norm, k_norm):
    n, d = x2.shape
    d_in = w_in.shape[1]
    tm = min(ROW_TILE, n)
    gid = jnp.arange(256) // HEAD_DIM
    group_ones = (gid[:, None] == gid[None, :]).astype(BF16)
    qn = jnp.tile(q_norm.astype(F32), 256 // HEAD_DIM)[None, :]
    kn = jnp.tile(k_norm.astype(F32), 256 // HEAD_DIM)[None, :]
    full = lambda shape: pl.BlockSpec(shape, lambda i: (0,) * len(shape))
    return pl.pallas_call(
        _inproj_kernel,
        grid=(n // tm,),
        in_specs=[
            pl.BlockSpec((tm, d), lambda i: (i, 0)),
            pl.BlockSpec((tm, 1), lambda i: (i, 0)),
            full((1, LANES)),
            full((1, d)),
            full((d, d_in)),
            full((1, 256)),
            full((1, 256)),
            full((256, 256)),
        ],
        out_specs=pl.BlockSpec((tm, d_in), lambda i: (i, 0)),
        out_shape=jax.ShapeDtypeStruct((n, d_in), BF16),
        compiler_params=pltpu.CompilerParams(
            dimension_semantics=("arbitrary",), vmem_limit_bytes=VMEM_LIMIT),
        name="in_proj",
    )(x2, pos2, freq, attn_norm.astype(F32)[None, :], w_in.astype(BF16), qn, kn, group_ones)


def _attn_kernel(lq1_ref, lk1_ref, lq2_ref, lk2_ref, sub_ref, q_ref, k_ref, v_ref, o_ref, *,
                 tq, lambda_init):
    i = pl.program_id(2)
    lam = (jnp.exp(jnp.sum(lq1_ref[...] * lk1_ref[...], keepdims=True))
           - jnp.exp(jnp.sum(lq2_ref[...] * lk2_ref[...], keepdims=True)) + lambda_init)

    q = q_ref[...]
    lane = lax.broadcasted_iota(I32, q.shape, 1)
    zero = jnp.zeros_like(q)
    qs = jnp.concatenate([jnp.where(lane < HEAD_DIM, q, zero),
                          jnp.where(lane >= HEAD_DIM, q, zero)], axis=0)

    def step(j, carry, masked):
        m, l, acc = carry
        start = pl.multiple_of(j * tq, tq)
        k = k_ref[pl.ds(start, tq), :]
        v = v_ref[pl.ds(start, tq), :]
        s = lax.dot_general(qs, k, (((1,), (1,)), ((), ())), preferred_element_type=F32)
        if masked:
            row = lax.broadcasted_iota(I32, s.shape, 0)
            row = jnp.where(row >= tq, row - tq, row)
            col = lax.broadcasted_iota(I32, s.shape, 1)
            s = jnp.where(col <= row, s, -jnp.inf)
        m_new = jnp.maximum(m, jnp.max(s, axis=1, keepdims=True))
        alpha = jnp.exp(m - m_new)
        p = jnp.exp(s - m_new)
        l = alpha * l + jnp.sum(p, axis=1, keepdims=True)
        acc = alpha * acc + jnp.dot(p.astype(BF16), v, preferred_element_type=F32)
        return m_new, l, acc

    init = (jnp.full((2 * tq, 1), -jnp.inf, F32), jnp.zeros((2 * tq, 1), F32),
            jnp.zeros((2 * tq, V_DIM), F32))
    carry = lax.fori_loop(0, i, lambda j, cr: step(j, cr, False), init)
    _, l, acc = step(i, carry, True)
    o = acc / l
    o = o[:tq] - lam * o[tq:]
    o = o * lax.rsqrt(jnp.mean(o * o, axis=-1, keepdims=True) + EPS) * sub_ref[...] * (1.0 - lambda_init)
    o_ref[...] = o.astype(o_ref.dtype)


def _attention(z, batch, seq, lam_q1, lam_k1, lam_q2, lam_k2, subln, lambda_init):
    n = z.shape[0]
    tq = min(Q_TILE, seq)
    nq = seq // tq
    kcol = ATTN_W // V_DIM
    vec = lambda: pl.BlockSpec((1, HEAD_DIM), lambda b, h, i: (0, 0))
    return pl.pallas_call(
        functools.partial(_attn_kernel, tq=tq, lambda_init=lambda_init),
        grid=(batch, N_HEADS, nq),
        in_specs=[
            vec(), vec(), vec(), vec(),
            pl.BlockSpec((1, V_DIM), lambda b, h, i: (0, 0)),
            pl.BlockSpec((tq, V_DIM), lambda b, h, i: (b * nq + i, h)),
            pl.BlockSpec((seq, V_DIM), lambda b, h, i: (b, kcol + h)),
            pl.BlockSpec((seq, V_DIM), lambda b, h, i: (b, 2 * kcol + h)),
        ],
        out_specs=pl.BlockSpec((tq, V_DIM), lambda b, h, i: (b * nq + i, h)),
        out_shape=jax.ShapeDtypeStruct((n, ATTN_W), BF16),
        compiler_params=pltpu.CompilerParams(
            dimension_semantics=("arbitrary", "arbitrary", "arbitrary"), vmem_limit_bytes=VMEM_LIMIT),
        name="diff_attn",
    )(lam_q1.astype(F32)[None, :], lam_k1.astype(F32)[None, :], lam_q2.astype(F32)[None, :],
      lam_k2.astype(F32)[None, :], subln.astype(F32)[None, :], z, z, z)


def _mix_kernel(x_ref, a_ref, u_ref, wpool_ref, pscale_ref, wout_ref, fn_ref, wr_ref, br_ref,
                h1_ref, xn_ref, eidx_ref, gate_ref, rank_ref, cnt_ref,
                ubuf, carry_ref, *, tm, tiles_per_seq):
    i = pl.program_id(0)

    @pl.when(i == 0)
    def _():
        carry_ref[...] = jnp.zeros_like(carry_ref)

    @pl.when(i % tiles_per_seq == 0)
    def _():
        ubuf[0:POOL_HALO, :] = jnp.zeros((POOL_HALO, POOL_W), F32)

    ubuf[POOL_HALO:POOL_HALO + tm, :] = u_ref[...].astype(F32)

    t_seq = (i % tiles_per_seq) * tm + lax.broadcasted_iota(I32, (tm, 1), 0)
    mixed = jnp.dot(a_ref[...], wout_ref[0:ATTN_W, :], preferred_element_type=F32)
    for g, w in enumerate(POOL_WINDOWS):
        cols = slice(g * POOL_GROUP, (g + 1) * POOL_GROUP)
        cur = ubuf[POOL_HALO:POOL_HALO + tm, cols]
        win = cur
        for j in range(1, w):
            win = win + ubuf[POOL_HALO - j:POOL_HALO - j + tm, cols]
        count = jnp.minimum(t_seq + 1, w).astype(F32)
        pooled = win / count - cur
        pg = jnp.dot(pooled.astype(BF16), wpool_ref[g], preferred_element_type=F32) * pscale_ref[:, cols]
        mixed = mixed + jnp.dot(pg.astype(BF16), wout_ref[ATTN_W + g * POOL_GROUP:ATTN_W + (g + 1) * POOL_GROUP, :],
                                preferred_element_type=F32)
    ubuf[0:POOL_HALO, :] = ubuf[tm:tm + POOL_HALO, :]

    h1 = x_ref[...] + mixed
    h1_ref[...] = h1
    xn = h1 * lax.rsqrt(jnp.mean(h1 * h1, axis=-1, keepdims=True) + EPS) * fn_ref[...]
    half = xn.shape[1] // 2
    xn_ref[...] = _pack_bf16_pair(xn[:, :half], xn[:, half:])

    logits = jnp.dot(xn.astype(BF16), wr_ref[...], preferred_element_type=F32) + br_ref[...]
    lane = lax.broadcasted_iota(I32, logits.shape, 1).astype(F32)
    vals, idxs, hots = [], [], []
    rest = logits
    for _ in range(TOP_K):
        mx = jnp.max(rest, axis=1, keepdims=True)
        idx = jnp.min(jnp.where(rest == mx, lane, float(N_EXPERTS)), axis=1, keepdims=True)
        hot = lane == idx
        vals.append(mx)
        idxs.append(idx)
        hots.append(hot)
        rest = jnp.where(hot, -jnp.inf, rest)
    exps = [jnp.exp(v - vals[0]) for v in vals]
    denom = exps[0] + exps[1] + exps[2] + exps[3]

    chosen = jnp.zeros(logits.shape, F32)
    for hot in hots:
        chosen = chosen + hot.astype(F32)
    r = lax.broadcasted_iota(I32, (tm, tm), 0)
    cidx = lax.broadcasted_iota(I32, (tm, tm), 1)
    tri = (cidx < r).astype(BF16)
    before = jnp.dot(tri, chosen.astype(BF16), preferred_element_type=F32) + carry_ref[...]

    k_lane = lax.broadcasted_iota(I32, (tm, TOP_K), 1)
    eidx = jnp.zeros((tm, TOP_K), F32)
    gates = jnp.zeros((tm, TOP_K), F32)
    ranks = jnp.zeros((tm, TOP_K), F32)
    for k in range(TOP_K):
        rk = jnp.sum(jnp.where(hots[k], before, 0.0), axis=1, keepdims=True)
        eidx = jnp.where(k_lane == k, idxs[k], eidx)
        gates = jnp.where(k_lane == k, exps[k] / denom, gates)
        ranks = jnp.where(k_lane == k, rk, ranks)
    eidx_ref[...] = eidx.astype(I32)
    gate_ref[...] = gates
    rank_ref[...] = ranks.astype(I32)

    carry_ref[...] = carry_ref[...] + jnp.sum(chosen, axis=0, keepdims=True)
    cnt_ref[...] = carry_ref[...].astype(I32)


def _mix_and_route(x2, a_out, z, seq, w_pool, pool_scale, w_out, ffn_norm, w_router, b_router):
    n, d = x2.shape
    tm = min(ROW_TILE, seq)
    ucol = (z.shape[1] - POOL_W) // POOL_W
    full = lambda shape: pl.BlockSpec(shape, lambda i: (0,) * len(shape))
    row = lambda width: pl.BlockSpec((tm, width), lambda i: (i, 0))
    return pl.pallas_call(
        functools.partial(_mix_kernel, tm=tm, tiles_per_seq=seq // tm),
        grid=(n // tm,),
        in_specs=[
            row(d),
            row(ATTN_W),
            pl.BlockSpec((tm, POOL_W), lambda i: (i, ucol)),
            full(w_pool.shape),
            full((1, POOL_W)),
            full(w_out.shape),
            full((1, d)),
            full((d, N_EXPERTS)),
            full((1, N_EXPERTS)),
        ],
        out_specs=[row(d), row(d // 2), row(TOP_K), row(TOP_K), row(TOP_K), full((1, N_EXPERTS))],
        out_shape=[
            jax.ShapeDtypeStruct((n, d), F32),
            jax.ShapeDtypeStruct((n, d // 2), U32),
            jax.ShapeDtypeStruct((n, TOP_K), I32),
            jax.ShapeDtypeStruct((n, TOP_K), F32),
            jax.ShapeDtypeStruct((n, TOP_K), I32),
            jax.ShapeDtypeStruct((1, N_EXPERTS), I32),
        ],
        scratch_shapes=[pltpu.VMEM((POOL_HALO + tm, POOL_W), F32), pltpu.VMEM((1, N_EXPERTS), F32)],
        compiler_params=pltpu.CompilerParams(
            dimension_semantics=("arbitrary",), vmem_limit_bytes=VMEM_LIMIT),
        name="mix_route",
    )(x2, a_out, z, w_pool.astype(BF16), pool_scale.astype(F32)[None, :], w_out.astype(BF16),
      ffn_norm.astype(F32)[None, :], w_router.astype(BF16), b_router.astype(F32)[None, :])


def _dispatch_kernel(dest_ref, xn_hbm, xbuf_in, xbuf_hbm, sem, *, tt):
    del xbuf_in
    i = pl.program_id(0)
    base = i * tt

    def row_copy(src_row, dst_row):
        return pltpu.make_async_copy(xn_hbm.at[pl.ds(src_row, 1)], xbuf_hbm.at[pl.ds(dst_row, 1)], sem)

    def issue(r, _):
        for k in range(TOP_K):
            row_copy(base + r, dest_ref[0, 0, r * TOP_K + k]).start()
        return 0

    lax.fori_loop(0, tt, issue, 0)

    def drain(r, _):
        for k in range(TOP_K):
            row_copy(base + r, dest_ref[0, 0, r * TOP_K + k]).wait()
        return 0

    lax.fori_loop(0, tt, drain, 0)


def _dispatch(xn_packed, dest, n_slots):
    n, width = xn_packed.shape
    tt = min(TOKEN_TILE, n)
    nt = n // tt
    xbuf0 = jnp.zeros((n_slots, width), U32)
    return pl.pallas_call(
        functools.partial(_dispatch_kernel, tt=tt),
        grid=(nt,),
        in_specs=[
            pl.BlockSpec((1, 1, tt * TOP_K), lambda i: (i, 0, 0), memory_space=pltpu.SMEM),
            pl.BlockSpec(memory_space=pl.ANY),
            pl.BlockSpec(memory_space=pl.ANY),
        ],
        out_specs=pl.BlockSpec(memory_space=pl.ANY),
        out_shape=jax.ShapeDtypeStruct((n_slots, width), U32),
        scratch_shapes=[pltpu.SemaphoreType.DMA(())],
        input_output_aliases={2: 0},
        compiler_params=pltpu.CompilerParams(
            dimension_semantics=("arbitrary",), has_side_effects=True),
        name="dispatch",
    )(dest.reshape(nt, 1, tt * TOP_K), xn_packed, xbuf0)


def _expert_kernel(be_ref, nu_ref, x_ref, wg_ref, bg_ref, wu_ref, bu_ref, wd_ref, bd_ref, y_ref,
                   wg_bf, wu_bf, wd_bf):
    i = pl.program_id(0)
    prev = be_ref[jnp.maximum(i - 1, 0)]
    fresh = jnp.logical_or(i == 0, be_ref[i] != prev)

    @pl.when(jnp.logical_and(fresh, i < nu_ref[0]))
    def _():
        wg_bf[...] = wg_ref[0].astype(BF16)
        wu_bf[...] = wu_ref[0].astype(BF16)
        wd_bf[...] = wd_ref[0].astype(BF16)

    @pl.when(i < nu_ref[0])
    def _():
        lo, hi = _unpack_bf16_pair(x_ref[...])
        lo = lo.astype(BF16)
        hi = hi.astype(BF16)
        half = lo.shape[1]

        def proj(w_bf, b_ref):
            return (jnp.dot(lo, w_bf[0:half, :], preferred_element_type=F32)
                    + jnp.dot(hi, w_bf[half:, :], preferred_element_type=F32) + b_ref[0])

        gt = jnp.minimum(proj(wg_bf, bg_ref), SWIGLU_LIMIT)
        up = jnp.clip(proj(wu_bf, bu_ref), -SWIGLU_LIMIT, SWIGLU_LIMIT)
        hdn = (up + 1.0) * (gt * jax.nn.sigmoid(SWIGLU_ALPHA * gt))
        y = jnp.dot(hdn.astype(BF16), wd_bf[...], preferred_element_type=F32) + bd_ref[0]
        yh = y.shape[1] // 2
        y_ref[...] = _pack_bf16_pair(y[:, :yh], y[:, yh:])

    @pl.when(i >= nu_ref[0])
    def _():
        y_ref[...] = jnp.zeros_like(y_ref)


def _experts(xbuf, block_expert, n_used, w_gate, b_gate, w_up, b_up, w_down, b_down):
    n_slots, width = xbuf.shape
    ne, d, de = w_gate.shape
    te = EXPERT_TILE
    nb = n_slots // te
    last = lambda i, be, nu: jnp.minimum(i, nu[0] - 1)
    wspec = lambda shape: pl.BlockSpec((1,) + shape, lambda i, be, nu: (be[i], 0, 0))
    grid_spec = pltpu.PrefetchScalarGridSpec(
        num_scalar_prefetch=2,
        grid=(nb,),
        in_specs=[
            pl.BlockSpec((te, width), lambda i, be, nu: (last(i, be, nu), 0)),
            wspec((d, de)), wspec((1, de)),
            wspec((d, de)), wspec((1, de)),
            wspec((de, d)), wspec((1, d)),
        ],
        out_specs=pl.BlockSpec((te, d // 2), lambda i, be, nu: (i, 0)),
        scratch_shapes=[pltpu.VMEM((d, de), BF16), pltpu.VMEM((d, de), BF16), pltpu.VMEM((de, d), BF16)],
    )
    return pl.pallas_call(
        _expert_kernel,
        grid_spec=grid_spec,
        out_shape=jax.ShapeDtypeStruct((n_slots, d // 2), U32),
        compiler_params=pltpu.CompilerParams(
            dimension_semantics=("arbitrary",), vmem_limit_bytes=VMEM_LIMIT),
        name="experts",
    )(block_expert, n_used, xbuf, w_gate, b_gate.reshape(ne, 1, de), w_up, b_up.reshape(ne, 1, de),
      w_down, b_down.reshape(ne, 1, d))


def _combine_kernel(dest_ref, ybuf_hbm, h1_ref, gate_ref, p_ref, gn_ref, wg_ref, wp_ref, pn_ref, o_ref,
                    gbuf, sem, *, tt):
    def row_copy(r, k):
        return pltpu.make_async_copy(ybuf_hbm.at[pl.ds(dest_ref[0, 0, r * TOP_K + k], 1)],
                                     gbuf.at[pl.ds(k * tt + r, 1)], sem)

    def issue(r, _):
        for k in range(TOP_K):
            row_copy(r, k).start()
        return 0

    lax.fori_loop(0, tt, issue, 0)

    def drain(r, _):
        for k in range(TOP_K):
            row_copy(r, k).wait()
        return 0

    lax.fori_loop(0, tt, drain, 0)

    gates = gate_ref[...]
    moe_lo = jnp.zeros((tt, gbuf.shape[1]), F32)
    moe_hi = jnp.zeros((tt, gbuf.shape[1]), F32)
    for k in range(TOP_K):
        lo, hi = _unpack_bf16_pair(gbuf[k * tt:(k + 1) * tt, :])
        gk = gates[:, k:k + 1]
        moe_lo = moe_lo + gk * lo
        moe_hi = moe_hi + gk * hi
    h2 = h1_ref[...] + jnp.concatenate([moe_lo, moe_hi], axis=1)

    hn = h2 * lax.rsqrt(jnp.mean(h2 * h2, axis=-1, keepdims=True) + EPS) * gn_ref[...]
    gate = jax.nn.sigmoid(jnp.dot(hn.astype(BF16), wg_ref[...], preferred_element_type=F32))
    e = jnp.dot(p_ref[...].astype(BF16), wp_ref[...], preferred_element_type=F32)
    e = e * lax.rsqrt(jnp.mean(e * e, axis=-1, keepdims=True) + EPS) * pn_ref[...]
    o_ref[...] = h2 + gate * e


def _combine_ple(ybuf, dest, h1, gates, p2, ple_gate_norm, w_ple_gate, w_ple_proj, ple_post_norm):
    n, d = h1.shape
    tt = min(TOKEN_TILE, n)
    nt = n // tt
    full = lambda shape: pl.BlockSpec(shape, lambda i: (0,) * len(shape))
    row = lambda width: pl.BlockSpec((tt, width), lambda i: (i, 0))
    return pl.pallas_call(
        functools.partial(_combine_kernel, tt=tt),
        grid=(nt,),
        in_specs=[
            pl.BlockSpec((1, 1, tt * TOP_K), lambda i: (i, 0, 0), memory_space=pltpu.SMEM),
            pl.BlockSpec(memory_space=pl.ANY),
            row(d), row(TOP_K), row(p2.shape[1]),
            full((1, d)), full(w_ple_gate.shape), full(w_ple_proj.shape), full((1, d)),
        ],
        out_specs=row(d),
        out_shape=jax.ShapeDtypeStruct((n, d), F32),
        scratch_shapes=[pltpu.VMEM((TOP_K * tt, ybuf.shape[1]), U32), pltpu.SemaphoreType.DMA(())],
        compiler_params=pltpu.CompilerParams(
            dimension_semantics=("arbitrary",), vmem_limit_bytes=VMEM_LIMIT),
        name="combine_ple",
    )(dest.reshape(nt, 1, tt * TOP_K), ybuf, h1, gates, p2, ple_gate_norm.astype(F32)[None, :],
      w_ple_gate.astype(BF16), w_ple_proj.astype(BF16), ple_post_norm.astype(F32)[None, :])


def _layer(h, p_i, positions, lambda_init, attn_norm, w_in, q_norm, k_norm, lam_q1, lam_k1, lam_q2,
           lam_k2, subln, w_pool, pool_scale, w_out, ffn_norm, w_router, b_router, w_gate, b_gate,
           w_up, b_up, w_down, b_down, ple_gate_norm, w_ple_gate, w_ple_proj, ple_post_norm):
    batch, seq, d = h.shape
    n = batch * seq
    x2 = h.reshape(n, d)
    pos2 = positions.reshape(n, 1).astype(I32)

    lane = jnp.arange(LANES) % HEAD_DIM
    freqs = ROPE_THETA ** (-jnp.arange(0, ROPE_DIM, 2, dtype=F32) / ROPE_DIM)
    freq = jnp.where(lane < ROPE_DIM, freqs[lane % ROPE_HALF], 0.0).astype(F32)[None, :]

    z = _in_proj(x2, pos2, freq, attn_norm, w_in, q_norm, k_norm)
    a_out = _attention(z, batch, seq, lam_q1, lam_k1, lam_q2, lam_k2, subln, lambda_init)
    h1, xn_packed, eidx, gates, rank, counts = _mix_and_route(
        x2, a_out, z, seq, w_pool, pool_scale, w_out, ffn_norm, w_router, b_router)

    te = EXPERT_TILE
    counts = counts[0]
    padded = ((counts + te - 1) // te) * te
    pend = jnp.cumsum(padded)
    pstart = pend - padded
    dest = pstart[eidx] + rank
    nb = -(-(n * TOP_K) // te) + N_EXPERTS
    n_used = (pend[-1] // te).astype(I32)
    blk = jnp.arange(nb, dtype=I32)
    block_expert = jnp.clip(jnp.searchsorted(pend, jnp.minimum(blk, n_used - 1) * te, side="right"),
                            0, N_EXPERTS - 1).astype(I32)

    xbuf = _dispatch(xn_packed, dest, nb * te)
    ybuf = _experts(xbuf, block_expert, n_used[None], w_gate, b_gate, w_up, b_up, w_down, b_down)
    out = _combine_ple(ybuf, dest, h1, gates, p_i.reshape(n, -1), ple_gate_norm, w_ple_gate,
                       w_ple_proj, ple_post_norm)
    return out.reshape(batch, seq, d)


def kernel(x, p, positions, attn_norm, w_in, q_norm, k_norm, lam_q1, lam_k1, lam_q2, lam_k2, subln,
           w_pool, pool_scale, w_out, ffn_norm, w_router, b_router, w_gate, b_gate, w_up, b_up,
           w_down, b_down, ple_gate_norm, w_ple_gate, w_ple_proj, ple_post_norm):
    h = x
    for i in range(attn_norm.shape[0]):
        lambda_init = 0.8 - 0.6 * math.exp(-0.3 * i)
        h = _layer(h, p[i], positions, lambda_init, attn_norm[i], w_in[i], q_norm[i], k_norm[i],
                   lam_q1[i], lam_k1[i], lam_q2[i], lam_k2[i], subln[i], w_pool[i], pool_scale[i],
                   w_out[i], ffn_norm[i], w_router[i], b_router[i], w_gate[i], b_gate[i], w_up[i],
                   b_up[i], w_down[i], b_down[i], ple_gate_norm[i], w_ple_gate[i], w_ple_proj[i],
                   ple_post_norm[i])
    return h
```

```python
import functools
import math

import jax
import jax.numpy as jnp
from jax import lax
from jax.experimental import pallas as pl
from jax.experimental.pallas import tpu as pltpu

F32 = jnp.float32
BF16 = jnp.bfloat16
I32 = jnp.int32

N_HEADS = 4
HEAD_DIM = 64
V_DIM = 2 * HEAD_DIM
ATTN_W = N_HEADS * V_DIM
POOL_WINDOWS = (2, 4, 8, 16)
POOL_GROUP = 128
POOL_W = len(POOL_WINDOWS) * POOL_GROUP
POOL_HALO = 16
ROPE_DIM = HEAD_DIM // 4
ROPE_HALF = ROPE_DIM // 2
ROPE_THETA = 500000.0
N_EXPERTS = 32
TOP_K = 4
SWIGLU_LIMIT = 7.0
SWIGLU_ALPHA = 1.702
EPS = 1e-6
LANES = 128
VMEM_LIMIT = 56 * 1024 * 1024

ROW_TILE = 512
Q_TILE = 256
EXPERT_TILE = 512
TOKEN_TILE = 256


def _inproj_kernel(x_ref, pos_ref, freq_ref, an_ref, w_ref, qn_ref, kn_ref, g_ref, z_ref):
    x = x_ref[...]
    ms = jnp.mean(x * x, axis=-1, keepdims=True)
    hn = (x * lax.rsqrt(ms + EPS) * an_ref[...]).astype(BF16)

    ang = pos_ref[...].astype(F32) * freq_ref[...]
    cos = jnp.cos(ang)
    sin = jnp.sin(ang)
    lane = lax.broadcasted_iota(I32, ang.shape, 1) & (HEAD_DIM - 1)
    c = jnp.where(lane < ROPE_DIM, cos, 1.0)
    s_dn = jnp.where((lane >= ROPE_HALF) & (lane < ROPE_DIM), sin, 0.0)
    s_up = jnp.where(lane < ROPE_HALF, -sin, 0.0)

    for part, (nrm_ref, scale) in enumerate(((qn_ref, 1.0 / math.sqrt(HEAD_DIM)), (kn_ref, 1.0))):
        zc = jnp.dot(hn, w_ref[:, part * ATTN_W:(part + 1) * ATTN_W], preferred_element_type=F32)
        for half in range(ATTN_W // 256):
            zh = zc[:, half * 256:(half + 1) * 256]
            ss = jnp.dot((zh * zh).astype(BF16), g_ref[...], preferred_element_type=F32)
            y = zh * lax.rsqrt(ss * (1.0 / HEAD_DIM) + EPS) * nrm_ref[...]
            for blk in range(2):
                yb = y[:, blk * LANES:(blk + 1) * LANES]
                r = yb * c + pltpu.roll(yb, ROPE_HALF, 1) * s_dn + pltpu.roll(yb, LANES - ROPE_HALF, 1) * s_up
                col = part * ATTN_W + half * 256 + blk * LANES
                z_ref[:, col:col + LANES] = (r * scale).astype(BF16)

    zc = jnp.dot(hn, w_ref[:, 2 * ATTN_W:], preferred_element_type=F32)
    z_ref[:, 2 * ATTN_W:] = zc.astype(BF16)


def _in_proj(x2, pos2, freq, attn_norm, w_in, q_norm, k_norm):
    n, d = x2.shape
    d_in = w_in.shape[1]
    tm = min(ROW_TILE, n)
    gid = jnp.arange(256) // HEAD_DIM
    group_ones = (gid[:, None] == gid[None, :]).astype(BF16)
    qn = jnp.tile(q_norm.astype(F32), 256 // HEAD_DIM)[None, :]
    kn = jnp.tile(k_norm.astype(F32), 256 // HEAD_DIM)[None, :]
    full = lambda shape: pl.BlockSpec(shape, lambda i: (0,) * len(shape))
    return pl.pallas_call(
        _inproj_kernel,
        grid=(n // tm,),
        in_specs=[
            pl.BlockSpec((tm, d), lambda i: (i, 0)),
            pl.BlockSpec((tm, 1), lambda i: (i, 0)),
            full((1, LANES)),
            full((1, d)),
            full((d, d_in)),
            full((1, 256)),
            full((1, 256)),
            full((256, 256)),
        ],
        out_specs=pl.BlockSpec((tm, d_in), lambda i: (i, 0)),
        out_shape=jax.ShapeDtypeStruct((n, d_in), BF16),
        compiler_params=pltpu.CompilerParams(
            dimension_semantics=("arbitrary",), vmem_limit_bytes=VMEM_LIMIT),
        name="in_proj",
    )(x2, pos2, freq, attn_norm.astype(F32)[None, :], w_in.astype(BF16), qn, kn, group_ones)


def _attn_kernel(lq1_ref, lk1_ref, lq2_ref, lk2_ref, sub_ref, q_ref, k_ref, v_ref, o_ref, *,
                 tq, lambda_init):
    i = pl.program_id(2)
    lam = (jnp.exp(jnp.sum(lq1_ref[...] * lk1_ref[...], keepdims=True))
           - jnp.exp(jnp.sum(lq2_ref[...] * lk2_ref[...], keepdims=True)) + lambda_init)

    q = q_ref[...]
    lane = lax.broadcasted_iota(I32, q.shape, 1)
    zero = jnp.zeros_like(q)
    qs = jnp.concatenate([jnp.where(lane < HEAD_DIM, q, zero),
                          jnp.where(lane >= HEAD_DIM, q, zero)], axis=0)

    def step(j, carry, masked):
        m, l, acc = carry
        start = pl.multiple_of(j * tq, tq)
        k = k_ref[pl.ds(start, tq), :]
        v = v_ref[pl.ds(start, tq), :]
        s = lax.dot_general(qs, k, (((1,), (1,)), ((), ())), preferred_element_type=F32)
        if masked:
            row = lax.broadcasted_iota(I32, s.shape, 0)
            row = jnp.where(row >= tq, row - tq, row)
            col = lax.broadcasted_iota(I32, s.shape, 1)
            s = jnp.where(col <= row, s, -jnp.inf)
        m_new = jnp.maximum(m, jnp.max(s, axis=1, keepdims=True))
        alpha = jnp.exp(m - m_new)
        p = jnp.exp(s - m_new)
        l = alpha * l + jnp.sum(p, axis=1, keepdims=True)
        acc = alpha * acc + jnp.dot(p.astype(BF16), v, preferred_element_type=F32)
        return m_new, l, acc

    init = (jnp.full((2 * tq, 1), -jnp.inf, F32), jnp.zeros((2 * tq, 1), F32),
            jnp.zeros((2 * tq, V_DIM), F32))
    carry = lax.fori_loop(0, i, lambda j, cr: step(j, cr, False), init)
    _, l, acc = step(i, carry, True)
    o = acc / l
    o = o[:tq] - lam * o[tq:]
    o = o * lax.rsqrt(jnp.mean(o * o, axis=-1, keepdims=True) + EPS) * sub_ref[...] * (1.0 - lambda_init)
    o_ref[...] = o.astype(o_ref.dtype)


def _attention(z, batch, seq, lam_q1, lam_k1, lam_q2, lam_k2, subln, lambda_init):
    n = z.shape[0]
    tq = min(Q_TILE, seq)
    nq = seq // tq
    kcol = ATTN_W // V_DIM
    vec = lambda: pl.BlockSpec((1, HEAD_DIM), lambda b, h, i: (0, 0))
    return pl.pallas_call(
        functools.partial(_attn_kernel, tq=tq, lambda_init=lambda_init),
        grid=(batch, N_HEADS, nq),
        in_specs=[
            vec(), vec(), vec(), vec(),
            pl.BlockSpec((1, V_DIM), lambda b, h, i: (0, 0)),
            pl.BlockSpec((tq, V_DIM), lambda b, h, i: (b * nq + i, h)),
            pl.BlockSpec((seq, V_DIM), lambda b, h, i: (b, kcol + h)),
            pl.BlockSpec((seq, V_DIM), lambda b, h, i: (b, 2 * kcol + h)),
        ],
        out_specs=pl.BlockSpec((tq, V_DIM), lambda b, h, i: (b * nq + i, h)),
        out_shape=jax.ShapeDtypeStruct((n, ATTN_W), BF16),
        compiler_params=pltpu.CompilerParams(
            dimension_semantics=("arbitrary", "arbitrary", "arbitrary"), vmem_limit_bytes=VMEM_LIMIT),
        name="diff_attn",
    )(lam_q1.astype(F32)[None, :], lam_k1.astype(F32)[None, :], lam_q2.astype(F32)[None, :],
      lam_k2.astype(F32)[None, :], subln.astype(F32)[None, :], z, z, z)


def _mix_kernel(x_ref, a_ref, u_ref, wpool_ref, pscale_ref, wout_ref, fn_ref, wr_ref, br_ref,
                h1_ref, xn_ref, gate_ref, route_ref, cnt_ref,
                ubuf, carry_ref, *, tm, tiles_per_seq):
    i = pl.program_id(0)

    @pl.when(i == 0)
    def _():
        carry_ref[...] = jnp.zeros_like(carry_ref)

    @pl.when(i % tiles_per_seq == 0)
    def _():
        ubuf[0:POOL_HALO, :] = jnp.zeros((POOL_HALO, POOL_W), F32)

    ubuf[POOL_HALO:POOL_HALO + tm, :] = u_ref[...].astype(F32)

    t_seq = (i % tiles_per_seq) * tm + lax.broadcasted_iota(I32, (tm, 1), 0)
    mixed = jnp.dot(a_ref[...], wout_ref[0:ATTN_W, :], preferred_element_type=F32)
    for g, w in enumerate(POOL_WINDOWS):
        cols = slice(g * POOL_GROUP, (g + 1) * POOL_GROUP)
        cur = ubuf[POOL_HALO:POOL_HALO + tm, cols]
        win = cur
        for j in range(1, w):
            win = win + ubuf[POOL_HALO - j:POOL_HALO - j + tm, cols]
        count = jnp.minimum(t_seq + 1, w).astype(F32)
        pooled = win / count - cur
        pg = jnp.dot(pooled.astype(BF16), wpool_ref[g], preferred_element_type=F32) * pscale_ref[:, cols]
        mixed = mixed + jnp.dot(pg.astype(BF16), wout_ref[ATTN_W + g * POOL_GROUP:ATTN_W + (g + 1) * POOL_GROUP, :],
                                preferred_element_type=F32)
    ubuf[0:POOL_HALO, :] = ubuf[tm:tm + POOL_HALO, :]

    h1 = x_ref[...] + mixed
    h1_ref[...] = h1
    xn = h1 * lax.rsqrt(jnp.mean(h1 * h1, axis=-1, keepdims=True) + EPS) * fn_ref[...]
    xn_ref[...] = xn

    logits = jnp.dot(xn.astype(BF16), wr_ref[...], preferred_element_type=F32) + br_ref[...]
    lane = lax.broadcasted_iota(I32, logits.shape, 1).astype(F32)
    vals, idxs, hots = [], [], []
    rest = logits
    for _ in range(TOP_K):
        mx = jnp.max(rest, axis=1, keepdims=True)
        idx = jnp.min(jnp.where(rest == mx, lane, float(N_EXPERTS)), axis=1, keepdims=True)
        hot = lane == idx
        vals.append(mx)
        idxs.append(idx)
        hots.append(hot)
        rest = jnp.where(hot, -jnp.inf, rest)
    exps = [jnp.exp(v - vals[0]) for v in vals]
    denom = exps[0] + exps[1] + exps[2] + exps[3]

    chosen = jnp.zeros(logits.shape, F32)
    for hot in hots:
        chosen = chosen + hot.astype(F32)
    r = lax.broadcasted_iota(I32, (tm, tm), 0)
    cidx = lax.broadcasted_iota(I32, (tm, tm), 1)
    tri = (cidx < r).astype(BF16)
    before = jnp.dot(tri, chosen.astype(BF16), preferred_element_type=F32) + carry_ref[...]

    k_lane = lax.broadcasted_iota(I32, (tm, TOP_K), 1)
    r_lane = lax.broadcasted_iota(I32, (tm, LANES), 1)
    gates = jnp.zeros((tm, TOP_K), F32)
    route = jnp.zeros((tm, LANES), F32)
    for k in range(TOP_K):
        rk = jnp.sum(jnp.where(hots[k], before, 0.0), axis=1, keepdims=True)
        gates = jnp.where(k_lane == k, exps[k] / denom, gates)
        route = jnp.where(r_lane == k, idxs[k], route)
        route = jnp.where(r_lane == TOP_K + k, rk, route)
    gate_ref[...] = gates
    route_ref[...] = route.T[0:2 * TOP_K, :].astype(I32)

    carry_ref[...] = carry_ref[...] + jnp.sum(chosen, axis=0, keepdims=True)
    cnt_ref[...] = carry_ref[...].astype(I32)


def _mix_and_route(x2, a_out, z, seq, w_pool, pool_scale, w_out, ffn_norm, w_router, b_router):
    n, d = x2.shape
    tm = min(ROW_TILE, seq)
    ucol = (z.shape[1] - POOL_W) // POOL_W
    full = lambda shape: pl.BlockSpec(shape, lambda i: (0,) * len(shape))
    row = lambda width: pl.BlockSpec((tm, width), lambda i: (i, 0))
    return pl.pallas_call(
        functools.partial(_mix_kernel, tm=tm, tiles_per_seq=seq // tm),
        grid=(n // tm,),
        in_specs=[
            row(d),
            row(ATTN_W),
            pl.BlockSpec((tm, POOL_W), lambda i: (i, ucol)),
            full(w_pool.shape),
            full((1, POOL_W)),
            full(w_out.shape),
            full((1, d)),
            full((d, N_EXPERTS)),
            full((1, N_EXPERTS)),
        ],
        out_specs=[row(d), row(d), row(TOP_K),
                   pl.BlockSpec((2 * TOP_K, tm), lambda i: (0, i)), full((1, N_EXPERTS))],
        out_shape=[
            jax.ShapeDtypeStruct((n, d), F32),
            jax.ShapeDtypeStruct((n, d), F32),
            jax.ShapeDtypeStruct((n, TOP_K), F32),
            jax.ShapeDtypeStruct((2 * TOP_K, n), I32),
            jax.ShapeDtypeStruct((1, N_EXPERTS), I32),
        ],
        scratch_shapes=[pltpu.VMEM((POOL_HALO + tm, POOL_W), F32), pltpu.VMEM((1, N_EXPERTS), F32)],
        compiler_params=pltpu.CompilerParams(
            dimension_semantics=("arbitrary",), vmem_limit_bytes=VMEM_LIMIT),
        name="mix_route",
    )(x2, a_out, z, w_pool.astype(BF16), pool_scale.astype(F32)[None, :], w_out.astype(BF16),
      ffn_norm.astype(F32)[None, :], w_router.astype(BF16), b_router.astype(F32)[None, :])


DMA_CHUNK = 32


def _rolling_row_copies(row_copy, n_tokens, sems):
    n_chunks = n_tokens // DMA_CHUNK
    assert n_chunks >= 2 and n_chunks % 2 == 0

    def start(c, sem):
        for rr in range(DMA_CHUNK):
            for k in range(TOP_K):
                row_copy(c * DMA_CHUNK + rr, k, sem).start(priority=k % 2)

    def wait(c, sem):
        for rr in range(DMA_CHUNK):
            for k in range(TOP_K):
                row_copy(c * DMA_CHUNK + rr, k, sem).wait()

    start(0, sems.at[0])

    def pair(cc, carry):
        c = 2 * cc
        start(c + 1, sems.at[1])
        wait(c, sems.at[0])
        pl.when(c + 2 < n_chunks)(lambda: start(c + 2, sems.at[0]))
        wait(c + 1, sems.at[1])
        return carry

    lax.fori_loop(0, n_chunks // 2, pair, 0)


def _dispatch_kernel(pend_ref, padded_ref, dest_ref, x_ref, xbuf_hbm, zbuf, sems, zsem, *, tt, te, nb):
    i = pl.program_id(0)

    @pl.when(i == 0)
    def _():
        zbuf[...] = jnp.zeros_like(zbuf)
        n_used = pend_ref[N_EXPERTS - 1] // te

        def fills():
            for e in range(N_EXPERTS):
                start = pl.multiple_of(jnp.maximum(pend_ref[e] - te, 0), te)
                yield padded_ref[e] > 0, pltpu.make_async_copy(zbuf, xbuf_hbm.at[pl.ds(start, te)], zsem)
            for e in range(N_EXPERTS):
                blk = n_used + e
                start = pl.multiple_of(jnp.minimum(blk, nb - 1) * te, te)
                yield blk < nb, pltpu.make_async_copy(zbuf, xbuf_hbm.at[pl.ds(start, te)], zsem)

        for cond, cp in fills():
            pl.when(cond)(cp.start)
        for cond, cp in fills():
            pl.when(cond)(cp.wait)

    def row_copy(r, k, sem):
        return pltpu.make_async_copy(x_ref.at[pl.ds(r, 1)],
                                     xbuf_hbm.at[pl.ds(dest_ref[0, 0, k * tt + r], 1)], sem)

    _rolling_row_copies(row_copy, tt, sems)


def _tile_major(dest, tt):
    n = dest.shape[1]
    return dest.reshape(TOP_K, n // tt, tt).transpose(1, 0, 2).reshape(n // tt, 1, TOP_K * tt)


def _dispatch(xn_packed, dest, pend, padded, nb):
    n, width = xn_packed.shape
    tt = min(2 * TOKEN_TILE, n)
    te = EXPERT_TILE
    grid_spec = pltpu.PrefetchScalarGridSpec(
        num_scalar_prefetch=2,
        grid=(n // tt,),
        in_specs=[
            pl.BlockSpec((1, 1, tt * TOP_K), lambda i, pe, pa: (i, 0, 0), memory_space=pltpu.SMEM),
            pl.BlockSpec((tt, width), lambda i, pe, pa: (i, 0)),
        ],
        out_specs=pl.BlockSpec(memory_space=pl.ANY),
        scratch_shapes=[pltpu.VMEM((te, width), F32), pltpu.SemaphoreType.DMA((2,)),
                        pltpu.SemaphoreType.DMA(())],
    )
    return pl.pallas_call(
        functools.partial(_dispatch_kernel, tt=tt, te=te, nb=nb),
        grid_spec=grid_spec,
        out_shape=jax.ShapeDtypeStruct((nb * te, width), F32),
        compiler_params=pltpu.CompilerParams(
            dimension_semantics=("arbitrary",), has_side_effects=True, vmem_limit_bytes=VMEM_LIMIT),
        name="dispatch",
    )(pend, padded, _tile_major(dest, tt), xn_packed)


def _expert_kernel(be_ref, nu_ref, x_ref, wg_ref, bg_ref, wu_ref, bu_ref, wd_ref, bd_ref, y_ref,
                   wg_bf, wu_bf, wd_bf):
    i = pl.program_id(0)
    prev = be_ref[jnp.maximum(i - 1, 0)]
    fresh = jnp.logical_or(i == 0, be_ref[i] != prev)

    @pl.when(jnp.logical_and(fresh, i < nu_ref[0]))
    def _():
        wg_bf[...] = wg_ref[0].astype(BF16)
        wu_bf[...] = wu_ref[0].astype(BF16)
        wd_bf[...] = wd_ref[0].astype(BF16)

    @pl.when(i < nu_ref[0])
    def _():
        xb = x_ref[...].astype(BF16)

        def proj(w_bf, b_ref):
            return jnp.dot(xb, w_bf[...], preferred_element_type=F32) + b_ref[0]

        gt = jnp.minimum(proj(wg_bf, bg_ref), SWIGLU_LIMIT)
        up = jnp.clip(proj(wu_bf, bu_ref), -SWIGLU_LIMIT, SWIGLU_LIMIT)
        hdn = (up + 1.0) * (gt * jax.nn.sigmoid(SWIGLU_ALPHA * gt))
        y_ref[...] = jnp.dot(hdn.astype(BF16), wd_bf[...], preferred_element_type=F32) + bd_ref[0]

    @pl.when(i >= nu_ref[0])
    def _():
        y_ref[...] = jnp.zeros_like(y_ref)


def _experts(xbuf, block_expert, n_used, w_gate, b_gate, w_up, b_up, w_down, b_down):
    n_slots, width = xbuf.shape
    ne, d, de = w_gate.shape
    te = EXPERT_TILE
    nb = n_slots // te
    last = lambda i, be, nu: jnp.minimum(i, nu[0] - 1)
    wspec = lambda shape: pl.BlockSpec((1,) + shape, lambda i, be, nu: (be[i], 0, 0))
    grid_spec = pltpu.PrefetchScalarGridSpec(
        num_scalar_prefetch=2,
        grid=(nb,),
        in_specs=[
            pl.BlockSpec((te, width), lambda i, be, nu: (last(i, be, nu), 0)),
            wspec((d, de)), wspec((1, de)),
            wspec((d, de)), wspec((1, de)),
            wspec((de, d)), wspec((1, d)),
        ],
        out_specs=pl.BlockSpec((te, d), lambda i, be, nu: (i, 0)),
        scratch_shapes=[pltpu.VMEM((d, de), BF16), pltpu.VMEM((d, de), BF16), pltpu.VMEM((de, d), BF16)],
    )
    return pl.pallas_call(
        _expert_kernel,
        grid_spec=grid_spec,
        out_shape=jax.ShapeDtypeStruct((n_slots, d), F32),
        compiler_params=pltpu.CompilerParams(
            dimension_semantics=("arbitrary",), vmem_limit_bytes=VMEM_LIMIT),
        name="experts",
    )(block_expert, n_used, xbuf, w_gate, b_gate.reshape(ne, 1, de), w_up, b_up.reshape(ne, 1, de),
      w_down, b_down.reshape(ne, 1, d))


def _combine_kernel(dest_ref, ybuf_hbm, h1_ref, gate_ref, p_ref, gn_ref, wg_ref, wp_ref, pn_ref, o_ref,
                    gbuf, sems, *, tt):
    def row_copy(r, k, sem):
        return pltpu.make_async_copy(ybuf_hbm.at[pl.ds(dest_ref[0, 0, k * tt + r], 1)],
                                     gbuf.at[pl.ds(k * tt + r, 1)], sem)

    _rolling_row_copies(row_copy, tt, sems)

    gates = gate_ref[...]
    h2 = h1_ref[...]
    for k in range(TOP_K):
        h2 = h2 + gates[:, k:k + 1] * gbuf[k * tt:(k + 1) * tt, :]

    hn = h2 * lax.rsqrt(jnp.mean(h2 * h2, axis=-1, keepdims=True) + EPS) * gn_ref[...]
    gate = jax.nn.sigmoid(jnp.dot(hn.astype(BF16), wg_ref[...], preferred_element_type=F32))
    e = jnp.dot(p_ref[...].astype(BF16), wp_ref[...], preferred_element_type=F32)
    e = e * lax.rsqrt(jnp.mean(e * e, axis=-1, keepdims=True) + EPS) * pn_ref[...]
    o_ref[...] = h2 + gate * e


def _combine_ple(ybuf, dest, h1, gates, p2, ple_gate_norm, w_ple_gate, w_ple_proj, ple_post_norm):
    n, d = h1.shape
    tt = min(TOKEN_TILE, n)
    nt = n // tt
    full = lambda shape: pl.BlockSpec(shape, lambda i: (0,) * len(shape))
    row = lambda width: pl.BlockSpec((tt, width), lambda i: (i, 0))
    return pl.pallas_call(
        functools.partial(_combine_kernel, tt=tt),
        grid=(nt,),
        in_specs=[
            pl.BlockSpec((1, 1, tt * TOP_K), lambda i: (i, 0, 0), memory_space=pltpu.SMEM),
            pl.BlockSpec(memory_space=pl.ANY),
            row(d), row(TOP_K), row(p2.shape[1]),
            full((1, d)), full(w_ple_gate.shape), full(w_ple_proj.shape), full((1, d)),
        ],
        out_specs=row(d),
        out_shape=jax.ShapeDtypeStruct((n, d), F32),
        scratch_shapes=[pltpu.VMEM((TOP_K * tt, ybuf.shape[1]), F32), pltpu.SemaphoreType.DMA((2,))],
        compiler_params=pltpu.CompilerParams(
            dimension_semantics=("arbitrary",), vmem_limit_bytes=VMEM_LIMIT),
        name="combine_ple",
    )(_tile_major(dest, tt), ybuf, h1, gates, p2, ple_gate_norm.astype(F32)[None, :],
      w_ple_gate.astype(BF16), w_ple_proj.astype(BF16), ple_post_norm.astype(F32)[None, :])


def _layer(h, p_i, positions, lambda_init, attn_norm, w_in, q_norm, k_norm, lam_q1, lam_k1, lam_q2,
           lam_k2, subln, w_pool, pool_scale, w_out, ffn_norm, w_router, b_router, w_gate, b_gate,
           w_up, b_up, w_down, b_down, ple_gate_norm, w_ple_gate, w_ple_proj, ple_post_norm):
    batch, seq, d = h.shape
    n = batch * seq
    x2 = h.reshape(n, d)
    pos2 = positions.reshape(n, 1).astype(I32)

    lane = jnp.arange(LANES) % HEAD_DIM
    freqs = ROPE_THETA ** (-jnp.arange(0, ROPE_DIM, 2, dtype=F32) / ROPE_DIM)
    freq = jnp.where(lane < ROPE_DIM, freqs[lane % ROPE_HALF], 0.0).astype(F32)[None, :]

    z = _in_proj(x2, pos2, freq, attn_norm, w_in, q_norm, k_norm)
    a_out = _attention(z, batch, seq, lam_q1, lam_k1, lam_q2, lam_k2, subln, lambda_init)
    h1, xn_packed, gates, route, counts = _mix_and_route(
        x2, a_out, z, seq, w_pool, pool_scale, w_out, ffn_norm, w_router, b_router)

    te = EXPERT_TILE
    counts = counts[0]
    padded = ((counts + te - 1) // te) * te
    pend = jnp.cumsum(padded).astype(I32)
    pstart = pend - padded
    eidx, rank = route[:TOP_K], route[TOP_K:]
    experts = jnp.arange(N_EXPERTS, dtype=I32)
    dest = rank + jnp.sum(jnp.where(eidx[..., None] == experts, pstart, 0), axis=-1)
    nb = -(-(n * TOP_K) // te) + N_EXPERTS
    n_used = pend[-1] // te
    first_row = jnp.minimum(jnp.arange(nb, dtype=I32), n_used - 1) * te
    block_expert = jnp.minimum(jnp.sum(pend[None, :] <= first_row[:, None], axis=1), N_EXPERTS - 1).astype(I32)

    xbuf = _dispatch(xn_packed, dest, pend, padded.astype(I32), nb)
    ybuf = _experts(xbuf, block_expert, n_used[None], w_gate, b_gate, w_up, b_up, w_down, b_down)
    out = _combine_ple(ybuf, dest, h1, gates, p_i.reshape(n, -1), ple_gate_norm, w_ple_gate,
                       w_ple_proj, ple_post_norm)
    return out.reshape(batch, seq, d)


def kernel(x, p, positions, attn_norm, w_in, q_norm, k_norm, lam_q1, lam_k1, lam_q2, lam_k2, subln,
           w_pool, pool_scale, w_out, ffn_norm, w_router, b_router, w_gate, b_gate, w_up, b_up,
           w_down, b_down, ple_gate_norm, w_ple_gate, w_ple_proj, ple_post_norm):
    h = x
    for i in range(attn_norm.shape[0]):
        lambda_init = 0.8 - 0.6 * math.exp(-0.3 * i)
        h = _layer(h, p[i], positions, lambda_init, attn_norm[i], w_in[i], q_norm[i], k_norm[i],
                   lam_q1[i], lam_k1[i], lam_q2[i], lam_k2[i], subln[i], w_pool[i], pool_scale[i],
                   w_out[i], ffn_norm[i], w_router[i], b_router[i], w_gate[i], b_gate[i], w_up[i],
                   b_up[i], w_down[i], b_down[i], ple_gate_norm[i], w_ple_gate[i], w_ple_proj[i],
                   ple_post_norm[i])
    return h
```

```python
import functools
import math

import jax
import jax.numpy as jnp
from jax import lax
from jax.experimental import pallas as pl
from jax.experimental.pallas import tpu as pltpu

F32 = jnp.float32
BF16 = jnp.bfloat16
I32 = jnp.int32

N_HEADS = 4
HEAD_DIM = 64
V_DIM = 2 * HEAD_DIM
ATTN_W = N_HEADS * V_DIM
POOL_WINDOWS = (2, 4, 8, 16)
POOL_GROUP = 128
POOL_W = len(POOL_WINDOWS) * POOL_GROUP
POOL_HALO = 16
ROPE_DIM = HEAD_DIM // 4
ROPE_HALF = ROPE_DIM // 2
ROPE_THETA = 500000.0
N_EXPERTS = 32
TOP_K = 4
SWIGLU_LIMIT = 7.0
SWIGLU_ALPHA = 1.702
EPS = 1e-6
LANES = 128
VMEM_LIMIT = 56 * 1024 * 1024

ROW_TILE = 512
Q_TILE = 512
EXPERT_TILE = 512
TOKEN_TILE = 256


def _inproj_kernel(x_ref, pos_ref, freq_ref, an_ref, w_ref, qn_ref, kn_ref, g_ref, z_ref):
    x = x_ref[...]
    ms = jnp.mean(x * x, axis=-1, keepdims=True)
    hn = (x * lax.rsqrt(ms + EPS) * an_ref[...]).astype(BF16)

    ang = pos_ref[...].astype(F32) * freq_ref[...]
    cos = jnp.cos(ang)
    sin = jnp.sin(ang)
    lane = lax.broadcasted_iota(I32, ang.shape, 1) & (HEAD_DIM - 1)
    c = jnp.where(lane < ROPE_DIM, cos, 1.0)
    s_dn = jnp.where((lane >= ROPE_HALF) & (lane < ROPE_DIM), sin, 0.0)
    s_up = jnp.where(lane < ROPE_HALF, -sin, 0.0)

    for part, (nrm_ref, scale) in enumerate(((qn_ref, 1.0 / math.sqrt(HEAD_DIM)), (kn_ref, 1.0))):
        zc = jnp.dot(hn, w_ref[:, part * ATTN_W:(part + 1) * ATTN_W], preferred_element_type=F32)
        for half in range(ATTN_W // 256):
            zh = zc[:, half * 256:(half + 1) * 256]
            ss = jnp.dot((zh * zh).astype(BF16), g_ref[...], preferred_element_type=F32)
            y = zh * lax.rsqrt(ss * (1.0 / HEAD_DIM) + EPS) * nrm_ref[...]
            for blk in range(2):
                yb = y[:, blk * LANES:(blk + 1) * LANES]
                r = yb * c + pltpu.roll(yb, ROPE_HALF, 1) * s_dn + pltpu.roll(yb, LANES - ROPE_HALF, 1) * s_up
                col = part * ATTN_W + half * 256 + blk * LANES
                z_ref[:, col:col + LANES] = (r * scale).astype(BF16)

    zc = jnp.dot(hn, w_ref[:, 2 * ATTN_W:], preferred_element_type=F32)
    z_ref[:, 2 * ATTN_W:] = zc.astype(BF16)


def _in_proj(x2, pos2, freq, attn_norm, w_in, q_norm, k_norm):
    n, d = x2.shape
    d_in = w_in.shape[1]
    tm = min(ROW_TILE, n)
    gid = jnp.arange(256) // HEAD_DIM
    group_ones = (gid[:, None] == gid[None, :]).astype(BF16)
    qn = jnp.tile(q_norm.astype(F32), 256 // HEAD_DIM)[None, :]
    kn = jnp.tile(k_norm.astype(F32), 256 // HEAD_DIM)[None, :]
    full = lambda shape: pl.BlockSpec(shape, lambda i: (0,) * len(shape))
    return pl.pallas_call(
        _inproj_kernel,
        grid=(n // tm,),
        in_specs=[
            pl.BlockSpec((tm, d), lambda i: (i, 0)),
            pl.BlockSpec((tm, 1), lambda i: (i, 0)),
            full((1, LANES)),
            full((1, d)),
            full((d, d_in)),
            full((1, 256)),
            full((1, 256)),
            full((256, 256)),
        ],
        out_specs=pl.BlockSpec((tm, d_in), lambda i: (i, 0)),
        out_shape=jax.ShapeDtypeStruct((n, d_in), BF16),
        compiler_params=pltpu.CompilerParams(
            dimension_semantics=("arbitrary",), vmem_limit_bytes=VMEM_LIMIT),
        name="in_proj",
    )(x2, pos2, freq, attn_norm.astype(F32)[None, :], w_in.astype(BF16), qn, kn, group_ones)


def _attn_kernel(lq1_ref, lk1_ref, lq2_ref, lk2_ref, sub_ref, q_ref, k_ref, v_ref, o_ref, acc_ref, *,
                 tq, lambda_init):
    i = pl.program_id(2)
    lam = (jnp.exp(jnp.sum(lq1_ref[...] * lk1_ref[...], keepdims=True))
           - jnp.exp(jnp.sum(lq2_ref[...] * lk2_ref[...], keepdims=True)) + lambda_init)

    q = q_ref[...]
    lane = lax.broadcasted_iota(I32, q.shape, 1)
    zero = jnp.zeros_like(q)
    qs = jnp.concatenate([jnp.where(lane < HEAD_DIM, q, zero),
                          jnp.where(lane >= HEAD_DIM, q, zero)], axis=0)

    ones = jnp.ones((tq, V_DIM), BF16)
    acc_ref[...] = jnp.zeros(acc_ref.shape, F32)

    def step(j, m, masked):
        start = pl.multiple_of(j * tq, tq)
        k = k_ref[pl.ds(start, tq), :]
        v1 = jnp.concatenate([v_ref[pl.ds(start, tq), :], ones], axis=1)
        nt = (((1,), (1,)), ((), ()))
        s = jnp.concatenate([lax.dot_general(qs, k[:tq // 2], nt, preferred_element_type=F32),
                             lax.dot_general(qs, k[tq // 2:], nt, preferred_element_type=F32)],
                            axis=1)
        if masked:
            row = lax.broadcasted_iota(I32, s.shape, 0)
            row = jnp.where(row >= tq, row - tq, row)
            col = lax.broadcasted_iota(I32, s.shape, 1)
            s = jnp.where(col <= row, s, -jnp.inf)
        m_new = jnp.maximum(m, jnp.max(s, axis=1, keepdims=True))
        p = jnp.exp(s - m_new).astype(BF16)
        alpha = jnp.exp(m - m_new)
        for c in range(2):
            rows = slice(c * tq, (c + 1) * tq)
            acc_ref[rows, :] = alpha[rows] * acc_ref[rows, :] + jnp.dot(p[rows], v1, preferred_element_type=F32)
        return m_new

    m = lax.fori_loop(0, i, lambda j, m: step(j, m, False), jnp.full((2 * tq, 1), -jnp.inf, F32))
    step(i, m, True)
    o = acc_ref[:, :V_DIM] / acc_ref[:, V_DIM:]
    o = o[:tq] - lam * o[tq:]
    o = o * lax.rsqrt(jnp.mean(o * o, axis=-1, keepdims=True) + EPS) * sub_ref[...] * (1.0 - lambda_init)
    o_ref[...] = o.astype(o_ref.dtype)


def _attention(z, batch, seq, lam_q1, lam_k1, lam_q2, lam_k2, subln, lambda_init):
    n = z.shape[0]
    tq = min(Q_TILE, seq)
    nq = seq // tq
    kcol = ATTN_W // V_DIM
    vec = lambda: pl.BlockSpec((1, HEAD_DIM), lambda b, h, i: (0, 0))
    return pl.pallas_call(
        functools.partial(_attn_kernel, tq=tq, lambda_init=lambda_init),
        grid=(batch, N_HEADS, nq),
        in_specs=[
            vec(), vec(), vec(), vec(),
            pl.BlockSpec((1, V_DIM), lambda b, h, i: (0, 0)),
            pl.BlockSpec((tq, V_DIM), lambda b, h, i: (b * nq + i, h)),
            pl.BlockSpec((seq, V_DIM), lambda b, h, i: (b, kcol + h)),
            pl.BlockSpec((seq, V_DIM), lambda b, h, i: (b, 2 * kcol + h)),
        ],
        out_specs=pl.BlockSpec((tq, V_DIM), lambda b, h, i: (b * nq + i, h)),
        out_shape=jax.ShapeDtypeStruct((n, ATTN_W), BF16),
        scratch_shapes=[pltpu.VMEM((2 * tq, 2 * V_DIM), F32)],
        compiler_params=pltpu.CompilerParams(
            dimension_semantics=("arbitrary", "arbitrary", "arbitrary"), vmem_limit_bytes=VMEM_LIMIT),
        name="diff_attn",
    )(lam_q1.astype(F32)[None, :], lam_k1.astype(F32)[None, :], lam_q2.astype(F32)[None, :],
      lam_k2.astype(F32)[None, :], subln.astype(F32)[None, :], z, z, z)


def _mix_kernel(x_ref, a_ref, u_ref, wpool_ref, pscale_ref, wout_ref, fn_ref, wr_ref, br_ref,
                h1_ref, xn_ref, gate_ref, route_ref, cnt_ref,
                ubuf, carry_ref, *, tm, tiles_per_seq):
    i = pl.program_id(0)

    @pl.when(i == 0)
    def _():
        carry_ref[...] = jnp.zeros_like(carry_ref)

    @pl.when(i % tiles_per_seq == 0)
    def _():
        ubuf[0:POOL_HALO, :] = jnp.zeros((POOL_HALO, POOL_W), F32)

    ubuf[POOL_HALO:POOL_HALO + tm, :] = u_ref[...].astype(F32)

    t_seq = (i % tiles_per_seq) * tm + lax.broadcasted_iota(I32, (tm, 1), 0)
    mixed = jnp.dot(a_ref[...], wout_ref[0:ATTN_W, :], preferred_element_type=F32)
    for g, w in enumerate(POOL_WINDOWS):
        cols = slice(g * POOL_GROUP, (g + 1) * POOL_GROUP)
        cur = ubuf[POOL_HALO:POOL_HALO + tm, cols]
        win = cur
        for j in range(1, w):
            win = win + ubuf[POOL_HALO - j:POOL_HALO - j + tm, cols]
        count = jnp.minimum(t_seq + 1, w).astype(F32)
        pooled = win / count - cur
        pg = jnp.dot(pooled.astype(BF16), wpool_ref[g], preferred_element_type=F32) * pscale_ref[:, cols]
        mixed = mixed + jnp.dot(pg.astype(BF16), wout_ref[ATTN_W + g * POOL_GROUP:ATTN_W + (g + 1) * POOL_GROUP, :],
                                preferred_element_type=F32)
    ubuf[0:POOL_HALO, :] = ubuf[tm:tm + POOL_HALO, :]

    h1 = x_ref[...] + mixed
    h1_ref[...] = h1
    xn = h1 * lax.rsqrt(jnp.mean(h1 * h1, axis=-1, keepdims=True) + EPS) * fn_ref[...]
    xn_ref[...] = xn

    logits = jnp.dot(xn.astype(BF16), wr_ref[...], preferred_element_type=F32) + br_ref[...]
    lane = lax.broadcasted_iota(I32, logits.shape, 1).astype(F32)
    vals, idxs, hots = [], [], []
    rest = logits
    for _ in range(TOP_K):
        mx = jnp.max(rest, axis=1, keepdims=True)
        idx = jnp.min(jnp.where(rest == mx, lane, float(N_EXPERTS)), axis=1, keepdims=True)
        hot = lane == idx
        vals.append(mx)
        idxs.append(idx)
        hots.append(hot)
        rest = jnp.where(hot, -jnp.inf, rest)
    exps = [jnp.exp(v - vals[0]) for v in vals]
    denom = exps[0] + exps[1] + exps[2] + exps[3]

    chosen = jnp.zeros(logits.shape, F32)
    for hot in hots:
        chosen = chosen + hot.astype(F32)
    r = lax.broadcasted_iota(I32, (tm, tm), 0)
    cidx = lax.broadcasted_iota(I32, (tm, tm), 1)
    tri = (cidx < r).astype(BF16)
    before = jnp.dot(tri, chosen.astype(BF16), preferred_element_type=F32) + carry_ref[...]

    k_lane = lax.broadcasted_iota(I32, (tm, TOP_K), 1)
    r_lane = lax.broadcasted_iota(I32, (tm, LANES), 1)
    gates = jnp.zeros((tm, TOP_K), F32)
    route = jnp.zeros((tm, LANES), F32)
    for k in range(TOP_K):
        rk = jnp.sum(jnp.where(hots[k], before, 0.0), axis=1, keepdims=True)
        gates = jnp.where(k_lane == k, exps[k] / denom, gates)
        route = jnp.where(r_lane == k, idxs[k], route)
        route = jnp.where(r_lane == TOP_K + k, rk, route)
    gate_ref[...] = gates
    route_ref[...] = route.T[0:2 * TOP_K, :].astype(I32)

    carry_ref[...] = carry_ref[...] + jnp.sum(chosen, axis=0, keepdims=True)
    cnt_ref[...] = carry_ref[...].astype(I32)


def _mix_and_route(x2, a_out, z, seq, w_pool, pool_scale, w_out, ffn_norm, w_router, b_router):
    n, d = x2.shape
    tm = min(ROW_TILE, seq)
    ucol = (z.shape[1] - POOL_W) // POOL_W
    full = lambda shape: pl.BlockSpec(shape, lambda i: (0,) * len(shape))
    row = lambda width: pl.BlockSpec((tm, width), lambda i: (i, 0))
    return pl.pallas_call(
        functools.partial(_mix_kernel, tm=tm, tiles_per_seq=seq // tm),
        grid=(n // tm,),
        in_specs=[
            row(d),
            row(ATTN_W),
            pl.BlockSpec((tm, POOL_W), lambda i: (i, ucol)),
            full(w_pool.shape),
            full((1, POOL_W)),
            full(w_out.shape),
            full((1, d)),
            full((d, N_EXPERTS)),
            full((1, N_EXPERTS)),
        ],
        out_specs=[row(d), row(d), row(TOP_K),
                   pl.BlockSpec((2 * TOP_K, tm), lambda i: (0, i)), full((1, N_EXPERTS))],
        out_shape=[
            jax.ShapeDtypeStruct((n, d), F32),
            jax.ShapeDtypeStruct((n, d), F32),
            jax.ShapeDtypeStruct((n, TOP_K), F32),
            jax.ShapeDtypeStruct((2 * TOP_K, n), I32),
            jax.ShapeDtypeStruct((1, N_EXPERTS), I32),
        ],
        scratch_shapes=[pltpu.VMEM((POOL_HALO + tm, POOL_W), F32), pltpu.VMEM((1, N_EXPERTS), F32)],
        compiler_params=pltpu.CompilerParams(
            dimension_semantics=("arbitrary",), vmem_limit_bytes=VMEM_LIMIT),
        name="mix_route",
    )(x2, a_out, z, w_pool.astype(BF16), pool_scale.astype(F32)[None, :], w_out.astype(BF16),
      ffn_norm.astype(F32)[None, :], w_router.astype(BF16), b_router.astype(F32)[None, :])


DMA_CHUNK = 32


def _rolling_row_copies(row_copy, n_tokens, sems):
    n_chunks = n_tokens // DMA_CHUNK
    assert n_chunks >= 2 and n_chunks % 2 == 0

    def start(c, sem):
        for rr in range(DMA_CHUNK):
            for k in range(TOP_K):
                row_copy(c * DMA_CHUNK + rr, k, sem).start(priority=k % 2)

    def wait(c, sem):
        for rr in range(DMA_CHUNK):
            for k in range(TOP_K):
                row_copy(c * DMA_CHUNK + rr, k, sem).wait()

    start(0, sems.at[0])

    def pair(cc, carry):
        c = 2 * cc
        start(c + 1, sems.at[1])
        wait(c, sems.at[0])
        pl.when(c + 2 < n_chunks)(lambda: start(c + 2, sems.at[0]))
        wait(c + 1, sems.at[1])
        return carry

    lax.fori_loop(0, n_chunks // 2, pair, 0)


def _dispatch_kernel(pend_ref, padded_ref, dest_ref, x_ref, xbuf_hbm, zbuf, sems, zsem, *, tt, te, nb):
    i = pl.program_id(0)

    @pl.when(i == 0)
    def _():
        zbuf[...] = jnp.zeros_like(zbuf)
        n_used = pend_ref[N_EXPERTS - 1] // te

        def fills():
            for e in range(N_EXPERTS):
                start = pl.multiple_of(jnp.maximum(pend_ref[e] - te, 0), te)
                yield padded_ref[e] > 0, pltpu.make_async_copy(zbuf, xbuf_hbm.at[pl.ds(start, te)], zsem)
            for e in range(N_EXPERTS):
                blk = n_used + e
                start = pl.multiple_of(jnp.minimum(blk, nb - 1) * te, te)
                yield blk < nb, pltpu.make_async_copy(zbuf, xbuf_hbm.at[pl.ds(start, te)], zsem)

        for cond, cp in fills():
            pl.when(cond)(cp.start)
        for cond, cp in fills():
            pl.when(cond)(cp.wait)

    def row_copy(r, k, sem):
        return pltpu.make_async_copy(x_ref.at[pl.ds(r, 1)],
                                     xbuf_hbm.at[pl.ds(dest_ref[0, 0, k * tt + r], 1)], sem)

    _rolling_row_copies(row_copy, tt, sems)


def _tile_major(dest, tt):
    n = dest.shape[1]
    return dest.reshape(TOP_K, n // tt, tt).transpose(1, 0, 2).reshape(n // tt, 1, TOP_K * tt)


def _dispatch(xn_packed, dest, pend, padded, nb):
    n, width = xn_packed.shape
    tt = min(2 * TOKEN_TILE, n)
    te = EXPERT_TILE
    grid_spec = pltpu.PrefetchScalarGridSpec(
        num_scalar_prefetch=2,
        grid=(n // tt,),
        in_specs=[
            pl.BlockSpec((1, 1, tt * TOP_K), lambda i, pe, pa: (i, 0, 0), memory_space=pltpu.SMEM),
            pl.BlockSpec((tt, width), lambda i, pe, pa: (i, 0)),
        ],
        out_specs=pl.BlockSpec(memory_space=pl.ANY),
        scratch_shapes=[pltpu.VMEM((te, width), F32), pltpu.SemaphoreType.DMA((2,)),
                        pltpu.SemaphoreType.DMA(())],
    )
    return pl.pallas_call(
        functools.partial(_dispatch_kernel, tt=tt, te=te, nb=nb),
        grid_spec=grid_spec,
        out_shape=jax.ShapeDtypeStruct((nb * te, width), F32),
        compiler_params=pltpu.CompilerParams(
            dimension_semantics=("arbitrary",), has_side_effects=True, vmem_limit_bytes=VMEM_LIMIT),
        name="dispatch",
    )(pend, padded, _tile_major(dest, tt), xn_packed)


def _expert_kernel(be_ref, nu_ref, x_ref, wg_ref, bg_ref, wu_ref, bu_ref, wd_ref, bd_ref, y_ref,
                   wg_bf, wu_bf, wd_bf):
    i = pl.program_id(0)
    prev = be_ref[jnp.maximum(i - 1, 0)]
    fresh = jnp.logical_or(i == 0, be_ref[i] != prev)

    @pl.when(jnp.logical_and(fresh, i < nu_ref[0]))
    def _():
        wg_bf[...] = wg_ref[0].astype(BF16)
        wu_bf[...] = wu_ref[0].astype(BF16)
        wd_bf[...] = wd_ref[0].astype(BF16)

    @pl.when(i < nu_ref[0])
    def _():
        xb = x_ref[...].astype(BF16)

        def proj(w_bf, b_ref):
            return jnp.dot(xb, w_bf[...], preferred_element_type=F32) + b_ref[0]

        gt = jnp.minimum(proj(wg_bf, bg_ref), SWIGLU_LIMIT)
        up = jnp.clip(proj(wu_bf, bu_ref), -SWIGLU_LIMIT, SWIGLU_LIMIT)
        hdn = (up + 1.0) * (gt * jax.nn.sigmoid(SWIGLU_ALPHA * gt))
        y_ref[...] = jnp.dot(hdn.astype(BF16), wd_bf[...], preferred_element_type=F32) + bd_ref[0]

    @pl.when(i >= nu_ref[0])
    def _():
        y_ref[...] = jnp.zeros_like(y_ref)


def _experts(xbuf, block_expert, n_used, w_gate, b_gate, w_up, b_up, w_down, b_down):
    n_slots, width = xbuf.shape
    ne, d, de = w_gate.shape
    te = EXPERT_TILE
    nb = n_slots // te
    last = lambda i, be, nu: jnp.minimum(i, nu[0] - 1)
    wspec = lambda shape: pl.BlockSpec((1,) + shape, lambda i, be, nu: (be[i], 0, 0))
    grid_spec = pltpu.PrefetchScalarGridSpec(
        num_scalar_prefetch=2,
        grid=(nb,),
        in_specs=[
            pl.BlockSpec((te, width), lambda i, be, nu: (last(i, be, nu), 0)),
            wspec((d, de)), wspec((1, de)),
            wspec((d, de)), wspec((1, de)),
            wspec((de, d)), wspec((1, d)),
        ],
        out_specs=pl.BlockSpec((te, d), lambda i, be, nu: (i, 0)),
        scratch_shapes=[pltpu.VMEM((d, de), BF16), pltpu.VMEM((d, de), BF16), pltpu.VMEM((de, d), BF16)],
    )
    return pl.pallas_call(
        _expert_kernel,
        grid_spec=grid_spec,
        out_shape=jax.ShapeDtypeStruct((n_slots, d), F32),
        compiler_params=pltpu.CompilerParams(
            dimension_semantics=("arbitrary",), vmem_limit_bytes=VMEM_LIMIT),
        name="experts",
    )(block_expert, n_used, xbuf, w_gate, b_gate.reshape(ne, 1, de), w_up, b_up.reshape(ne, 1, de),
      w_down, b_down.reshape(ne, 1, d))


def _combine_kernel(dest_ref, ybuf_hbm, h1_ref, gate_ref, p_ref, gn_ref, wg_ref, wp_ref, pn_ref, o_ref,
                    gbuf, sems, *, tt):
    def row_copy(r, k, sem):
        return pltpu.make_async_copy(ybuf_hbm.at[pl.ds(dest_ref[0, 0, k * tt + r], 1)],
                                     gbuf.at[pl.ds(k * tt + r, 1)], sem)

    _rolling_row_copies(row_copy, tt, sems)

    gates = gate_ref[...]
    h2 = h1_ref[...]
    for k in range(TOP_K):
        h2 = h2 + gates[:, k:k + 1] * gbuf[k * tt:(k + 1) * tt, :]

    hn = h2 * lax.rsqrt(jnp.mean(h2 * h2, axis=-1, keepdims=True) + EPS) * gn_ref[...]
    gate = jax.nn.sigmoid(jnp.dot(hn.astype(BF16), wg_ref[...], preferred_element_type=F32))
    e = jnp.dot(p_ref[...].astype(BF16), wp_ref[...], preferred_element_type=F32)
    e = e * lax.rsqrt(jnp.mean(e * e, axis=-1, keepdims=True) + EPS) * pn_ref[...]
    o_ref[...] = h2 + gate * e


def _combine_ple(ybuf, dest, h1, gates, p2, ple_gate_norm, w_ple_gate, w_ple_proj, ple_post_norm):
    n, d = h1.shape
    tt = min(TOKEN_TILE, n)
    nt = n // tt
    full = lambda shape: pl.BlockSpec(shape, lambda i: (0,) * len(shape))
    row = lambda width: pl.BlockSpec((tt, width), lambda i: (i, 0))
    return pl.pallas_call(
        functools.partial(_combine_kernel, tt=tt),
        grid=(nt,),
        in_specs=[
            pl.BlockSpec((1, 1, tt * TOP_K), lambda i: (i, 0, 0), memory_space=pltpu.SMEM),
            pl.BlockSpec(memory_space=pl.ANY),
            row(d), row(TOP_K), row(p2.shape[1]),
            full((1, d)), full(w_ple_gate.shape), full(w_ple_proj.shape), full((1, d)),
        ],
        out_specs=row(d),
        out_shape=jax.ShapeDtypeStruct((n, d), F32),
        scratch_shapes=[pltpu.VMEM((TOP_K * tt, ybuf.shape[1]), F32), pltpu.SemaphoreType.DMA((2,))],
        compiler_params=pltpu.CompilerParams(
            dimension_semantics=("arbitrary",), vmem_limit_bytes=VMEM_LIMIT),
        name="combine_ple",
    )(_tile_major(dest, tt), ybuf, h1, gates, p2, ple_gate_norm.astype(F32)[None, :],
      w_ple_gate.astype(BF16), w_ple_proj.astype(BF16), ple_post_norm.astype(F32)[None, :])


def _layer(h, p_i, positions, lambda_init, attn_norm, w_in, q_norm, k_norm, lam_q1, lam_k1, lam_q2,
           lam_k2, subln, w_pool, pool_scale, w_out, ffn_norm, w_router, b_router, w_gate, b_gate,
           w_up, b_up, w_down, b_down, ple_gate_norm, w_ple_gate, w_ple_proj, ple_post_norm):
    batch, seq, d = h.shape
    n = batch * seq
    x2 = h.reshape(n, d)
    pos2 = positions.reshape(n, 1).astype(I32)

    lane = jnp.arange(LANES) % HEAD_DIM
    freqs = ROPE_THETA ** (-jnp.arange(0, ROPE_DIM, 2, dtype=F32) / ROPE_DIM)
    freq = jnp.where(lane < ROPE_DIM, freqs[lane % ROPE_HALF], 0.0).astype(F32)[None, :]

    z = _in_proj(x2, pos2, freq, attn_norm, w_in, q_norm, k_norm)
    a_out = _attention(z, batch, seq, lam_q1, lam_k1, lam_q2, lam_k2, subln, lambda_init)
    h1, xn_packed, gates, route, counts = _mix_and_route(
        x2, a_out, z, seq, w_pool, pool_scale, w_out, ffn_norm, w_router, b_router)

    te = EXPERT_TILE
    counts = counts[0]
    padded = ((counts + te - 1) // te) * te
    pend = jnp.cumsum(padded).astype(I32)
    pstart = pend - padded
    eidx, rank = route[:TOP_K], route[TOP_K:]
    experts = jnp.arange(N_EXPERTS, dtype=I32)
    dest = rank + jnp.sum(jnp.where(eidx[..., None] == experts, pstart, 0), axis=-1)
    nb = -(-(n * TOP_K) // te) + N_EXPERTS
    n_used = pend[-1] // te
    first_row = jnp.minimum(jnp.arange(nb, dtype=I32), n_used - 1) * te
    block_expert = jnp.minimum(jnp.sum(pend[None, :] <= first_row[:, None], axis=1), N_EXPERTS - 1).astype(I32)

    xbuf = _dispatch(xn_packed, dest, pend, padded.astype(I32), nb)
    ybuf = _experts(xbuf, block_expert, n_used[None], w_gate, b_gate, w_up, b_up, w_down, b_down)
    out = _combine_ple(ybuf, dest, h1, gates, p_i.reshape(n, -1), ple_gate_norm, w_ple_gate,
                       w_ple_proj, ple_post_norm)
    return out.reshape(batch, seq, d)


def kernel(x, p, positions, attn_norm, w_in, q_norm, k_norm, lam_q1, lam_k1, lam_q2, lam_k2, subln,
           w_pool, pool_scale, w_out, ffn_norm, w_router, b_router, w_gate, b_gate, w_up, b_up,
           w_down, b_down, ple_gate_norm, w_ple_gate, w_ple_proj, ple_post_norm):
    h = x
    for i in range(attn_norm.shape[0]):
        lambda_init = 0.8 - 0.6 * math.exp(-0.3 * i)
        h = _layer(h, p[i], positions, lambda_init, attn_norm[i], w_in[i], q_norm[i], k_norm[i],
                   lam_q1[i], lam_k1[i], lam_q2[i], lam_k2[i], subln[i], w_pool[i], pool_scale[i],
                   w_out[i], ffn_norm[i], w_router[i], b_router[i], w_gate[i], b_gate[i], w_up[i],
                   b_up[i], w_down[i], b_down[i], ple_gate_norm[i], w_ple_gate[i], w_ple_proj[i],
                   ple_post_norm[i])
    return h
```

```python
import functools
import math

import jax
import jax.numpy as jnp
from jax import lax
from jax.experimental import pallas as pl
from jax.experimental.pallas import tpu as pltpu

F32 = jnp.float32
BF16 = jnp.bfloat16
I32 = jnp.int32

N_HEADS = 4
HEAD_DIM = 64
V_DIM = 2 * HEAD_DIM
ATTN_W = N_HEADS * V_DIM
POOL_WINDOWS = (2, 4, 8, 16)
POOL_GROUP = 128
POOL_W = len(POOL_WINDOWS) * POOL_GROUP
POOL_HALO = 16
ROPE_DIM = HEAD_DIM // 4
ROPE_HALF = ROPE_DIM // 2
ROPE_THETA = 500000.0
N_EXPERTS = 32
TOP_K = 4
SWIGLU_LIMIT = 7.0
SWIGLU_ALPHA = 1.702
EPS = 1e-6
LANES = 128
MXU_DIM = 256
VMEM_LIMIT = 56 * 1024 * 1024

ROW_TILE = 512
Q_TILE = 512
EXPERT_TILE = 512
SEG_ALIGN = 8
SEG_SIZES = (512, 256, 128, 64, 32, 16, 8)
assert SEG_SIZES[0] >= ROW_TILE and SEG_SIZES[-1] == SEG_ALIGN


def _inproj_kernel(x_ref, pos_ref, freq_ref, an_ref, w_ref, qn_ref, kn_ref, g_ref, z_ref):
    x = x_ref[...]
    ms = jnp.mean(x * x, axis=-1, keepdims=True)
    hn = (x * lax.rsqrt(ms + EPS) * an_ref[...]).astype(BF16)

    ang = pos_ref[...].astype(F32) * freq_ref[...]
    cos = jnp.cos(ang)
    sin = jnp.sin(ang)
    lane = lax.broadcasted_iota(I32, ang.shape, 1) & (HEAD_DIM - 1)
    c = jnp.where(lane < ROPE_DIM, cos, 1.0)
    s_dn = jnp.where((lane >= ROPE_HALF) & (lane < ROPE_DIM), sin, 0.0)
    s_up = jnp.where(lane < ROPE_HALF, -sin, 0.0)

    for part, (nrm_ref, scale) in enumerate(((qn_ref, 1.0 / math.sqrt(HEAD_DIM)), (kn_ref, 1.0))):
        zc = jnp.dot(hn, w_ref[:, part * ATTN_W:(part + 1) * ATTN_W], preferred_element_type=F32)
        for half in range(ATTN_W // 256):
            zh = zc[:, half * 256:(half + 1) * 256]
            ss = jnp.dot((zh * zh).astype(BF16), g_ref[...], preferred_element_type=F32)
            y = zh * lax.rsqrt(ss * (1.0 / HEAD_DIM) + EPS) * nrm_ref[...]
            for blk in range(2):
                yb = y[:, blk * LANES:(blk + 1) * LANES]
                r = yb * c + pltpu.roll(yb, ROPE_HALF, 1) * s_dn + pltpu.roll(yb, LANES - ROPE_HALF, 1) * s_up
                col = part * ATTN_W + half * 256 + blk * LANES
                z_ref[:, col:col + LANES] = (r * scale).astype(BF16)

    zc = jnp.dot(hn, w_ref[:, 2 * ATTN_W:], preferred_element_type=F32)
    z_ref[:, 2 * ATTN_W:] = zc.astype(BF16)


def _in_proj(x2, pos2, freq, attn_norm, w_in, q_norm, k_norm):
    n, d = x2.shape
    d_in = w_in.shape[1]
    tm = min(ROW_TILE, n)
    gid = jnp.arange(256) // HEAD_DIM
    group_ones = (gid[:, None] == gid[None, :]).astype(BF16)
    qn = jnp.tile(q_norm.astype(F32), 256 // HEAD_DIM)[None, :]
    kn = jnp.tile(k_norm.astype(F32), 256 // HEAD_DIM)[None, :]
    full = lambda shape: pl.BlockSpec(shape, lambda i: (0,) * len(shape))
    return pl.pallas_call(
        _inproj_kernel,
        grid=(n // tm,),
        in_specs=[
            pl.BlockSpec((tm, d), lambda i: (i, 0)),
            pl.BlockSpec((tm, 1), lambda i: (i, 0)),
            full((1, LANES)),
            full((1, d)),
            full((d, d_in)),
            full((1, 256)),
            full((1, 256)),
            full((256, 256)),
        ],
        out_specs=pl.BlockSpec((tm, d_in), lambda i: (i, 0)),
        out_shape=jax.ShapeDtypeStruct((n, d_in), BF16),
        compiler_params=pltpu.CompilerParams(
            dimension_semantics=("arbitrary",), vmem_limit_bytes=VMEM_LIMIT),
        name="in_proj",
    )(x2, pos2, freq, attn_norm.astype(F32)[None, :], w_in.astype(BF16), qn, kn, group_ones)


def _attn_kernel(lq1_ref, lk1_ref, lq2_ref, lk2_ref, sub_ref, q_ref, k_ref, v_ref, o_ref, acc_ref, *,
                 tq, lambda_init):
    i = pl.program_id(2)
    lam = (jnp.exp(jnp.sum(lq1_ref[...] * lk1_ref[...], keepdims=True))
           - jnp.exp(jnp.sum(lq2_ref[...] * lk2_ref[...], keepdims=True)) + lambda_init)

    q = q_ref[...]
    lane = lax.broadcasted_iota(I32, q.shape, 1)
    zero = jnp.zeros_like(q)
    qs = jnp.concatenate([jnp.where(lane < HEAD_DIM, q, zero),
                          jnp.where(lane >= HEAD_DIM, q, zero)], axis=0)

    ones = jnp.ones((tq, V_DIM), BF16)
    acc_ref[...] = jnp.zeros(acc_ref.shape, F32)

    def step(j, m, masked):
        start = pl.multiple_of(j * tq, tq)
        k = k_ref[pl.ds(start, tq), :]
        v1 = jnp.concatenate([v_ref[pl.ds(start, tq), :], ones], axis=1)
        nt = (((1,), (1,)), ((), ()))
        s = jnp.concatenate([lax.dot_general(qs, k[:tq // 2], nt, preferred_element_type=F32),
                             lax.dot_general(qs, k[tq // 2:], nt, preferred_element_type=F32)],
                            axis=1)
        if masked:
            row = lax.broadcasted_iota(I32, s.shape, 0)
            row = jnp.where(row >= tq, row - tq, row)
            col = lax.broadcasted_iota(I32, s.shape, 1)
            s = jnp.where(col <= row, s, -jnp.inf)
        m_new = jnp.maximum(m, jnp.max(s, axis=1, keepdims=True))
        p = jnp.exp(s - m_new).astype(BF16)
        alpha = jnp.exp(m - m_new)
        for c in range(2):
            rows = slice(c * tq, (c + 1) * tq)
            acc_ref[rows, :] = alpha[rows] * acc_ref[rows, :] + jnp.dot(p[rows], v1, preferred_element_type=F32)
        return m_new

    m = lax.fori_loop(0, i, lambda j, m: step(j, m, False), jnp.full((2 * tq, 1), -jnp.inf, F32))
    step(i, m, True)
    o = acc_ref[:, :V_DIM] / acc_ref[:, V_DIM:]
    o = o[:tq] - lam * o[tq:]
    o = o * lax.rsqrt(jnp.mean(o * o, axis=-1, keepdims=True) + EPS) * sub_ref[...] * (1.0 - lambda_init)
    o_ref[...] = o.astype(o_ref.dtype)


def _attention(z, batch, seq, lam_q1, lam_k1, lam_q2, lam_k2, subln, lambda_init):
    n = z.shape[0]
    tq = min(Q_TILE, seq)
    nq = seq // tq
    kcol = ATTN_W // V_DIM
    vec = lambda: pl.BlockSpec((1, HEAD_DIM), lambda b, h, i: (0, 0))
    return pl.pallas_call(
        functools.partial(_attn_kernel, tq=tq, lambda_init=lambda_init),
        grid=(batch, N_HEADS, nq),
        in_specs=[
            vec(), vec(), vec(), vec(),
            pl.BlockSpec((1, V_DIM), lambda b, h, i: (0, 0)),
            pl.BlockSpec((tq, V_DIM), lambda b, h, i: (b * nq + i, h)),
            pl.BlockSpec((seq, V_DIM), lambda b, h, i: (b, kcol + h)),
            pl.BlockSpec((seq, V_DIM), lambda b, h, i: (b, 2 * kcol + h)),
        ],
        out_specs=pl.BlockSpec((tq, V_DIM), lambda b, h, i: (b * nq + i, h)),
        out_shape=jax.ShapeDtypeStruct((n, ATTN_W), BF16),
        scratch_shapes=[pltpu.VMEM((2 * tq, 2 * V_DIM), F32)],
        compiler_params=pltpu.CompilerParams(
            dimension_semantics=("arbitrary", "arbitrary", "arbitrary"), vmem_limit_bytes=VMEM_LIMIT),
        name="diff_attn",
    )(lam_q1.astype(F32)[None, :], lam_k1.astype(F32)[None, :], lam_q2.astype(F32)[None, :],
      lam_k2.astype(F32)[None, :], subln.astype(F32)[None, :], z, z, z)


def _mix_kernel(x_ref, a_ref, u_ref, wpool_ref, pscale_ref, wout_ref, fn_ref, wr_ref, br_ref,
                h1_ref, xn_ref, gate_ref, route_ref, cnt_ref,
                ubuf, *, tm, tiles_per_seq):
    i = pl.program_id(0)

    @pl.when(i % tiles_per_seq == 0)
    def _():
        ubuf[0:POOL_HALO, :] = jnp.zeros((POOL_HALO, POOL_W), F32)

    ubuf[POOL_HALO:POOL_HALO + tm, :] = u_ref[...].astype(F32)

    t_seq = (i % tiles_per_seq) * tm + lax.broadcasted_iota(I32, (tm, 1), 0)
    mixed = jnp.dot(a_ref[...], wout_ref[0:ATTN_W, :], preferred_element_type=F32)
    for g, w in enumerate(POOL_WINDOWS):
        cols = slice(g * POOL_GROUP, (g + 1) * POOL_GROUP)
        cur = ubuf[POOL_HALO:POOL_HALO + tm, cols]
        win = cur
        for j in range(1, w):
            win = win + ubuf[POOL_HALO - j:POOL_HALO - j + tm, cols]
        count = jnp.minimum(t_seq + 1, w).astype(F32)
        pooled = win / count - cur
        pg = jnp.dot(pooled.astype(BF16), wpool_ref[g], preferred_element_type=F32) * pscale_ref[:, cols]
        mixed = mixed + jnp.dot(pg.astype(BF16), wout_ref[ATTN_W + g * POOL_GROUP:ATTN_W + (g + 1) * POOL_GROUP, :],
                                preferred_element_type=F32)
    ubuf[0:POOL_HALO, :] = ubuf[tm:tm + POOL_HALO, :]

    h1 = x_ref[...] + mixed
    h1_ref[...] = h1
    xn = h1 * lax.rsqrt(jnp.mean(h1 * h1, axis=-1, keepdims=True) + EPS) * fn_ref[...]
    xn_ref[...] = xn

    logits = jnp.dot(xn.astype(BF16), wr_ref[...], preferred_element_type=F32) + br_ref[...]
    lane = lax.broadcasted_iota(I32, logits.shape, 1).astype(F32)
    vals, idxs, hots = [], [], []
    rest = logits
    for _ in range(TOP_K):
        mx = jnp.max(rest, axis=1, keepdims=True)
        idx = jnp.min(jnp.where(rest == mx, lane, float(N_EXPERTS)), axis=1, keepdims=True)
        hot = lane == idx
        vals.append(mx)
        idxs.append(idx)
        hots.append(hot)
        rest = jnp.where(hot, -jnp.inf, rest)
    exps = [jnp.exp(v - vals[0]) for v in vals]
    denom = exps[0] + exps[1] + exps[2] + exps[3]

    chosen = jnp.zeros(logits.shape, F32)
    for hot in hots:
        chosen = chosen + hot.astype(F32)
    r = lax.broadcasted_iota(I32, (tm, tm), 0)
    cidx = lax.broadcasted_iota(I32, (tm, tm), 1)
    tri = (cidx < r).astype(BF16)
    before = jnp.dot(tri, chosen.astype(BF16), preferred_element_type=F32)

    k_lane = lax.broadcasted_iota(I32, (tm, TOP_K), 1)
    r_lane = lax.broadcasted_iota(I32, (tm, LANES), 1)
    gates = jnp.zeros((tm, TOP_K), F32)
    route = jnp.zeros((tm, LANES), F32)
    for k in range(TOP_K):
        rk = jnp.sum(jnp.where(hots[k], before, 0.0), axis=1, keepdims=True)
        gates = jnp.where(k_lane == k, exps[k] / denom, gates)
        route = jnp.where(r_lane == k, idxs[k], route)
        route = jnp.where(r_lane == TOP_K + k, rk, route)
    gate_ref[...] = gates
    route_ref[...] = route.T[0:2 * TOP_K, :].astype(I32)
    cnt_ref[...] = jnp.broadcast_to(jnp.sum(chosen, axis=0, keepdims=True), cnt_ref.shape).astype(I32)


def _mix_and_route(x2, a_out, z, seq, w_pool, pool_scale, w_out, ffn_norm, w_router, b_router):
    n, d = x2.shape
    tm = min(ROW_TILE, seq)
    ucol = (z.shape[1] - POOL_W) // POOL_W
    full = lambda shape: pl.BlockSpec(shape, lambda i: (0,) * len(shape))
    row = lambda width: pl.BlockSpec((tm, width), lambda i: (i, 0))
    return pl.pallas_call(
        functools.partial(_mix_kernel, tm=tm, tiles_per_seq=seq // tm),
        grid=(n // tm,),
        in_specs=[
            row(d),
            row(ATTN_W),
            pl.BlockSpec((tm, POOL_W), lambda i: (i, ucol)),
            full(w_pool.shape),
            full((1, POOL_W)),
            full(w_out.shape),
            full((1, d)),
            full((d, N_EXPERTS)),
            full((1, N_EXPERTS)),
        ],
        out_specs=[row(d), row(d), row(TOP_K),
                   pl.BlockSpec((2 * TOP_K, tm), lambda i: (0, i)),
                   pl.BlockSpec((SEG_ALIGN, N_EXPERTS), lambda i: (i, 0))],
        out_shape=[
            jax.ShapeDtypeStruct((n, d), F32),
            jax.ShapeDtypeStruct((n, d), F32),
            jax.ShapeDtypeStruct((n, TOP_K), F32),
            jax.ShapeDtypeStruct((2 * TOP_K, n), I32),
            jax.ShapeDtypeStruct((n // tm * SEG_ALIGN, N_EXPERTS), I32),
        ],
        scratch_shapes=[pltpu.VMEM((POOL_HALO + tm, POOL_W), F32)],
        compiler_params=pltpu.CompilerParams(
            dimension_semantics=("arbitrary",), vmem_limit_bytes=VMEM_LIMIT),
        name="mix_route",
    )(x2, a_out, z, w_pool.astype(BF16), pool_scale.astype(F32)[None, :], w_out.astype(BF16),
      ffn_norm.astype(F32)[None, :], w_router.astype(BF16), b_router.astype(F32)[None, :])


def _sorted_rows(tile):
    rows = TOP_K * tile + N_EXPERTS * (SEG_ALIGN - 1)
    return -(-rows // MXU_DIM) * MXU_DIM


def _segment_copies(tab_ref, make_copy):
    for e in range(N_EXPERTS):
        off_l = tab_ref[0, 0, e]
        length = tab_ref[0, 0, N_EXPERTS + e]
        off_g = tab_ref[0, 0, 2 * N_EXPERTS + e]
        for size in SEG_SIZES:
            take = (length & size) != 0
            yield take, make_copy(pl.multiple_of(off_l, SEG_ALIGN), pl.multiple_of(off_g, SEG_ALIGN), size)
            step = jnp.where(take, size, 0)
            off_l = off_l + step
            off_g = off_g + step


def _dispatch_kernel(pend_ref, padded_ref, tab_ref, lpos_ref, x_ref, xbuf_hbm, sbuf, zbuf, sem, zsem, *,
                     te, nb):
    i = pl.program_id(0)

    @pl.when(i == 0)
    def _():
        zbuf[...] = jnp.zeros_like(zbuf)
        n_used = pend_ref[N_EXPERTS - 1] // te

        def fills():
            for e in range(N_EXPERTS):
                start = pl.multiple_of(jnp.maximum(pend_ref[e] - te, 0), te)
                yield padded_ref[e] > 0, pltpu.make_async_copy(zbuf, xbuf_hbm.at[pl.ds(start, te)], zsem)
            for e in range(N_EXPERTS):
                blk = n_used + e
                start = pl.multiple_of(jnp.minimum(blk, nb - 1) * te, te)
                yield blk < nb, pltpu.make_async_copy(zbuf, xbuf_hbm.at[pl.ds(start, te)], zsem)

        for cond, cp in fills():
            pl.when(cond)(cp.start)
        for cond, cp in fills():
            pl.when(cond)(cp.wait)

    rows, tile = sbuf.shape[0], x_ref.shape[0]
    pos = lax.broadcasted_iota(I32, (rows, tile), 0)
    hit = pos == lpos_ref[0:1, :]
    for k in range(1, TOP_K):
        hit = jnp.logical_or(hit, pos == lpos_ref[k:k + 1, :])
    perm = jnp.where(hit, 1.0, 0.0).astype(BF16)
    xb = x_ref[...].astype(BF16)
    half = xb.shape[1] // 2
    sbuf[:, :half] = jnp.dot(perm, xb[:, :half], preferred_element_type=F32)
    sbuf[:, half:] = jnp.dot(perm, xb[:, half:], preferred_element_type=F32)

    def piece(off_l, off_g, size):
        return pltpu.make_async_copy(sbuf.at[pl.ds(off_l, size)], xbuf_hbm.at[pl.ds(off_g, size)], sem)

    for cond, cp in _segment_copies(tab_ref, piece):
        pl.when(cond)(cp.start)
    for cond, cp in _segment_copies(tab_ref, piece):
        pl.when(cond)(cp.wait)


def _dispatch(xn, tab, lpos8, pend, padded, nb):
    n, width = xn.shape
    tile = n // tab.shape[0]
    te = EXPERT_TILE
    grid_spec = pltpu.PrefetchScalarGridSpec(
        num_scalar_prefetch=2,
        grid=(n // tile,),
        in_specs=[
            pl.BlockSpec((1, 1, 3 * N_EXPERTS), lambda i, pe, pa: (i, 0, 0), memory_space=pltpu.SMEM),
            pl.BlockSpec((2 * TOP_K, tile), lambda i, pe, pa: (0, i)),
            pl.BlockSpec((tile, width), lambda i, pe, pa: (i, 0)),
        ],
        out_specs=pl.BlockSpec(memory_space=pl.ANY),
        scratch_shapes=[pltpu.VMEM((_sorted_rows(tile), width), F32), pltpu.VMEM((te, width), F32),
                        pltpu.SemaphoreType.DMA(()), pltpu.SemaphoreType.DMA(())],
    )
    return pl.pallas_call(
        functools.partial(_dispatch_kernel, te=te, nb=nb),
        grid_spec=grid_spec,
        out_shape=jax.ShapeDtypeStruct((nb * te, width), F32),
        compiler_params=pltpu.CompilerParams(
            dimension_semantics=("arbitrary",), has_side_effects=True, vmem_limit_bytes=VMEM_LIMIT),
        name="dispatch",
    )(pend, padded, tab, lpos8, xn)


def _expert_kernel(be_ref, nu_ref, x_ref, wg_ref, bg_ref, wu_ref, bu_ref, wd_ref, bd_ref, y_ref,
                   wg_bf, wu_bf, wd_bf):
    i = pl.program_id(0)
    prev = be_ref[jnp.maximum(i - 1, 0)]
    fresh = jnp.logical_or(i == 0, be_ref[i] != prev)

    @pl.when(jnp.logical_and(fresh, i < nu_ref[0]))
    def _():
        wg_bf[...] = wg_ref[0].astype(BF16)
        wu_bf[...] = wu_ref[0].astype(BF16)
        wd_bf[...] = wd_ref[0].astype(BF16)

    @pl.when(i < nu_ref[0])
    def _():
        xb = x_ref[...].astype(BF16)

        def proj(w_bf, b_ref):
            return jnp.dot(xb, w_bf[...], preferred_element_type=F32) + b_ref[0]

        gt = jnp.minimum(proj(wg_bf, bg_ref), SWIGLU_LIMIT)
        up = jnp.clip(proj(wu_bf, bu_ref), -SWIGLU_LIMIT, SWIGLU_LIMIT)
        hdn = (up + 1.0) * (gt * jax.nn.sigmoid(SWIGLU_ALPHA * gt))
        y_ref[...] = jnp.dot(hdn.astype(BF16), wd_bf[...], preferred_element_type=F32) + bd_ref[0]

    @pl.when(i >= nu_ref[0])
    def _():
        y_ref[...] = jnp.zeros_like(y_ref)


def _experts(xbuf, block_expert, n_used, w_gate, b_gate, w_up, b_up, w_down, b_down):
    n_slots, width = xbuf.shape
    ne, d, de = w_gate.shape
    te = EXPERT_TILE
    nb = n_slots // te
    last = lambda i, be, nu: jnp.minimum(i, nu[0] - 1)
    wspec = lambda shape: pl.BlockSpec((1,) + shape, lambda i, be, nu: (be[i], 0, 0))
    grid_spec = pltpu.PrefetchScalarGridSpec(
        num_scalar_prefetch=2,
        grid=(nb,),
        in_specs=[
            pl.BlockSpec((te, width), lambda i, be, nu: (last(i, be, nu), 0)),
            wspec((d, de)), wspec((1, de)),
            wspec((d, de)), wspec((1, de)),
            wspec((de, d)), wspec((1, d)),
        ],
        out_specs=pl.BlockSpec((te, d), lambda i, be, nu: (i, 0)),
        scratch_shapes=[pltpu.VMEM((d, de), BF16), pltpu.VMEM((d, de), BF16), pltpu.VMEM((de, d), BF16)],
    )
    return pl.pallas_call(
        _expert_kernel,
        grid_spec=grid_spec,
        out_shape=jax.ShapeDtypeStruct((n_slots, d), F32),
        compiler_params=pltpu.CompilerParams(
            dimension_semantics=("arbitrary",), vmem_limit_bytes=VMEM_LIMIT),
        name="experts",
    )(block_expert, n_used, xbuf, w_gate, b_gate.reshape(ne, 1, de), w_up, b_up.reshape(ne, 1, de),
      w_down, b_down.reshape(ne, 1, d))


def _combine_kernel(tab_ref, ybuf_hbm, h1_ref, lcol_ref, gate_ref, p_ref, gn_ref, wg_ref, wp_ref, pn_ref,
                    o_ref, ysort, sem):
    @pl.when(pl.program_id(0) == 0)
    def _():
        ysort[...] = jnp.zeros_like(ysort)

    def piece(off_l, off_g, size):
        return pltpu.make_async_copy(ybuf_hbm.at[pl.ds(off_g, size)], ysort.at[pl.ds(off_l, size)], sem)

    for cond, cp in _segment_copies(tab_ref, piece):
        pl.when(cond)(cp.start)
    for cond, cp in _segment_copies(tab_ref, piece):
        pl.when(cond)(cp.wait)

    tile, rows = h1_ref.shape[0], ysort.shape[0]
    pos = lax.broadcasted_iota(I32, (tile, rows), 1)
    lcol = lcol_ref[...]
    gates = gate_ref[...]
    g = jnp.zeros((tile, rows), F32)
    for k in range(TOP_K):
        g = g + jnp.where(pos == lcol[:, k:k + 1], gates[:, k:k + 1], 0.0)
    gb = g.astype(BF16)
    half = ysort.shape[1] // 2
    moe = jnp.concatenate(
        [jnp.dot(gb, ysort[:, :half].astype(BF16), preferred_element_type=F32),
         jnp.dot(gb, ysort[:, half:].astype(BF16), preferred_element_type=F32)], axis=1)
    h2 = h1_ref[...] + moe

    hn = h2 * lax.rsqrt(jnp.mean(h2 * h2, axis=-1, keepdims=True) + EPS) * gn_ref[...]
    gate = jax.nn.sigmoid(jnp.dot(hn.astype(BF16), wg_ref[...], preferred_element_type=F32))
    e = jnp.dot(p_ref[...].astype(BF16), wp_ref[...], preferred_element_type=F32)
    e = e * lax.rsqrt(jnp.mean(e * e, axis=-1, keepdims=True) + EPS) * pn_ref[...]
    o_ref[...] = h2 + gate * e


def _combine_ple(ybuf, tab, lcol, h1, gates, p2, ple_gate_norm, w_ple_gate, w_ple_proj, ple_post_norm):
    n, d = h1.shape
    tile = n // tab.shape[0]
    full = lambda shape: pl.BlockSpec(shape, lambda i: (0,) * len(shape))
    row = lambda width: pl.BlockSpec((tile, width), lambda i: (i, 0))
    return pl.pallas_call(
        _combine_kernel,
        grid=(n // tile,),
        in_specs=[
            pl.BlockSpec((1, 1, 3 * N_EXPERTS), lambda i: (i, 0, 0), memory_space=pltpu.SMEM),
            pl.BlockSpec(memory_space=pl.ANY),
            row(d), row(TOP_K), row(TOP_K), row(p2.shape[1]),
            full((1, d)), full(w_ple_gate.shape), full(w_ple_proj.shape), full((1, d)),
        ],
        out_specs=row(d),
        out_shape=jax.ShapeDtypeStruct((n, d), F32),
        scratch_shapes=[pltpu.VMEM((_sorted_rows(tile), d), F32), pltpu.SemaphoreType.DMA(())],
        compiler_params=pltpu.CompilerParams(
            dimension_semantics=("arbitrary",), vmem_limit_bytes=VMEM_LIMIT),
        name="combine_ple",
    )(tab, ybuf, h1, lcol, gates, p2, ple_gate_norm.astype(F32)[None, :],
      w_ple_gate.astype(BF16), w_ple_proj.astype(BF16), ple_post_norm.astype(F32)[None, :])


def _layer(h, p_i, positions, lambda_init, attn_norm, w_in, q_norm, k_norm, lam_q1, lam_k1, lam_q2,
           lam_k2, subln, w_pool, pool_scale, w_out, ffn_norm, w_router, b_router, w_gate, b_gate,
           w_up, b_up, w_down, b_down, ple_gate_norm, w_ple_gate, w_ple_proj, ple_post_norm):
    batch, seq, d = h.shape
    n = batch * seq
    x2 = h.reshape(n, d)
    pos2 = positions.reshape(n, 1).astype(I32)

    lane = jnp.arange(LANES) % HEAD_DIM
    freqs = ROPE_THETA ** (-jnp.arange(0, ROPE_DIM, 2, dtype=F32) / ROPE_DIM)
    freq = jnp.where(lane < ROPE_DIM, freqs[lane % ROPE_HALF], 0.0).astype(F32)[None, :]

    z = _in_proj(x2, pos2, freq, attn_norm, w_in, q_norm, k_norm)
    a_out = _attention(z, batch, seq, lam_q1, lam_k1, lam_q2, lam_k2, subln, lambda_init)
    h1, xn, gates, route, cnt8 = _mix_and_route(
        x2, a_out, z, seq, w_pool, pool_scale, w_out, ffn_norm, w_router, b_router)

    te = EXPERT_TILE
    nt = cnt8.shape[0] // SEG_ALIGN
    tile = n // nt
    cnt = cnt8.reshape(nt, SEG_ALIGN, N_EXPERTS)[:, 0, :]
    seg = (cnt + SEG_ALIGN - 1) // SEG_ALIGN * SEG_ALIGN
    lstart = jnp.cumsum(seg, axis=1) - seg
    total = jnp.sum(seg, axis=0)
    padded = (total + te - 1) // te * te
    pend = jnp.cumsum(padded).astype(I32)
    gstart = (pend - padded)[None, :] + jnp.cumsum(seg, axis=0) - seg
    tab = jnp.concatenate([lstart, seg, gstart], axis=1).astype(I32).reshape(nt, 1, 3 * N_EXPERTS)

    eidx = route[:TOP_K].reshape(TOP_K, nt, tile)
    experts = jnp.arange(N_EXPERTS, dtype=I32)
    lpos = (route[TOP_K:].reshape(TOP_K, nt, tile)
            + jnp.sum(jnp.where(eidx[..., None] == experts, lstart[None, :, None, :], 0), axis=-1))
    lpos = lpos.reshape(TOP_K, n).astype(I32)
    lpos8 = jnp.concatenate([lpos, jnp.zeros_like(lpos)], axis=0)

    nb = -(-(n * TOP_K + nt * N_EXPERTS * (SEG_ALIGN - 1) + N_EXPERTS * (te - 1)) // te)
    n_used = pend[-1] // te
    first_row = jnp.minimum(jnp.arange(nb, dtype=I32), n_used - 1) * te
    block_expert = jnp.minimum(jnp.sum(pend[None, :] <= first_row[:, None], axis=1), N_EXPERTS - 1).astype(I32)

    xbuf = _dispatch(xn, tab, lpos8, pend, padded.astype(I32), nb)
    ybuf = _experts(xbuf, block_expert, n_used[None], w_gate, b_gate, w_up, b_up, w_down, b_down)
    out = _combine_ple(ybuf, tab, lpos.T, h1, gates, p_i.reshape(n, -1), ple_gate_norm, w_ple_gate,
                       w_ple_proj, ple_post_norm)
    return out.reshape(batch, seq, d)


def kernel(x, p, positions, attn_norm, w_in, q_norm, k_norm, lam_q1, lam_k1, lam_q2, lam_k2, subln,
           w_pool, pool_scale, w_out, ffn_norm, w_router, b_router, w_gate, b_gate, w_up, b_up,
           w_down, b_down, ple_gate_norm, w_ple_gate, w_ple_proj, ple_post_norm):
    h = x
    for i in range(attn_norm.shape[0]):
        lambda_init = 0.8 - 0.6 * math.exp(-0.3 * i)
        h = _layer(h, p[i], positions, lambda_init, attn_norm[i], w_in[i], q_norm[i], k_norm[i],
                   lam_q1[i], lam_k1[i], lam_q2[i], lam_k2[i], subln[i], w_pool[i], pool_scale[i],
                   w_out[i], ffn_norm[i], w_router[i], b_router[i], w_gate[i], b_gate[i], w_up[i],
                   b_up[i], w_down[i], b_down[i], ple_gate_norm[i], w_ple_gate[i], w_ple_proj[i],
                   ple_post_norm[i])
    return h
```

```python
import functools
import math

import jax
import jax.numpy as jnp
from jax import lax
from jax.experimental import pallas as pl
from jax.experimental.pallas import tpu as pltpu

F32 = jnp.float32
BF16 = jnp.bfloat16
I32 = jnp.int32

N_HEADS = 4
HEAD_DIM = 64
V_DIM = 2 * HEAD_DIM
ATTN_W = N_HEADS * V_DIM
POOL_WINDOWS = (2, 4, 8, 16)
POOL_GROUP = 128
POOL_W = len(POOL_WINDOWS) * POOL_GROUP
POOL_HALO = 16
ROPE_DIM = HEAD_DIM // 4
ROPE_HALF = ROPE_DIM // 2
ROPE_THETA = 500000.0
N_EXPERTS = 32
TOP_K = 4
SWIGLU_LIMIT = 7.0
SWIGLU_ALPHA = 1.702
EPS = 1e-6
LANES = 128
MXU_DIM = 256
VMEM_LIMIT = 56 * 1024 * 1024

ROW_TILE = 512
Q_TILE = 512
EXPERT_TILE = 512
SEG_ALIGN = 8
SEG_SIZES = (512, 256, 128, 64, 32, 16, 8)
assert SEG_SIZES[0] >= ROW_TILE and SEG_SIZES[-1] == SEG_ALIGN


def _inproj_kernel(x_ref, pos_ref, freq_ref, an_ref, w_ref, qn_ref, kn_ref, g_ref, z_ref):
    x = x_ref[...]
    ms = jnp.mean(x * x, axis=-1, keepdims=True)
    hn = (x * lax.rsqrt(ms + EPS) * an_ref[...]).astype(BF16)

    ang = pos_ref[...].astype(F32) * freq_ref[...]
    cos = jnp.cos(ang)
    sin = jnp.sin(ang)
    lane = lax.broadcasted_iota(I32, ang.shape, 1) & (HEAD_DIM - 1)
    c = jnp.where(lane < ROPE_DIM, cos, 1.0)
    s_dn = jnp.where((lane >= ROPE_HALF) & (lane < ROPE_DIM), sin, 0.0)
    s_up = jnp.where(lane < ROPE_HALF, -sin, 0.0)

    for part, (nrm_ref, scale) in enumerate(((qn_ref, 1.0 / math.sqrt(HEAD_DIM)), (kn_ref, 1.0))):
        zc = jnp.dot(hn, w_ref[:, part * ATTN_W:(part + 1) * ATTN_W], preferred_element_type=F32)
        for half in range(ATTN_W // 256):
            zh = zc[:, half * 256:(half + 1) * 256]
            ss = jnp.dot((zh * zh).astype(BF16), g_ref[...], preferred_element_type=F32)
            y = zh * lax.rsqrt(ss * (1.0 / HEAD_DIM) + EPS) * nrm_ref[...]
            for blk in range(2):
                yb = y[:, blk * LANES:(blk + 1) * LANES]
                r = yb * c + pltpu.roll(yb, ROPE_HALF, 1) * s_dn + pltpu.roll(yb, LANES - ROPE_HALF, 1) * s_up
                col = part * ATTN_W + half * 256 + blk * LANES
                z_ref[:, col:col + LANES] = (r * scale).astype(BF16)

    zc = jnp.dot(hn, w_ref[:, 2 * ATTN_W:], preferred_element_type=F32)
    z_ref[:, 2 * ATTN_W:] = zc.astype(BF16)


def _in_proj(x2, pos2, freq, attn_norm, w_in, q_norm, k_norm):
    n, d = x2.shape
    d_in = w_in.shape[1]
    tm = min(ROW_TILE, n)
    gid = jnp.arange(256) // HEAD_DIM
    group_ones = (gid[:, None] == gid[None, :]).astype(BF16)
    qn = jnp.tile(q_norm.astype(F32), 256 // HEAD_DIM)[None, :]
    kn = jnp.tile(k_norm.astype(F32), 256 // HEAD_DIM)[None, :]
    full = lambda shape: pl.BlockSpec(shape, lambda i: (0,) * len(shape))
    return pl.pallas_call(
        _inproj_kernel,
        grid=(n // tm,),
        in_specs=[
            pl.BlockSpec((tm, d), lambda i: (i, 0)),
            pl.BlockSpec((tm, 1), lambda i: (i, 0)),
            full((1, LANES)),
            full((1, d)),
            full((d, d_in)),
            full((1, 256)),
            full((1, 256)),
            full((256, 256)),
        ],
        out_specs=pl.BlockSpec((tm, d_in), lambda i: (i, 0)),
        out_shape=jax.ShapeDtypeStruct((n, d_in), BF16),
        compiler_params=pltpu.CompilerParams(
            dimension_semantics=("arbitrary",), vmem_limit_bytes=VMEM_LIMIT),
        name="in_proj",
    )(x2, pos2, freq, attn_norm.astype(F32)[None, :], w_in.astype(BF16), qn, kn, group_ones)


def _attn_kernel(lq1_ref, lk1_ref, lq2_ref, lk2_ref, sub_ref, q_ref, k_ref, v_ref, o_ref,
                 acc_ref, sa_ref, sb_ref, *, tq, lambda_init):
    i = pl.program_id(2)
    lam = (jnp.exp(jnp.sum(lq1_ref[...] * lk1_ref[...], keepdims=True))
           - jnp.exp(jnp.sum(lq2_ref[...] * lk2_ref[...], keepdims=True)) + lambda_init)

    q = q_ref[...]
    lane = lax.broadcasted_iota(I32, q.shape, 1)
    zero = jnp.zeros_like(q)
    qs = jnp.concatenate([jnp.where(lane < HEAD_DIM, q, zero),
                          jnp.where(lane >= HEAD_DIM, q, zero)], axis=0)

    ones = jnp.ones((tq, V_DIM), BF16)
    acc_ref[...] = jnp.zeros(acc_ref.shape, F32)

    def scores(j, s_ref):
        k = k_ref[pl.ds(pl.multiple_of(j * tq, tq), tq), :]
        nt = (((1,), (1,)), ((), ()))
        s_ref[:, :tq // 2] = lax.dot_general(qs, k[:tq // 2], nt, preferred_element_type=F32)
        s_ref[:, tq // 2:] = lax.dot_general(qs, k[tq // 2:], nt, preferred_element_type=F32)

    def update(j, s_ref, m, masked=False):
        v1 = jnp.concatenate([v_ref[pl.ds(pl.multiple_of(j * tq, tq), tq), :], ones], axis=1)
        s = s_ref[...]
        if masked:
            row = lax.broadcasted_iota(I32, s.shape, 0)
            row = jnp.where(row >= tq, row - tq, row)
            col = lax.broadcasted_iota(I32, s.shape, 1)
            s = jnp.where(col <= row, s, -jnp.inf)
        m_new = jnp.maximum(m, jnp.max(s, axis=1, keepdims=True))
        p = jnp.exp(s - m_new).astype(BF16)
        alpha = jnp.exp(m - m_new)
        for c in range(2):
            rows = slice(c * tq, (c + 1) * tq)
            acc_ref[rows, :] = alpha[rows] * acc_ref[rows, :] + jnp.dot(p[rows], v1, preferred_element_type=F32)
        return m_new

    def pair(jj, m):
        j = 2 * jj
        scores(j + 1, sb_ref)
        m = update(j, sa_ref, m)
        scores(j + 2, sa_ref)
        return update(j + 1, sb_ref, m)

    scores(0, sa_ref)
    m = lax.fori_loop(0, i // 2, pair, jnp.full((2 * tq, 1), -jnp.inf, F32))

    @pl.when(i % 2 == 0)
    def _():
        update(i, sa_ref, m, masked=True)

    @pl.when(i % 2 == 1)
    def _():
        scores(i, sb_ref)
        update(i, sb_ref, update(i - 1, sa_ref, m), masked=True)
    o = acc_ref[:, :V_DIM] / acc_ref[:, V_DIM:]
    o = o[:tq] - lam * o[tq:]
    o = o * lax.rsqrt(jnp.mean(o * o, axis=-1, keepdims=True) + EPS) * sub_ref[...] * (1.0 - lambda_init)
    o_ref[...] = o.astype(o_ref.dtype)


def _attention(z, batch, seq, lam_q1, lam_k1, lam_q2, lam_k2, subln, lambda_init):
    n = z.shape[0]
    tq = min(Q_TILE, seq)
    nq = seq // tq
    kcol = ATTN_W // V_DIM
    vec = lambda: pl.BlockSpec((1, HEAD_DIM), lambda b, h, i: (0, 0))
    return pl.pallas_call(
        functools.partial(_attn_kernel, tq=tq, lambda_init=lambda_init),
        grid=(batch, N_HEADS, nq),
        in_specs=[
            vec(), vec(), vec(), vec(),
            pl.BlockSpec((1, V_DIM), lambda b, h, i: (0, 0)),
            pl.BlockSpec((tq, V_DIM), lambda b, h, i: (b * nq + i, h)),
            pl.BlockSpec((seq, V_DIM), lambda b, h, i: (b, kcol + h)),
            pl.BlockSpec((seq, V_DIM), lambda b, h, i: (b, 2 * kcol + h)),
        ],
        out_specs=pl.BlockSpec((tq, V_DIM), lambda b, h, i: (b * nq + i, h)),
        out_shape=jax.ShapeDtypeStruct((n, ATTN_W), BF16),
        scratch_shapes=[pltpu.VMEM((2 * tq, 2 * V_DIM), F32), pltpu.VMEM((2 * tq, tq), F32),
                        pltpu.VMEM((2 * tq, tq), F32)],
        compiler_params=pltpu.CompilerParams(
            dimension_semantics=("arbitrary", "arbitrary", "arbitrary"), vmem_limit_bytes=VMEM_LIMIT),
        name="diff_attn",
    )(lam_q1.astype(F32)[None, :], lam_k1.astype(F32)[None, :], lam_q2.astype(F32)[None, :],
      lam_k2.astype(F32)[None, :], subln.astype(F32)[None, :], z, z, z)


def _mix_kernel(x_ref, a_ref, u_ref, wpool_ref, pscale_ref, wout_ref, fn_ref, wr_ref, br_ref,
                h1_ref, xn_ref, gate_ref, route_ref, cnt_ref,
                ubuf, *, tm, tiles_per_seq):
    i = pl.program_id(0)

    @pl.when(i % tiles_per_seq == 0)
    def _():
        ubuf[0:POOL_HALO, :] = jnp.zeros((POOL_HALO, POOL_W), F32)

    ubuf[POOL_HALO:POOL_HALO + tm, :] = u_ref[...].astype(F32)

    t_seq = (i % tiles_per_seq) * tm + lax.broadcasted_iota(I32, (tm, 1), 0)
    mixed = jnp.dot(a_ref[...], wout_ref[0:ATTN_W, :], preferred_element_type=F32)
    for g, w in enumerate(POOL_WINDOWS):
        cols = slice(g * POOL_GROUP, (g + 1) * POOL_GROUP)
        cur = ubuf[POOL_HALO:POOL_HALO + tm, cols]
        win = cur
        for j in range(1, w):
            win = win + ubuf[POOL_HALO - j:POOL_HALO - j + tm, cols]
        count = jnp.minimum(t_seq + 1, w).astype(F32)
        pooled = win / count - cur
        pg = jnp.dot(pooled.astype(BF16), wpool_ref[g], preferred_element_type=F32) * pscale_ref[:, cols]
        mixed = mixed + jnp.dot(pg.astype(BF16), wout_ref[ATTN_W + g * POOL_GROUP:ATTN_W + (g + 1) * POOL_GROUP, :],
                                preferred_element_type=F32)
    ubuf[0:POOL_HALO, :] = ubuf[tm:tm + POOL_HALO, :]

    h1 = x_ref[...] + mixed
    h1_ref[...] = h1
    xn = h1 * lax.rsqrt(jnp.mean(h1 * h1, axis=-1, keepdims=True) + EPS) * fn_ref[...]
    xn_ref[...] = xn

    logits = jnp.dot(xn.astype(BF16), wr_ref[...], preferred_element_type=F32) + br_ref[...]
    lane = lax.broadcasted_iota(I32, logits.shape, 1).astype(F32)
    vals, idxs, hots = [], [], []
    rest = logits
    for _ in range(TOP_K):
        mx = jnp.max(rest, axis=1, keepdims=True)
        idx = jnp.min(jnp.where(rest == mx, lane, float(N_EXPERTS)), axis=1, keepdims=True)
        hot = lane == idx
        vals.append(mx)
        idxs.append(idx)
        hots.append(hot)
        rest = jnp.where(hot, -jnp.inf, rest)
    exps = [jnp.exp(v - vals[0]) for v in vals]
    denom = exps[0] + exps[1] + exps[2] + exps[3]

    chosen = jnp.zeros(logits.shape, F32)
    for hot in hots:
        chosen = chosen + hot.astype(F32)
    r = lax.broadcasted_iota(I32, (tm, tm), 0)
    cidx = lax.broadcasted_iota(I32, (tm, tm), 1)
    tri = (cidx < r).astype(BF16)
    before = jnp.dot(tri, chosen.astype(BF16), preferred_element_type=F32)

    k_lane = lax.broadcasted_iota(I32, (tm, TOP_K), 1)
    r_lane = lax.broadcasted_iota(I32, (tm, LANES), 1)
    gates = jnp.zeros((tm, TOP_K), F32)
    route = jnp.zeros((tm, LANES), F32)
    for k in range(TOP_K):
        rk = jnp.sum(jnp.where(hots[k], before, 0.0), axis=1, keepdims=True)
        gates = jnp.where(k_lane == k, exps[k] / denom, gates)
        route = jnp.where(r_lane == k, idxs[k], route)
        route = jnp.where(r_lane == TOP_K + k, rk, route)
    gate_ref[...] = gates
    route_ref[...] = route.T[0:2 * TOP_K, :].astype(I32)
    cnt_ref[...] = jnp.broadcast_to(jnp.sum(chosen, axis=0, keepdims=True), cnt_ref.shape).astype(I32)


def _mix_and_route(x2, a_out, z, seq, w_pool, pool_scale, w_out, ffn_norm, w_router, b_router):
    n, d = x2.shape
    tm = min(ROW_TILE, seq)
    ucol = (z.shape[1] - POOL_W) // POOL_W
    full = lambda shape: pl.BlockSpec(shape, lambda i: (0,) * len(shape))
    row = lambda width: pl.BlockSpec((tm, width), lambda i: (i, 0))
    return pl.pallas_call(
        functools.partial(_mix_kernel, tm=tm, tiles_per_seq=seq // tm),
        grid=(n // tm,),
        in_specs=[
            row(d),
            row(ATTN_W),
            pl.BlockSpec((tm, POOL_W), lambda i: (i, ucol)),
            full(w_pool.shape),
            full((1, POOL_W)),
            full(w_out.shape),
            full((1, d)),
            full((d, N_EXPERTS)),
            full((1, N_EXPERTS)),
        ],
        out_specs=[row(d), row(d), row(TOP_K),
                   pl.BlockSpec((2 * TOP_K, tm), lambda i: (0, i)),
                   pl.BlockSpec((SEG_ALIGN, N_EXPERTS), lambda i: (i, 0))],
        out_shape=[
            jax.ShapeDtypeStruct((n, d), F32),
            jax.ShapeDtypeStruct((n, d), F32),
            jax.ShapeDtypeStruct((n, TOP_K), F32),
            jax.ShapeDtypeStruct((2 * TOP_K, n), I32),
            jax.ShapeDtypeStruct((n // tm * SEG_ALIGN, N_EXPERTS), I32),
        ],
        scratch_shapes=[pltpu.VMEM((POOL_HALO + tm, POOL_W), F32)],
        compiler_params=pltpu.CompilerParams(
            dimension_semantics=("arbitrary",), vmem_limit_bytes=VMEM_LIMIT),
        name="mix_route",
    )(x2, a_out, z, w_pool.astype(BF16), pool_scale.astype(F32)[None, :], w_out.astype(BF16),
      ffn_norm.astype(F32)[None, :], w_router.astype(BF16), b_router.astype(F32)[None, :])


def _sorted_rows(tile):
    rows = TOP_K * tile + N_EXPERTS * (SEG_ALIGN - 1)
    return -(-rows // MXU_DIM) * MXU_DIM


def _segment_copies(tab_ref, make_copy):
    for e in range(N_EXPERTS):
        off_l = tab_ref[0, 0, e]
        length = tab_ref[0, 0, N_EXPERTS + e]
        off_g = tab_ref[0, 0, 2 * N_EXPERTS + e]
        for size in SEG_SIZES:
            take = (length & size) != 0
            yield take, make_copy(pl.multiple_of(off_l, SEG_ALIGN), pl.multiple_of(off_g, SEG_ALIGN), size)
            step = jnp.where(take, size, 0)
            off_l = off_l + step
            off_g = off_g + step


def _start_segments(tab_ref, piece):
    for cond, cp in _segment_copies(tab_ref, piece):
        pl.when(cond)(cp.start)


def _wait_segments(tab_ref, piece):
    for cond, cp in _segment_copies(tab_ref, piece):
        pl.when(cond)(cp.wait)


def _dispatch_kernel(pend_ref, padded_ref, tab_ref, tab_prev_ref, lpos_ref, x_ref, xbuf_hbm,
                     sbuf, zbuf, sems, zsem, *, te, nb):
    i = pl.program_id(0)
    slot = i % 2

    @pl.when(i == 0)
    def _():
        zbuf[...] = jnp.zeros_like(zbuf)
        n_used = pend_ref[N_EXPERTS - 1] // te

        def fills():
            for e in range(N_EXPERTS):
                start = pl.multiple_of(jnp.maximum(pend_ref[e] - te, 0), te)
                yield padded_ref[e] > 0, pltpu.make_async_copy(zbuf, xbuf_hbm.at[pl.ds(start, te)], zsem)
            for e in range(N_EXPERTS):
                blk = n_used + e
                start = pl.multiple_of(jnp.minimum(blk, nb - 1) * te, te)
                yield blk < nb, pltpu.make_async_copy(zbuf, xbuf_hbm.at[pl.ds(start, te)], zsem)

        for cond, cp in fills():
            pl.when(cond)(cp.start)
        for cond, cp in fills():
            pl.when(cond)(cp.wait)

    rows, tile = sbuf.shape[1], x_ref.shape[0]
    pos = lax.broadcasted_iota(I32, (rows, tile), 0)
    hit = pos == lpos_ref[0:1, :]
    for k in range(1, TOP_K):
        hit = jnp.logical_or(hit, pos == lpos_ref[k:k + 1, :])
    perm = jnp.where(hit, 1.0, 0.0).astype(BF16)
    xb = x_ref[...].astype(BF16)
    half = xb.shape[1] // 2
    sbuf[slot, :, :half] = jnp.dot(perm, xb[:, :half], preferred_element_type=F32)
    sbuf[slot, :, half:] = jnp.dot(perm, xb[:, half:], preferred_element_type=F32)

    def pieces(buf_slot):
        def piece(off_l, off_g, size):
            return pltpu.make_async_copy(sbuf.at[buf_slot, pl.ds(off_l, size)],
                                         xbuf_hbm.at[pl.ds(off_g, size)], sems.at[buf_slot])
        return piece

    _start_segments(tab_ref, pieces(slot))
    pl.when(i > 0)(lambda: _wait_segments(tab_prev_ref, pieces(1 - slot)))
    pl.when(i == pl.num_programs(0) - 1)(lambda: _wait_segments(tab_ref, pieces(slot)))


def _dispatch(xn, tab, lpos8, pend, padded, nb):
    n, width = xn.shape
    tile = n // tab.shape[0]
    te = EXPERT_TILE
    grid_spec = pltpu.PrefetchScalarGridSpec(
        num_scalar_prefetch=2,
        grid=(n // tile,),
        in_specs=[
            pl.BlockSpec((1, 1, 3 * N_EXPERTS), lambda i, pe, pa: (i, 0, 0), memory_space=pltpu.SMEM),
            pl.BlockSpec((1, 1, 3 * N_EXPERTS), lambda i, pe, pa: (jnp.maximum(i - 1, 0), 0, 0),
                         memory_space=pltpu.SMEM),
            pl.BlockSpec((2 * TOP_K, tile), lambda i, pe, pa: (0, i)),
            pl.BlockSpec((tile, width), lambda i, pe, pa: (i, 0)),
        ],
        out_specs=pl.BlockSpec(memory_space=pl.ANY),
        scratch_shapes=[pltpu.VMEM((2, _sorted_rows(tile), width), F32), pltpu.VMEM((te, width), F32),
                        pltpu.SemaphoreType.DMA((2,)), pltpu.SemaphoreType.DMA(())],
    )
    return pl.pallas_call(
        functools.partial(_dispatch_kernel, te=te, nb=nb),
        grid_spec=grid_spec,
        out_shape=jax.ShapeDtypeStruct((nb * te, width), F32),
        compiler_params=pltpu.CompilerParams(
            dimension_semantics=("arbitrary",), has_side_effects=True, vmem_limit_bytes=VMEM_LIMIT),
        name="dispatch",
    )(pend, padded, tab, tab, lpos8, xn)


def _expert_kernel(be_ref, nu_ref, x_ref, wg_ref, bg_ref, wu_ref, bu_ref, wd_ref, bd_ref, y_ref,
                   wg_bf, wu_bf, wd_bf):
    i = pl.program_id(0)
    prev = be_ref[jnp.maximum(i - 1, 0)]
    fresh = jnp.logical_or(i == 0, be_ref[i] != prev)

    @pl.when(jnp.logical_and(fresh, i < nu_ref[0]))
    def _():
        wg_bf[...] = wg_ref[0].astype(BF16)
        wu_bf[...] = wu_ref[0].astype(BF16)
        wd_bf[...] = wd_ref[0].astype(BF16)

    @pl.when(i < nu_ref[0])
    def _():
        xb = x_ref[...].astype(BF16)

        def proj(w_bf, b_ref):
            return jnp.dot(xb, w_bf[...], preferred_element_type=F32) + b_ref[0]

        gt = jnp.minimum(proj(wg_bf, bg_ref), SWIGLU_LIMIT)
        up = jnp.clip(proj(wu_bf, bu_ref), -SWIGLU_LIMIT, SWIGLU_LIMIT)
        hdn = (up + 1.0) * (gt * jax.nn.sigmoid(SWIGLU_ALPHA * gt))
        y_ref[...] = jnp.dot(hdn.astype(BF16), wd_bf[...], preferred_element_type=F32) + bd_ref[0]

    @pl.when(i >= nu_ref[0])
    def _():
        y_ref[...] = jnp.zeros_like(y_ref)


def _experts(xbuf, block_expert, n_used, w_gate, b_gate, w_up, b_up, w_down, b_down):
    n_slots, width = xbuf.shape
    ne, d, de = w_gate.shape
    te = EXPERT_TILE
    nb = n_slots // te
    last = lambda i, be, nu: jnp.minimum(i, nu[0] - 1)
    wspec = lambda shape: pl.BlockSpec((1,) + shape, lambda i, be, nu: (be[i], 0, 0))
    grid_spec = pltpu.PrefetchScalarGridSpec(
        num_scalar_prefetch=2,
        grid=(nb,),
        in_specs=[
            pl.BlockSpec((te, width), lambda i, be, nu: (last(i, be, nu), 0)),
            wspec((d, de)), wspec((1, de)),
            wspec((d, de)), wspec((1, de)),
            wspec((de, d)), wspec((1, d)),
        ],
        out_specs=pl.BlockSpec((te, d), lambda i, be, nu: (i, 0)),
        scratch_shapes=[pltpu.VMEM((d, de), BF16), pltpu.VMEM((d, de), BF16), pltpu.VMEM((de, d), BF16)],
    )
    return pl.pallas_call(
        _expert_kernel,
        grid_spec=grid_spec,
        out_shape=jax.ShapeDtypeStruct((n_slots, d), F32),
        compiler_params=pltpu.CompilerParams(
            dimension_semantics=("arbitrary",), vmem_limit_bytes=VMEM_LIMIT),
        name="experts",
    )(block_expert, n_used, xbuf, w_gate, b_gate.reshape(ne, 1, de), w_up, b_up.reshape(ne, 1, de),
      w_down, b_down.reshape(ne, 1, d))


def _combine_kernel(tab_ref, tab_next_ref, ybuf_hbm, h1_ref, lcol_ref, gate_ref, p_ref, gn_ref, wg_ref,
                    wp_ref, pn_ref, o_ref, ysort, sems):
    i = pl.program_id(0)
    slot = i % 2

    def pieces(buf_slot):
        def piece(off_l, off_g, size):
            return pltpu.make_async_copy(ybuf_hbm.at[pl.ds(off_g, size)],
                                         ysort.at[buf_slot, pl.ds(off_l, size)], sems.at[buf_slot])
        return piece

    @pl.when(i == 0)
    def _():
        ysort[...] = jnp.zeros_like(ysort)
        _start_segments(tab_ref, pieces(0))

    pl.when(i < pl.num_programs(0) - 1)(lambda: _start_segments(tab_next_ref, pieces(1 - slot)))
    _wait_segments(tab_ref, pieces(slot))

    tile, rows = h1_ref.shape[0], ysort.shape[1]
    pos = lax.broadcasted_iota(I32, (tile, rows), 1)
    lcol = lcol_ref[...]
    gates = gate_ref[...]
    g = jnp.zeros((tile, rows), F32)
    for k in range(TOP_K):
        g = g + jnp.where(pos == lcol[:, k:k + 1], gates[:, k:k + 1], 0.0)
    gb = g.astype(BF16)
    half = ysort.shape[2] // 2
    moe = jnp.concatenate(
        [jnp.dot(gb, ysort[slot, :, :half].astype(BF16), preferred_element_type=F32),
         jnp.dot(gb, ysort[slot, :, half:].astype(BF16), preferred_element_type=F32)], axis=1)
    h2 = h1_ref[...] + moe

    hn = h2 * lax.rsqrt(jnp.mean(h2 * h2, axis=-1, keepdims=True) + EPS) * gn_ref[...]
    gate = jax.nn.sigmoid(jnp.dot(hn.astype(BF16), wg_ref[...], preferred_element_type=F32))
    e = jnp.dot(p_ref[...].astype(BF16), wp_ref[...], preferred_element_type=F32)
    e = e * lax.rsqrt(jnp.mean(e * e, axis=-1, keepdims=True) + EPS) * pn_ref[...]
    o_ref[...] = h2 + gate * e


def _combine_ple(ybuf, tab, lcol, h1, gates, p2, ple_gate_norm, w_ple_gate, w_ple_proj, ple_post_norm):
    n, d = h1.shape
    tile = n // tab.shape[0]
    full = lambda shape: pl.BlockSpec(shape, lambda i: (0,) * len(shape))
    row = lambda width: pl.BlockSpec((tile, width), lambda i: (i, 0))
    return pl.pallas_call(
        _combine_kernel,
        grid=(n // tile,),
        in_specs=[
            pl.BlockSpec((1, 1, 3 * N_EXPERTS), lambda i: (i, 0, 0), memory_space=pltpu.SMEM),
            pl.BlockSpec((1, 1, 3 * N_EXPERTS), lambda i: (jnp.minimum(i + 1, n // tile - 1), 0, 0),
                         memory_space=pltpu.SMEM),
            pl.BlockSpec(memory_space=pl.ANY),
            row(d), row(TOP_K), row(TOP_K), row(p2.shape[1]),
            full((1, d)), full(w_ple_gate.shape), full(w_ple_proj.shape), full((1, d)),
        ],
        out_specs=row(d),
        out_shape=jax.ShapeDtypeStruct((n, d), F32),
        scratch_shapes=[pltpu.VMEM((2, _sorted_rows(tile), d), F32), pltpu.SemaphoreType.DMA((2,))],
        compiler_params=pltpu.CompilerParams(
            dimension_semantics=("arbitrary",), vmem_limit_bytes=VMEM_LIMIT),
        name="combine_ple",
    )(tab, tab, ybuf, h1, lcol, gates, p2, ple_gate_norm.astype(F32)[None, :],
      w_ple_gate.astype(BF16), w_ple_proj.astype(BF16), ple_post_norm.astype(F32)[None, :])


def _layer(h, p_i, positions, lambda_init, attn_norm, w_in, q_norm, k_norm, lam_q1, lam_k1, lam_q2,
           lam_k2, subln, w_pool, pool_scale, w_out, ffn_norm, w_router, b_router, w_gate, b_gate,
           w_up, b_up, w_down, b_down, ple_gate_norm, w_ple_gate, w_ple_proj, ple_post_norm):
    batch, seq, d = h.shape
    n = batch * seq
    x2 = h.reshape(n, d)
    pos2 = positions.reshape(n, 1).astype(I32)

    lane = jnp.arange(LANES) % HEAD_DIM
    freqs = ROPE_THETA ** (-jnp.arange(0, ROPE_DIM, 2, dtype=F32) / ROPE_DIM)
    freq = jnp.where(lane < ROPE_DIM, freqs[lane % ROPE_HALF], 0.0).astype(F32)[None, :]

    z = _in_proj(x2, pos2, freq, attn_norm, w_in, q_norm, k_norm)
    a_out = _attention(z, batch, seq, lam_q1, lam_k1, lam_q2, lam_k2, subln, lambda_init)
    h1, xn, gates, route, cnt8 = _mix_and_route(
        x2, a_out, z, seq, w_pool, pool_scale, w_out, ffn_norm, w_router, b_router)

    te = EXPERT_TILE
    nt = cnt8.shape[0] // SEG_ALIGN
    tile = n // nt
    cnt = cnt8.reshape(nt, SEG_ALIGN, N_EXPERTS)[:, 0, :]
    seg = (cnt + SEG_ALIGN - 1) // SEG_ALIGN * SEG_ALIGN
    lstart = jnp.cumsum(seg, axis=1) - seg
    total = jnp.sum(seg, axis=0)
    padded = (total + te - 1) // te * te
    pend = jnp.cumsum(padded).astype(I32)
    gstart = (pend - padded)[None, :] + jnp.cumsum(seg, axis=0) - seg
    tab = jnp.concatenate([lstart, seg, gstart], axis=1).astype(I32).reshape(nt, 1, 3 * N_EXPERTS)

    eidx = route[:TOP_K].reshape(TOP_K, nt, tile)
    experts = jnp.arange(N_EXPERTS, dtype=I32)
    lpos = (route[TOP_K:].reshape(TOP_K, nt, tile)
            + jnp.sum(jnp.where(eidx[..., None] == experts, lstart[None, :, None, :], 0), axis=-1))
    lpos = lpos.reshape(TOP_K, n).astype(I32)
    lpos8 = jnp.concatenate([lpos, jnp.zeros_like(lpos)], axis=0)

    nb = -(-(n * TOP_K + nt * N_EXPERTS * (SEG_ALIGN - 1) + N_EXPERTS * (te - 1)) // te)
    n_used = pend[-1] // te
    first_row = jnp.minimum(jnp.arange(nb, dtype=I32), n_used - 1) * te
    block_expert = jnp.minimum(jnp.sum(pend[None, :] <= first_row[:, None], axis=1), N_EXPERTS - 1).astype(I32)

    xbuf = _dispatch(xn, tab, lpos8, pend, padded.astype(I32), nb)
    ybuf = _experts(xbuf, block_expert, n_used[None], w_gate, b_gate, w_up, b_up, w_down, b_down)
    out = _combine_ple(ybuf, tab, lpos.T, h1, gates, p_i.reshape(n, -1), ple_gate_norm, w_ple_gate,
                       w_ple_proj, ple_post_norm)
    return out.reshape(batch, seq, d)


def kernel(x, p, positions, attn_norm, w_in, q_norm, k_norm, lam_q1, lam_k1, lam_q2, lam_k2, subln,
           w_pool, pool_scale, w_out, ffn_norm, w_router, b_router, w_gate, b_gate, w_up, b_up,
           w_down, b_down, ple_gate_norm, w_ple_gate, w_ple_proj, ple_post_norm):
    h = x
    for i in range(attn_norm.shape[0]):
        lambda_init = 0.8 - 0.6 * math.exp(-0.3 * i)
        h = _layer(h, p[i], positions, lambda_init, attn_norm[i], w_in[i], q_norm[i], k_norm[i],
                   lam_q1[i], lam_k1[i], lam_q2[i], lam_k2[i], subln[i], w_pool[i], pool_scale[i],
                   w_out[i], ffn_norm[i], w_router[i], b_router[i], w_gate[i], b_gate[i], w_up[i],
                   b_up[i], w_down[i], b_down[i], ple_gate_norm[i], w_ple_gate[i], w_ple_proj[i],
                   ple_post_norm[i])
    return h
```

```python
import functools
import math

import jax
import jax.numpy as jnp
from jax import lax
from jax.experimental import pallas as pl
from jax.experimental.pallas import tpu as pltpu

F32 = jnp.float32
BF16 = jnp.bfloat16
I32 = jnp.int32

N_HEADS = 4
HEAD_DIM = 64
V_DIM = 2 * HEAD_DIM
ATTN_W = N_HEADS * V_DIM
POOL_WINDOWS = (2, 4, 8, 16)
POOL_GROUP = 128
POOL_W = len(POOL_WINDOWS) * POOL_GROUP
POOL_HALO = 16
ROPE_DIM = HEAD_DIM // 4
ROPE_HALF = ROPE_DIM // 2
ROPE_THETA = 500000.0
N_EXPERTS = 32
TOP_K = 4
SWIGLU_LIMIT = 7.0
SWIGLU_ALPHA = 1.702
EPS = 1e-6
LANES = 128
MXU_DIM = 256
VMEM_LIMIT = 56 * 1024 * 1024

ROW_TILE = 512
Q_TILE = 512
EXPERT_TILE = 512
SEG_ALIGN = 8
SEG_SIZES = (512, 256, 128, 64, 32, 16, 8)
assert SEG_SIZES[0] >= ROW_TILE and SEG_SIZES[-1] == SEG_ALIGN


def _inproj_kernel(x_ref, pos_ref, freq_ref, an_ref, w_ref, qn_ref, kn_ref, g_ref, z_ref):
    x = x_ref[...]
    ms = jnp.mean(x * x, axis=-1, keepdims=True)
    hn = (x * lax.rsqrt(ms + EPS) * an_ref[...]).astype(BF16)

    ang = pos_ref[...].astype(F32) * freq_ref[...]
    cos = jnp.cos(ang)
    sin = jnp.sin(ang)
    lane = lax.broadcasted_iota(I32, ang.shape, 1) & (HEAD_DIM - 1)
    c = jnp.where(lane < ROPE_DIM, cos, 1.0)
    s_dn = jnp.where((lane >= ROPE_HALF) & (lane < ROPE_DIM), sin, 0.0)
    s_up = jnp.where(lane < ROPE_HALF, -sin, 0.0)

    for part, (nrm_ref, scale) in enumerate(((qn_ref, 1.0 / math.sqrt(HEAD_DIM)), (kn_ref, 1.0))):
        zc = jnp.dot(hn, w_ref[:, part * ATTN_W:(part + 1) * ATTN_W], preferred_element_type=F32)
        for half in range(ATTN_W // 256):
            zh = zc[:, half * 256:(half + 1) * 256]
            ss = jnp.dot((zh * zh).astype(BF16), g_ref[...], preferred_element_type=F32)
            y = zh * lax.rsqrt(ss * (1.0 / HEAD_DIM) + EPS) * nrm_ref[...]
            for blk in range(2):
                yb = y[:, blk * LANES:(blk + 1) * LANES]
                r = yb * c + pltpu.roll(yb, ROPE_HALF, 1) * s_dn + pltpu.roll(yb, LANES - ROPE_HALF, 1) * s_up
                col = part * ATTN_W + half * 256 + blk * LANES
                z_ref[:, col:col + LANES] = (r * scale).astype(BF16)

    zc = jnp.dot(hn, w_ref[:, 2 * ATTN_W:], preferred_element_type=F32)
    z_ref[:, 2 * ATTN_W:] = zc.astype(BF16)


def _in_proj(x2, pos2, freq, attn_norm, w_in, q_norm, k_norm):
    n, d = x2.shape
    d_in = w_in.shape[1]
    tm = min(ROW_TILE, n)
    gid = jnp.arange(256) // HEAD_DIM
    group_ones = (gid[:, None] == gid[None, :]).astype(BF16)
    qn = jnp.tile(q_norm.astype(F32), 256 // HEAD_DIM)[None, :]
    kn = jnp.tile(k_norm.astype(F32), 256 // HEAD_DIM)[None, :]
    full = lambda shape: pl.BlockSpec(shape, lambda i: (0,) * len(shape))
    return pl.pallas_call(
        _inproj_kernel,
        grid=(n // tm,),
        in_specs=[
            pl.BlockSpec((tm, d), lambda i: (i, 0)),
            pl.BlockSpec((tm, 1), lambda i: (i, 0)),
            full((1, LANES)),
            full((1, d)),
            full((d, d_in)),
            full((1, 256)),
            full((1, 256)),
            full((256, 256)),
        ],
        out_specs=pl.BlockSpec((tm, d_in), lambda i: (i, 0)),
        out_shape=jax.ShapeDtypeStruct((n, d_in), BF16),
        compiler_params=pltpu.CompilerParams(
            dimension_semantics=("arbitrary",), vmem_limit_bytes=VMEM_LIMIT),
        name="in_proj",
    )(x2, pos2, freq, attn_norm.astype(F32)[None, :], w_in.astype(BF16), qn, kn, group_ones)


def _attn_kernel(lq1_ref, lk1_ref, lq2_ref, lk2_ref, sub_ref, q_ref, k_ref, v_ref, o_ref,
                 acc_ref, sa_ref, sb_ref, *, tq, lambda_init):
    i = pl.program_id(2)
    lam = (jnp.exp(jnp.sum(lq1_ref[...] * lk1_ref[...], keepdims=True))
           - jnp.exp(jnp.sum(lq2_ref[...] * lk2_ref[...], keepdims=True)) + lambda_init)

    q = q_ref[...]
    lane = lax.broadcasted_iota(I32, q.shape, 1)
    zero = jnp.zeros_like(q)
    qs = jnp.concatenate([jnp.where(lane < HEAD_DIM, q, zero),
                          jnp.where(lane >= HEAD_DIM, q, zero)], axis=0)

    ones = jnp.ones((tq, V_DIM), BF16)
    acc_ref[...] = jnp.zeros(acc_ref.shape, F32)

    def scores(j, s_ref):
        k = k_ref[pl.ds(pl.multiple_of(j * tq, tq), tq), :]
        nt = (((1,), (1,)), ((), ()))
        s_ref[:, :tq // 2] = lax.dot_general(qs, k[:tq // 2], nt, preferred_element_type=F32)
        s_ref[:, tq // 2:] = lax.dot_general(qs, k[tq // 2:], nt, preferred_element_type=F32)

    def update(j, s_ref, m, masked=False):
        v1 = jnp.concatenate([v_ref[pl.ds(pl.multiple_of(j * tq, tq), tq), :], ones], axis=1)
        s = s_ref[...]
        if masked:
            row = lax.broadcasted_iota(I32, s.shape, 0)
            row = jnp.where(row >= tq, row - tq, row)
            col = lax.broadcasted_iota(I32, s.shape, 1)
            s = jnp.where(col <= row, s, -jnp.inf)
        m_new = jnp.maximum(m, jnp.max(s, axis=1, keepdims=True))
        p = jnp.exp(s - m_new).astype(BF16)
        alpha = jnp.exp(m - m_new)
        for c in range(2):
            rows = slice(c * tq, (c + 1) * tq)
            acc_ref[rows, :] = alpha[rows] * acc_ref[rows, :] + jnp.dot(p[rows], v1, preferred_element_type=F32)
        return m_new

    def pair(jj, m):
        j = 2 * jj
        scores(j + 1, sb_ref)
        m = update(j, sa_ref, m)
        scores(j + 2, sa_ref)
        return update(j + 1, sb_ref, m)

    scores(0, sa_ref)
    m = lax.fori_loop(0, i // 2, pair, jnp.full((2 * tq, 1), -jnp.inf, F32))

    @pl.when(i % 2 == 0)
    def _():
        update(i, sa_ref, m, masked=True)

    @pl.when(i % 2 == 1)
    def _():
        scores(i, sb_ref)
        update(i, sb_ref, update(i - 1, sa_ref, m), masked=True)
    o = acc_ref[:, :V_DIM] / acc_ref[:, V_DIM:]
    o = o[:tq] - lam * o[tq:]
    o = o * lax.rsqrt(jnp.mean(o * o, axis=-1, keepdims=True) + EPS) * sub_ref[...] * (1.0 - lambda_init)
    o_ref[...] = o.astype(o_ref.dtype)


def _attention(z, batch, seq, lam_q1, lam_k1, lam_q2, lam_k2, subln, lambda_init):
    n = z.shape[0]
    tq = min(Q_TILE, seq)
    nq = seq // tq
    kcol = ATTN_W // V_DIM
    vec = lambda: pl.BlockSpec((1, HEAD_DIM), lambda b, h, i: (0, 0))
    return pl.pallas_call(
        functools.partial(_attn_kernel, tq=tq, lambda_init=lambda_init),
        grid=(batch, N_HEADS, nq),
        in_specs=[
            vec(), vec(), vec(), vec(),
            pl.BlockSpec((1, V_DIM), lambda b, h, i: (0, 0)),
            pl.BlockSpec((tq, V_DIM), lambda b, h, i: (b * nq + i, h)),
            pl.BlockSpec((seq, V_DIM), lambda b, h, i: (b, kcol + h)),
            pl.BlockSpec((seq, V_DIM), lambda b, h, i: (b, 2 * kcol + h)),
        ],
        out_specs=pl.BlockSpec((tq, V_DIM), lambda b, h, i: (b * nq + i, h)),
        out_shape=jax.ShapeDtypeStruct((n, ATTN_W), BF16),
        scratch_shapes=[pltpu.VMEM((2 * tq, 2 * V_DIM), F32), pltpu.VMEM((2 * tq, tq), F32),
                        pltpu.VMEM((2 * tq, tq), F32)],
        compiler_params=pltpu.CompilerParams(
            dimension_semantics=("arbitrary", "arbitrary", "arbitrary"), vmem_limit_bytes=VMEM_LIMIT),
        name="diff_attn",
    )(lam_q1.astype(F32)[None, :], lam_k1.astype(F32)[None, :], lam_q2.astype(F32)[None, :],
      lam_k2.astype(F32)[None, :], subln.astype(F32)[None, :], z, z, z)


def _mix_kernel(x_ref, a_ref, u_ref, wpool_ref, pscale_ref, wout_ref, fn_ref, wr_ref, br_ref,
                h1_ref, xn_ref, gate_ref, route_ref, cnt_ref,
                ubuf, *, tm, tiles_per_seq):
    i = pl.program_id(0)

    @pl.when(i % tiles_per_seq == 0)
    def _():
        ubuf[0:POOL_HALO, :] = jnp.zeros((POOL_HALO, POOL_W), F32)

    ubuf[POOL_HALO:POOL_HALO + tm, :] = u_ref[...].astype(F32)

    t_seq = (i % tiles_per_seq) * tm + lax.broadcasted_iota(I32, (tm, 1), 0)
    mix = [a_ref[...]]
    for g, w in enumerate(POOL_WINDOWS):
        cols = slice(g * POOL_GROUP, (g + 1) * POOL_GROUP)
        cur = ubuf[POOL_HALO:POOL_HALO + tm, cols]
        win = cur
        for j in range(1, w):
            win = win + ubuf[POOL_HALO - j:POOL_HALO - j + tm, cols]
        count = jnp.minimum(t_seq + 1, w).astype(F32)
        pooled = win / count - cur
        pg = jnp.dot(pooled.astype(BF16), wpool_ref[g], preferred_element_type=F32) * pscale_ref[:, cols]
        mix.append(pg.astype(BF16))
    ubuf[0:POOL_HALO, :] = ubuf[tm:tm + POOL_HALO, :]
    mixed = jnp.dot(jnp.concatenate(mix, axis=1), wout_ref[...], preferred_element_type=F32)

    h1 = x_ref[...] + mixed
    h1_ref[...] = h1
    xn = h1 * lax.rsqrt(jnp.mean(h1 * h1, axis=-1, keepdims=True) + EPS) * fn_ref[...]
    xn_ref[...] = xn

    logits = jnp.dot(xn.astype(BF16), wr_ref[...], preferred_element_type=F32) + br_ref[...]
    lane = lax.broadcasted_iota(I32, logits.shape, 1).astype(F32)
    vals, idxs, hots = [], [], []
    rest = logits
    for _ in range(TOP_K):
        mx = jnp.max(rest, axis=1, keepdims=True)
        idx = jnp.min(jnp.where(rest == mx, lane, float(N_EXPERTS)), axis=1, keepdims=True)
        hot = lane == idx
        vals.append(mx)
        idxs.append(idx)
        hots.append(hot)
        rest = jnp.where(hot, -jnp.inf, rest)
    exps = [jnp.exp(v - vals[0]) for v in vals]
    denom = exps[0] + exps[1] + exps[2] + exps[3]

    chosen = jnp.zeros(logits.shape, F32)
    for hot in hots:
        chosen = chosen + hot.astype(F32)
    r = lax.broadcasted_iota(I32, (tm, tm), 0)
    cidx = lax.broadcasted_iota(I32, (tm, tm), 1)
    tri = (cidx < r).astype(BF16)
    before = jnp.dot(tri, chosen.astype(BF16), preferred_element_type=F32)

    k_lane = lax.broadcasted_iota(I32, (tm, TOP_K), 1)
    r_lane = lax.broadcasted_iota(I32, (tm, LANES), 1)
    gates = jnp.zeros((tm, TOP_K), F32)
    route = jnp.zeros((tm, LANES), F32)
    for k in range(TOP_K):
        rk = jnp.sum(jnp.where(hots[k], before, 0.0), axis=1, keepdims=True)
        gates = jnp.where(k_lane == k, exps[k] / denom, gates)
        route = jnp.where(r_lane == k, idxs[k], route)
        route = jnp.where(r_lane == TOP_K + k, rk, route)
    gate_ref[...] = gates
    route_ref[...] = route.T[0:2 * TOP_K, :].astype(I32)
    cnt_ref[...] = jnp.broadcast_to(jnp.sum(chosen, axis=0, keepdims=True), cnt_ref.shape).astype(I32)


def _mix_and_route(x2, a_out, z, seq, w_pool, pool_scale, w_out, ffn_norm, w_router, b_router):
    n, d = x2.shape
    tm = min(ROW_TILE, seq)
    ucol = (z.shape[1] - POOL_W) // POOL_W
    full = lambda shape: pl.BlockSpec(shape, lambda i: (0,) * len(shape))
    row = lambda width: pl.BlockSpec((tm, width), lambda i: (i, 0))
    return pl.pallas_call(
        functools.partial(_mix_kernel, tm=tm, tiles_per_seq=seq // tm),
        grid=(n // tm,),
        in_specs=[
            row(d),
            row(ATTN_W),
            pl.BlockSpec((tm, POOL_W), lambda i: (i, ucol)),
            full(w_pool.shape),
            full((1, POOL_W)),
            full(w_out.shape),
            full((1, d)),
            full((d, N_EXPERTS)),
            full((1, N_EXPERTS)),
        ],
        out_specs=[row(d), row(d), row(TOP_K),
                   pl.BlockSpec((2 * TOP_K, tm), lambda i: (0, i)),
                   pl.BlockSpec((SEG_ALIGN, N_EXPERTS), lambda i: (i, 0))],
        out_shape=[
            jax.ShapeDtypeStruct((n, d), F32),
            jax.ShapeDtypeStruct((n, d), F32),
            jax.ShapeDtypeStruct((n, TOP_K), F32),
            jax.ShapeDtypeStruct((2 * TOP_K, n), I32),
            jax.ShapeDtypeStruct((n // tm * SEG_ALIGN, N_EXPERTS), I32),
        ],
        scratch_shapes=[pltpu.VMEM((POOL_HALO + tm, POOL_W), F32)],
        compiler_params=pltpu.CompilerParams(
            dimension_semantics=("arbitrary",), vmem_limit_bytes=VMEM_LIMIT),
        name="mix_route",
    )(x2, a_out, z, w_pool.astype(BF16), pool_scale.astype(F32)[None, :], w_out.astype(BF16),
      ffn_norm.astype(F32)[None, :], w_router.astype(BF16), b_router.astype(F32)[None, :])


def _sorted_rows(tile):
    rows = TOP_K * tile + N_EXPERTS * (SEG_ALIGN - 1)
    return -(-rows // MXU_DIM) * MXU_DIM


def _segment_copies(tab_ref, make_copy):
    for e in range(N_EXPERTS):
        off_l = tab_ref[0, 0, e]
        length = tab_ref[0, 0, N_EXPERTS + e]
        off_g = tab_ref[0, 0, 2 * N_EXPERTS + e]
        for size in SEG_SIZES:
            take = (length & size) != 0
            yield take, make_copy(pl.multiple_of(off_l, SEG_ALIGN), pl.multiple_of(off_g, SEG_ALIGN), size)
            step = jnp.where(take, size, 0)
            off_l = off_l + step
            off_g = off_g + step


def _start_segments(tab_ref, piece):
    for cond, cp in _segment_copies(tab_ref, piece):
        pl.when(cond)(cp.start)


def _wait_segments(tab_ref, piece):
    for e in range(N_EXPERTS):
        length = tab_ref[0, 0, N_EXPERTS + e]
        for size in SEG_SIZES:
            pl.when((length & size) != 0)(piece(0, 0, size).wait)


def _dispatch_kernel(pend_ref, padded_ref, tab_ref, tab_prev_ref, lpos_ref, x_ref, xbuf_hbm,
                     sbuf, zbuf, sems, zsem, *, te, nb):
    i = pl.program_id(0)
    slot = i % 2

    @pl.when(i == 0)
    def _():
        zbuf[...] = jnp.zeros_like(zbuf)
        n_used = pend_ref[N_EXPERTS - 1] // te

        def fills():
            for e in range(N_EXPERTS):
                start = pl.multiple_of(jnp.maximum(pend_ref[e] - te, 0), te)
                yield padded_ref[e] > 0, pltpu.make_async_copy(zbuf, xbuf_hbm.at[pl.ds(start, te)], zsem)
            for e in range(N_EXPERTS):
                blk = n_used + e
                start = pl.multiple_of(jnp.minimum(blk, nb - 1) * te, te)
                yield blk < nb, pltpu.make_async_copy(zbuf, xbuf_hbm.at[pl.ds(start, te)], zsem)

        for cond, cp in fills():
            pl.when(cond)(cp.start)
        for cond, cp in fills():
            pl.when(cond)(cp.wait)

    rows, tile = sbuf.shape[1], x_ref.shape[0]
    pos = lax.broadcasted_iota(I32, (rows, tile), 0)
    hit = pos == lpos_ref[0:1, :]
    for k in range(1, TOP_K):
        hit = jnp.logical_or(hit, pos == lpos_ref[k:k + 1, :])
    perm = jnp.where(hit, 1.0, 0.0).astype(BF16)
    xb = x_ref[...].astype(BF16)
    half = xb.shape[1] // 2
    sbuf[slot, :, :half] = jnp.dot(perm, xb[:, :half], preferred_element_type=F32)
    sbuf[slot, :, half:] = jnp.dot(perm, xb[:, half:], preferred_element_type=F32)

    def pieces(buf_slot):
        def piece(off_l, off_g, size):
            return pltpu.make_async_copy(sbuf.at[buf_slot, pl.ds(off_l, size)],
                                         xbuf_hbm.at[pl.ds(off_g, size)], sems.at[buf_slot])
        return piece

    _start_segments(tab_ref, pieces(slot))
    pl.when(i > 0)(lambda: _wait_segments(tab_prev_ref, pieces(1 - slot)))
    pl.when(i == pl.num_programs(0) - 1)(lambda: _wait_segments(tab_ref, pieces(slot)))


def _dispatch(xn, tab, lpos8, pend, padded, nb):
    n, width = xn.shape
    tile = n // tab.shape[0]
    te = EXPERT_TILE
    grid_spec = pltpu.PrefetchScalarGridSpec(
        num_scalar_prefetch=2,
        grid=(n // tile,),
        in_specs=[
            pl.BlockSpec((1, 1, 3 * N_EXPERTS), lambda i, pe, pa: (i, 0, 0), memory_space=pltpu.SMEM),
            pl.BlockSpec((1, 1, 3 * N_EXPERTS), lambda i, pe, pa: (jnp.maximum(i - 1, 0), 0, 0),
                         memory_space=pltpu.SMEM),
            pl.BlockSpec((2 * TOP_K, tile), lambda i, pe, pa: (0, i)),
            pl.BlockSpec((tile, width), lambda i, pe, pa: (i, 0)),
        ],
        out_specs=pl.BlockSpec(memory_space=pl.ANY),
        scratch_shapes=[pltpu.VMEM((2, _sorted_rows(tile), width), F32), pltpu.VMEM((te, width), F32),
                        pltpu.SemaphoreType.DMA((2,)), pltpu.SemaphoreType.DMA(())],
    )
    return pl.pallas_call(
        functools.partial(_dispatch_kernel, te=te, nb=nb),
        grid_spec=grid_spec,
        out_shape=jax.ShapeDtypeStruct((nb * te, width), F32),
        compiler_params=pltpu.CompilerParams(
            dimension_semantics=("arbitrary",), has_side_effects=True, vmem_limit_bytes=VMEM_LIMIT),
        name="dispatch",
    )(pend, padded, tab, tab, lpos8, xn)


def _expert_kernel(be_ref, nu_ref, nxt_ref, par_ref, x_ref, wg_hbm, bg_ref, wu_hbm, bu_ref, wd_hbm, bd_ref,
                   y_ref, wbuf, wg_bf, wu_bf, wd_bf, sems):
    i = pl.program_id(0)
    prev = be_ref[jnp.maximum(i - 1, 0)]
    fresh = jnp.logical_or(i == 0, be_ref[i] != prev)

    def weight_copies(e, slot):
        return [pltpu.make_async_copy(w_hbm.at[e], wbuf.at[slot, j], sems.at[slot])
                for j, w_hbm in enumerate((wg_hbm, wu_hbm, wd_hbm))]

    @pl.when(jnp.logical_and(fresh, i < nu_ref[0]))
    def _():
        slot = par_ref[i]

        @pl.when(i == 0)
        def _():
            for cp in weight_copies(be_ref[i], slot):
                cp.start()

        for cp in weight_copies(be_ref[i], slot):
            cp.wait()
        wg_bf[...] = wbuf[slot, 0].astype(BF16)
        wu_bf[...] = wbuf[slot, 1].astype(BF16)
        wd_bf[...] = wbuf[slot, 2].astype(BF16)

        @pl.when(nxt_ref[i] >= 0)
        def _():
            for cp in weight_copies(nxt_ref[i], 1 - slot):
                cp.start()

    @pl.when(i < nu_ref[0])
    def _():
        xb = x_ref[...].astype(BF16)

        def proj(w_bf, b_ref):
            return jnp.dot(xb, w_bf[...], preferred_element_type=F32) + b_ref[0]

        gt = jnp.minimum(proj(wg_bf, bg_ref), SWIGLU_LIMIT)
        up = jnp.clip(proj(wu_bf, bu_ref), -SWIGLU_LIMIT, SWIGLU_LIMIT)
        hdn = (up + 1.0) * (gt * jax.nn.sigmoid(SWIGLU_ALPHA * gt))
        y_ref[...] = jnp.dot(hdn.astype(BF16), wd_bf[...], preferred_element_type=F32) + bd_ref[0]

    @pl.when(i >= nu_ref[0])
    def _():
        y_ref[...] = jnp.zeros_like(y_ref)


def _experts(xbuf, block_expert, n_used, next_expert, parity, w_gate, b_gate, w_up, b_up, w_down, b_down):
    n_slots, width = xbuf.shape
    ne, d, de = w_gate.shape
    assert w_down.shape == (ne, d, de), "one (2, 3, d, de) staging buffer holds all three weights"
    te = EXPERT_TILE
    nb = n_slots // te
    last = lambda i, be, nu, nx, pa: jnp.minimum(i, nu[0] - 1)
    bspec = lambda width: pl.BlockSpec((1, 1, width), lambda i, be, nu, nx, pa: (be[i], 0, 0))
    hbm = pl.BlockSpec(memory_space=pl.ANY)
    grid_spec = pltpu.PrefetchScalarGridSpec(
        num_scalar_prefetch=4,
        grid=(nb,),
        in_specs=[
            pl.BlockSpec((te, width), lambda i, be, nu, nx, pa: (last(i, be, nu, nx, pa), 0)),
            hbm, bspec(de), hbm, bspec(de), hbm, bspec(d),
        ],
        out_specs=pl.BlockSpec((te, d), lambda i, be, nu, nx, pa: (i, 0)),
        scratch_shapes=[pltpu.VMEM((2, 3, d, de), F32), pltpu.VMEM((d, de), BF16), pltpu.VMEM((d, de), BF16),
                        pltpu.VMEM((de, d), BF16), pltpu.SemaphoreType.DMA((2,))],
    )
    return pl.pallas_call(
        _expert_kernel,
        grid_spec=grid_spec,
        out_shape=jax.ShapeDtypeStruct((n_slots, d), F32),
        compiler_params=pltpu.CompilerParams(
            dimension_semantics=("arbitrary",), vmem_limit_bytes=VMEM_LIMIT),
        name="experts",
    )(block_expert, n_used, next_expert, parity, xbuf, w_gate, b_gate.reshape(ne, 1, de), w_up,
      b_up.reshape(ne, 1, de), w_down, b_down.reshape(ne, 1, d))


def _combine_kernel(tab_ref, tab_next_ref, ybuf_hbm, h1_ref, lcol_ref, gate_ref, p_ref, gn_ref, wg_ref,
                    wp_ref, pn_ref, o_ref, ysort, sems):
    i = pl.program_id(0)
    slot = i % 2

    def pieces(buf_slot):
        def piece(off_l, off_g, size):
            return pltpu.make_async_copy(ybuf_hbm.at[pl.ds(off_g, size)],
                                         ysort.at[buf_slot, pl.ds(off_l, size)], sems.at[buf_slot])
        return piece

    @pl.when(i == 0)
    def _():
        ysort[...] = jnp.zeros_like(ysort)
        _start_segments(tab_ref, pieces(0))

    _wait_segments(tab_ref, pieces(slot))
    pl.when(i < pl.num_programs(0) - 1)(lambda: _start_segments(tab_next_ref, pieces(1 - slot)))

    tile, rows = h1_ref.shape[0], ysort.shape[1]
    pos = lax.broadcasted_iota(I32, (tile, rows), 1)
    lcol = lcol_ref[...]
    gates = gate_ref[...]
    g = jnp.zeros((tile, rows), F32)
    for k in range(TOP_K):
        g = g + jnp.where(pos == lcol[:, k:k + 1], gates[:, k:k + 1], 0.0)
    gb = g.astype(BF16)
    half = ysort.shape[2] // 2
    moe = jnp.concatenate(
        [jnp.dot(gb, ysort[slot, :, :half].astype(BF16), preferred_element_type=F32),
         jnp.dot(gb, ysort[slot, :, half:].astype(BF16), preferred_element_type=F32)], axis=1)
    h2 = h1_ref[...] + moe

    hn = h2 * lax.rsqrt(jnp.mean(h2 * h2, axis=-1, keepdims=True) + EPS) * gn_ref[...]
    gate = jax.nn.sigmoid(jnp.dot(hn.astype(BF16), wg_ref[...], preferred_element_type=F32))
    e = jnp.dot(p_ref[...].astype(BF16), wp_ref[...], preferred_element_type=F32)
    e = e * lax.rsqrt(jnp.mean(e * e, axis=-1, keepdims=True) + EPS) * pn_ref[...]
    o_ref[...] = h2 + gate * e


def _combine_ple(ybuf, tab, lcol, h1, gates, p2, ple_gate_norm, w_ple_gate, w_ple_proj, ple_post_norm):
    n, d = h1.shape
    tile = n // tab.shape[0]
    full = lambda shape: pl.BlockSpec(shape, lambda i: (0,) * len(shape))
    row = lambda width: pl.BlockSpec((tile, width), lambda i: (i, 0))
    return pl.pallas_call(
        _combine_kernel,
        grid=(n // tile,),
        in_specs=[
            pl.BlockSpec((1, 1, 3 * N_EXPERTS), lambda i: (i, 0, 0), memory_space=pltpu.SMEM),
            pl.BlockSpec((1, 1, 3 * N_EXPERTS), lambda i: (jnp.minimum(i + 1, n // tile - 1), 0, 0),
                         memory_space=pltpu.SMEM),
            pl.BlockSpec(memory_space=pl.ANY),
            row(d), row(TOP_K), row(TOP_K), row(p2.shape[1]),
            full((1, d)), full(w_ple_gate.shape), full(w_ple_proj.shape), full((1, d)),
        ],
        out_specs=row(d),
        out_shape=jax.ShapeDtypeStruct((n, d), F32),
        scratch_shapes=[pltpu.VMEM((2, _sorted_rows(tile), d), F32), pltpu.SemaphoreType.DMA((2,))],
        compiler_params=pltpu.CompilerParams(
            dimension_semantics=("arbitrary",), vmem_limit_bytes=VMEM_LIMIT),
        name="combine_ple",
    )(tab, tab, ybuf, h1, lcol, gates, p2, ple_gate_norm.astype(F32)[None, :],
      w_ple_gate.astype(BF16), w_ple_proj.astype(BF16), ple_post_norm.astype(F32)[None, :])


def _layer(h, p_i, positions, lambda_init, attn_norm, w_in, q_norm, k_norm, lam_q1, lam_k1, lam_q2,
           lam_k2, subln, w_pool, pool_scale, w_out, ffn_norm, w_router, b_router, w_gate, b_gate,
           w_up, b_up, w_down, b_down, ple_gate_norm, w_ple_gate, w_ple_proj, ple_post_norm):
    batch, seq, d = h.shape
    n = batch * seq
    x2 = h.reshape(n, d)
    pos2 = positions.reshape(n, 1).astype(I32)

    lane = jnp.arange(LANES) % HEAD_DIM
    freqs = ROPE_THETA ** (-jnp.arange(0, ROPE_DIM, 2, dtype=F32) / ROPE_DIM)
    freq = jnp.where(lane < ROPE_DIM, freqs[lane % ROPE_HALF], 0.0).astype(F32)[None, :]

    z = _in_proj(x2, pos2, freq, attn_norm, w_in, q_norm, k_norm)
    a_out = _attention(z, batch, seq, lam_q1, lam_k1, lam_q2, lam_k2, subln, lambda_init)
    h1, xn, gates, route, cnt8 = _mix_and_route(
        x2, a_out, z, seq, w_pool, pool_scale, w_out, ffn_norm, w_router, b_router)

    te = EXPERT_TILE
    nt = cnt8.shape[0] // SEG_ALIGN
    tile = n // nt
    cnt = cnt8.reshape(nt, SEG_ALIGN, N_EXPERTS)[:, 0, :]
    seg = (cnt + SEG_ALIGN - 1) // SEG_ALIGN * SEG_ALIGN
    lstart = jnp.cumsum(seg, axis=1) - seg
    total = jnp.sum(seg, axis=0)
    padded = (total + te - 1) // te * te
    pend = jnp.cumsum(padded).astype(I32)
    gstart = (pend - padded)[None, :] + jnp.cumsum(seg, axis=0) - seg
    tab = jnp.concatenate([lstart, seg, gstart], axis=1).astype(I32).reshape(nt, 1, 3 * N_EXPERTS)

    eidx = route[:TOP_K].reshape(TOP_K, nt, tile)
    experts = jnp.arange(N_EXPERTS, dtype=I32)
    lpos = (route[TOP_K:].reshape(TOP_K, nt, tile)
            + jnp.sum(jnp.where(eidx[..., None] == experts, lstart[None, :, None, :], 0), axis=-1))
    lpos = lpos.reshape(TOP_K, n).astype(I32)
    lpos8 = jnp.concatenate([lpos, jnp.zeros_like(lpos)], axis=0)

    nb = -(-(n * TOP_K + nt * N_EXPERTS * (SEG_ALIGN - 1) + N_EXPERTS * (te - 1)) // te)
    n_used = pend[-1] // te
    first_row = jnp.minimum(jnp.arange(nb, dtype=I32), n_used - 1) * te
    block_expert = jnp.minimum(jnp.sum(pend[None, :] <= first_row[:, None], axis=1), N_EXPERTS - 1).astype(I32)

    owns = padded > 0
    later = jnp.where(owns[None, :] & (experts[None, :] > experts[:, None]), experts[None, :], N_EXPERTS)
    next_owner = jnp.min(later, axis=1)
    next_owner = jnp.where(next_owner < N_EXPERTS, next_owner, -1).astype(I32)
    next_expert = next_owner[block_expert]
    parity = (jnp.cumsum(owns.astype(I32)) - 1)[block_expert] % 2

    xbuf = _dispatch(xn, tab, lpos8, pend, padded.astype(I32), nb)
    ybuf = _experts(xbuf, block_expert, n_used[None], next_expert, parity.astype(I32), w_gate, b_gate, w_up,
                    b_up, w_down, b_down)
    out = _combine_ple(ybuf, tab, lpos.T, h1, gates, p_i.reshape(n, -1), ple_gate_norm, w_ple_gate,
                       w_ple_proj, ple_post_norm)
    return out.reshape(batch, seq, d)


def kernel(x, p, positions, attn_norm, w_in, q_norm, k_norm, lam_q1, lam_k1, lam_q2, lam_k2, subln,
           w_pool, pool_scale, w_out, ffn_norm, w_router, b_router, w_gate, b_gate, w_up, b_up,
           w_down, b_down, ple_gate_norm, w_ple_gate, w_ple_proj, ple_post_norm):
    h = x
    for i in range(attn_norm.shape[0]):
        lambda_init = 0.8 - 0.6 * math.exp(-0.3 * i)
        h = _layer(h, p[i], positions, lambda_init, attn_norm[i], w_in[i], q_norm[i], k_norm[i],
                   lam_q1[i], lam_k1[i], lam_q2[i], lam_k2[i], subln[i], w_pool[i], pool_scale[i],
                   w_out[i], ffn_norm[i], w_router[i], b_router[i], w_gate[i], b_gate[i], w_up[i],
                   b_up[i], w_down[i], b_down[i], ple_gate_norm[i], w_ple_gate[i], w_ple_proj[i],
                   ple_post_norm[i])
    return h
```

```python
import functools
import math

import jax
import jax.numpy as jnp
from jax import lax
from jax.experimental import pallas as pl
from jax.experimental.pallas import tpu as pltpu

F32 = jnp.float32
BF16 = jnp.bfloat16
I32 = jnp.int32

N_HEADS = 4
HEAD_DIM = 64
V_DIM = 2 * HEAD_DIM
ATTN_W = N_HEADS * V_DIM
POOL_WINDOWS = (2, 4, 8, 16)
POOL_GROUP = 128
POOL_W = len(POOL_WINDOWS) * POOL_GROUP
POOL_HALO = 16
ROPE_DIM = HEAD_DIM // 4
ROPE_HALF = ROPE_DIM // 2
ROPE_THETA = 500000.0
N_EXPERTS = 32
TOP_K = 4
SWIGLU_LIMIT = 7.0
SWIGLU_ALPHA = 1.702
EPS = 1e-6
LANES = 128
MXU_DIM = 256
VMEM_LIMIT = 56 * 1024 * 1024

ROW_TILE = 512
Q_TILE = 512
EXPERT_TILE = 512
SEG_ALIGN = 8
SEG_SIZES = (512, 256, 128, 64, 32, 16, 8)
assert SEG_SIZES[0] >= ROW_TILE and SEG_SIZES[-1] == SEG_ALIGN


def _inproj_kernel(x_ref, pos_ref, freq_ref, an_ref, w_ref, qn_ref, kn_ref, g_ref, z_ref):
    x = x_ref[...]
    ms = jnp.mean(x * x, axis=-1, keepdims=True)
    hn = (x * lax.rsqrt(ms + EPS) * an_ref[...]).astype(BF16)

    tm = x.shape[0]
    ang = freq_ref[...] * pos_ref[0].astype(F32)
    packed = jnp.concatenate([jnp.cos(ang), jnp.sin(ang), jnp.zeros((LANES - ROPE_DIM, tm), F32)], axis=0)
    base = packed.T
    base = base + pltpu.roll(base, HEAD_DIM, 1)
    lane = lax.broadcasted_iota(I32, base.shape, 1) & (HEAD_DIM - 1)
    c = jnp.where(lane < ROPE_HALF, base, jnp.where(lane < ROPE_DIM, pltpu.roll(base, ROPE_HALF, 1), 1.0))
    s_dn = jnp.where((lane >= ROPE_HALF) & (lane < ROPE_DIM), base, 0.0)
    s_up = jnp.where(lane < ROPE_HALF, -pltpu.roll(base, LANES - ROPE_HALF, 1), 0.0)

    for part, (nrm_ref, scale) in enumerate(((qn_ref, 1.0 / math.sqrt(HEAD_DIM)), (kn_ref, 1.0))):
        zc = jnp.dot(hn, w_ref[:, part * ATTN_W:(part + 1) * ATTN_W], preferred_element_type=F32)
        for half in range(ATTN_W // 256):
            zh = zc[:, half * 256:(half + 1) * 256]
            ss = jnp.dot((zh * zh).astype(BF16), g_ref[...], preferred_element_type=F32)
            y = zh * lax.rsqrt(ss * (1.0 / HEAD_DIM) + EPS) * nrm_ref[...]
            for blk in range(2):
                yb = y[:, blk * LANES:(blk + 1) * LANES]
                r = yb * c + pltpu.roll(yb, ROPE_HALF, 1) * s_dn + pltpu.roll(yb, LANES - ROPE_HALF, 1) * s_up
                col = part * ATTN_W + half * 256 + blk * LANES
                z_ref[:, col:col + LANES] = (r * scale).astype(BF16)

    zc = jnp.dot(hn, w_ref[:, 2 * ATTN_W:], preferred_element_type=F32)
    z_ref[:, 2 * ATTN_W:] = zc.astype(BF16)


def _in_proj(x2, pos, freq, attn_norm, w_in, q_norm, k_norm):
    n, d = x2.shape
    d_in = w_in.shape[1]
    tm = min(ROW_TILE, n)
    pos3 = pos.reshape(n // tm, 1, tm)
    gid = jnp.arange(256) // HEAD_DIM
    group_ones = (gid[:, None] == gid[None, :]).astype(BF16)
    qn = jnp.tile(q_norm.astype(F32), 256 // HEAD_DIM)[None, :]
    kn = jnp.tile(k_norm.astype(F32), 256 // HEAD_DIM)[None, :]
    full = lambda shape: pl.BlockSpec(shape, lambda i: (0,) * len(shape))
    return pl.pallas_call(
        _inproj_kernel,
        grid=(n // tm,),
        in_specs=[
            pl.BlockSpec((tm, d), lambda i: (i, 0)),
            pl.BlockSpec((1, 1, tm), lambda i: (i, 0, 0)),
            full((ROPE_HALF, 1)),
            full((1, d)),
            full((d, d_in)),
            full((1, 256)),
            full((1, 256)),
            full((256, 256)),
        ],
        out_specs=pl.BlockSpec((tm, d_in), lambda i: (i, 0)),
        out_shape=jax.ShapeDtypeStruct((n, d_in), BF16),
        compiler_params=pltpu.CompilerParams(
            dimension_semantics=("arbitrary",), vmem_limit_bytes=VMEM_LIMIT),
        name="in_proj",
    )(x2, pos3, freq, attn_norm.astype(F32)[None, :], w_in.astype(BF16), qn, kn, group_ones)


def _attn_kernel(lq1_ref, lk1_ref, lq2_ref, lk2_ref, sub_ref, q_ref, k_ref, v_ref, o_ref,
                 acc_ref, sa_ref, sb_ref, *, tq, lambda_init):
    i = pl.program_id(2)
    lam = (jnp.exp(jnp.sum(lq1_ref[...] * lk1_ref[...], keepdims=True))
           - jnp.exp(jnp.sum(lq2_ref[...] * lk2_ref[...], keepdims=True)) + lambda_init)

    q = q_ref[...]
    lane = lax.broadcasted_iota(I32, q.shape, 1)
    zero = jnp.zeros_like(q)
    qs = jnp.concatenate([jnp.where(lane < HEAD_DIM, q, zero),
                          jnp.where(lane >= HEAD_DIM, q, zero)], axis=0)

    ones = jnp.ones((tq, V_DIM), BF16)
    acc_ref[...] = jnp.zeros(acc_ref.shape, F32)

    def scores(j, s_ref):
        k = k_ref[pl.ds(pl.multiple_of(j * tq, tq), tq), :]
        nt = (((1,), (1,)), ((), ()))
        s_ref[:, :tq // 2] = lax.dot_general(qs, k[:tq // 2], nt, preferred_element_type=F32)
        s_ref[:, tq // 2:] = lax.dot_general(qs, k[tq // 2:], nt, preferred_element_type=F32)

    def update(j, s_ref, m, masked=False):
        v1 = jnp.concatenate([v_ref[pl.ds(pl.multiple_of(j * tq, tq), tq), :], ones], axis=1)
        s = s_ref[...]
        if masked:
            row = lax.broadcasted_iota(I32, s.shape, 0)
            row = jnp.where(row >= tq, row - tq, row)
            col = lax.broadcasted_iota(I32, s.shape, 1)
            s = jnp.where(col <= row, s, -jnp.inf)
        m_new = jnp.maximum(m, jnp.max(s, axis=1, keepdims=True))
        p = jnp.exp(s - m_new).astype(BF16)
        alpha = jnp.exp(m - m_new)
        for c in range(2):
            rows = slice(c * tq, (c + 1) * tq)
            acc_ref[rows, :] = alpha[rows] * acc_ref[rows, :] + jnp.dot(p[rows], v1, preferred_element_type=F32)
        return m_new

    def pair(jj, m):
        j = 2 * jj
        scores(j + 1, sb_ref)
        m = update(j, sa_ref, m)
        scores(j + 2, sa_ref)
        return update(j + 1, sb_ref, m)

    scores(0, sa_ref)
    m = lax.fori_loop(0, i // 2, pair, jnp.full((2 * tq, 1), -jnp.inf, F32))

    @pl.when(i % 2 == 0)
    def _():
        update(i, sa_ref, m, masked=True)

    @pl.when(i % 2 == 1)
    def _():
        scores(i, sb_ref)
        update(i, sb_ref, update(i - 1, sa_ref, m), masked=True)
    o = acc_ref[:, :V_DIM] / acc_ref[:, V_DIM:]
    o = o[:tq] - lam * o[tq:]
    o = o * lax.rsqrt(jnp.mean(o * o, axis=-1, keepdims=True) + EPS) * sub_ref[...] * (1.0 - lambda_init)
    o_ref[...] = o.astype(o_ref.dtype)


def _attention(z, batch, seq, lam_q1, lam_k1, lam_q2, lam_k2, subln, lambda_init):
    n = z.shape[0]
    tq = min(Q_TILE, seq)
    nq = seq // tq
    kcol = ATTN_W // V_DIM
    vec = lambda: pl.BlockSpec((1, HEAD_DIM), lambda b, h, i: (0, 0))
    return pl.pallas_call(
        functools.partial(_attn_kernel, tq=tq, lambda_init=lambda_init),
        grid=(batch, N_HEADS, nq),
        in_specs=[
            vec(), vec(), vec(), vec(),
            pl.BlockSpec((1, V_DIM), lambda b, h, i: (0, 0)),
            pl.BlockSpec((tq, V_DIM), lambda b, h, i: (b * nq + i, h)),
            pl.BlockSpec((seq, V_DIM), lambda b, h, i: (b, kcol + h)),
            pl.BlockSpec((seq, V_DIM), lambda b, h, i: (b, 2 * kcol + h)),
        ],
        out_specs=pl.BlockSpec((tq, V_DIM), lambda b, h, i: (b * nq + i, h)),
        out_shape=jax.ShapeDtypeStruct((n, ATTN_W), BF16),
        scratch_shapes=[pltpu.VMEM((2 * tq, 2 * V_DIM), F32), pltpu.VMEM((2 * tq, tq), F32),
                        pltpu.VMEM((2 * tq, tq), F32)],
        compiler_params=pltpu.CompilerParams(
            dimension_semantics=("arbitrary", "arbitrary", "arbitrary"), vmem_limit_bytes=VMEM_LIMIT),
        name="diff_attn",
    )(lam_q1.astype(F32)[None, :], lam_k1.astype(F32)[None, :], lam_q2.astype(F32)[None, :],
      lam_k2.astype(F32)[None, :], subln.astype(F32)[None, :], z, z, z)


def _mix_kernel(x_ref, a_ref, u_ref, wpool_ref, pscale_ref, wout_ref, fn_ref, wr_ref, br_ref,
                h1_ref, xn_ref, gate_ref, route_ref, cnt_ref,
                ubuf, *, tm, tiles_per_seq):
    i = pl.program_id(0)

    @pl.when(i % tiles_per_seq == 0)
    def _():
        ubuf[0:POOL_HALO, :] = jnp.zeros((POOL_HALO, POOL_W), F32)

    ubuf[POOL_HALO:POOL_HALO + tm, :] = u_ref[...].astype(F32)

    t_seq = (i % tiles_per_seq) * tm + lax.broadcasted_iota(I32, (tm, 1), 0)
    mix = [a_ref[...]]
    for g, w in enumerate(POOL_WINDOWS):
        cols = slice(g * POOL_GROUP, (g + 1) * POOL_GROUP)
        cur = ubuf[POOL_HALO:POOL_HALO + tm, cols]
        win = cur
        for j in range(1, w):
            win = win + ubuf[POOL_HALO - j:POOL_HALO - j + tm, cols]
        count = jnp.minimum(t_seq + 1, w).astype(F32)
        pooled = win / count - cur
        pg = jnp.dot(pooled.astype(BF16), wpool_ref[g], preferred_element_type=F32) * pscale_ref[:, cols]
        mix.append(pg.astype(BF16))
    ubuf[0:POOL_HALO, :] = ubuf[tm:tm + POOL_HALO, :]
    mixed = jnp.dot(jnp.concatenate(mix, axis=1), wout_ref[...], preferred_element_type=F32)

    h1 = x_ref[...] + mixed
    h1_ref[...] = h1
    xn = h1 * lax.rsqrt(jnp.mean(h1 * h1, axis=-1, keepdims=True) + EPS) * fn_ref[...]
    xn_ref[...] = xn

    logits = lax.dot_general(wr_ref[...], xn.astype(BF16), (((1,), (1,)), ((), ())),
                             preferred_element_type=F32) + br_ref[...]
    expert = lax.broadcasted_iota(I32, logits.shape, 0).astype(F32)
    vals, idxs, hots = [], [], []
    rest = logits
    for _ in range(TOP_K):
        mx = jnp.max(rest, axis=0, keepdims=True)
        idx = jnp.min(jnp.where(rest == mx, expert, float(N_EXPERTS)), axis=0, keepdims=True)
        hot = expert == idx
        vals.append(mx)
        idxs.append(idx)
        hots.append(hot)
        rest = jnp.where(hot, -jnp.inf, rest)
    exps = [jnp.exp(v - vals[0]) for v in vals]
    denom = exps[0] + exps[1] + exps[2] + exps[3]

    chosen = jnp.zeros(logits.shape, F32)
    for hot in hots:
        chosen = chosen + hot.astype(F32)
    earlier = lax.broadcasted_iota(I32, (tm, tm), 0) < lax.broadcasted_iota(I32, (tm, tm), 1)
    before = jnp.dot(chosen.astype(BF16), jnp.where(earlier, 1.0, 0.0).astype(BF16), preferred_element_type=F32)

    ranks = [jnp.sum(jnp.where(hot, before, 0.0), axis=0, keepdims=True) for hot in hots]
    route_ref[...] = jnp.concatenate(idxs + ranks, axis=0).astype(I32)
    gate_ref[...] = jnp.concatenate([e / denom for e in exps] + [jnp.zeros_like(denom)] * TOP_K, axis=0)
    cnt_ref[...] = jnp.broadcast_to(jnp.sum(chosen, axis=1, keepdims=True), cnt_ref.shape).astype(I32)


def _mix_and_route(x2, a_out, z, seq, w_pool, pool_scale, w_out, ffn_norm, w_router, b_router):
    n, d = x2.shape
    tm = min(ROW_TILE, seq)
    ucol = (z.shape[1] - POOL_W) // POOL_W
    full = lambda shape: pl.BlockSpec(shape, lambda i: (0,) * len(shape))
    row = lambda width: pl.BlockSpec((tm, width), lambda i: (i, 0))
    return pl.pallas_call(
        functools.partial(_mix_kernel, tm=tm, tiles_per_seq=seq // tm),
        grid=(n // tm,),
        in_specs=[
            row(d),
            row(ATTN_W),
            pl.BlockSpec((tm, POOL_W), lambda i: (i, ucol)),
            full(w_pool.shape),
            full((1, POOL_W)),
            full(w_out.shape),
            full((1, d)),
            full((N_EXPERTS, d)),
            full((N_EXPERTS, 1)),
        ],
        out_specs=[row(d), row(d),
                   pl.BlockSpec((2 * TOP_K, tm), lambda i: (0, i)),
                   pl.BlockSpec((2 * TOP_K, tm), lambda i: (0, i)),
                   pl.BlockSpec((N_EXPERTS, LANES), lambda i: (i, 0))],
        out_shape=[
            jax.ShapeDtypeStruct((n, d), F32),
            jax.ShapeDtypeStruct((n, d), F32),
            jax.ShapeDtypeStruct((2 * TOP_K, n), F32),
            jax.ShapeDtypeStruct((2 * TOP_K, n), I32),
            jax.ShapeDtypeStruct((n // tm * N_EXPERTS, LANES), I32),
        ],
        scratch_shapes=[pltpu.VMEM((POOL_HALO + tm, POOL_W), F32)],
        compiler_params=pltpu.CompilerParams(
            dimension_semantics=("arbitrary",), vmem_limit_bytes=VMEM_LIMIT),
        name="mix_route",
    )(x2, a_out, z, w_pool.astype(BF16), pool_scale.astype(F32)[None, :], w_out.astype(BF16),
      ffn_norm.astype(F32)[None, :], w_router.T.astype(BF16), b_router.astype(F32)[:, None])


def _sorted_rows(tile):
    rows = TOP_K * tile + N_EXPERTS * (SEG_ALIGN - 1)
    return -(-rows // MXU_DIM) * MXU_DIM


def _segment_copies(tab_ref, make_copy):
    for e in range(N_EXPERTS):
        off_l = tab_ref[0, 0, e]
        length = tab_ref[0, 0, N_EXPERTS + e]
        off_g = tab_ref[0, 0, 2 * N_EXPERTS + e]
        for size in SEG_SIZES:
            take = (length & size) != 0
            yield take, make_copy(pl.multiple_of(off_l, SEG_ALIGN), pl.multiple_of(off_g, SEG_ALIGN), size)
            step = jnp.where(take, size, 0)
            off_l = off_l + step
            off_g = off_g + step


def _start_segments(tab_ref, piece):
    for cond, cp in _segment_copies(tab_ref, piece):
        pl.when(cond)(cp.start)


def _wait_segments(tab_ref, piece):
    for e in range(N_EXPERTS):
        length = tab_ref[0, 0, N_EXPERTS + e]
        for size in SEG_SIZES:
            pl.when((length & size) != 0)(piece(0, 0, size).wait)


def _dispatch_kernel(pend_ref, padded_ref, tab_ref, tab_prev_ref, route_ref, x_ref, xbuf_hbm, lpos_ref,
                     sbuf, zbuf, sems, zsem, *, te, nb):
    i = pl.program_id(0)
    slot = i % 2

    @pl.when(i == 0)
    def _():
        zbuf[...] = jnp.zeros_like(zbuf)
        n_used = pend_ref[N_EXPERTS - 1] // te

        def fills():
            for e in range(N_EXPERTS):
                start = pl.multiple_of(jnp.maximum(pend_ref[e] - te, 0), te)
                yield padded_ref[e] > 0, pltpu.make_async_copy(zbuf, xbuf_hbm.at[pl.ds(start, te)], zsem)
            for e in range(N_EXPERTS):
                blk = n_used + e
                start = pl.multiple_of(jnp.minimum(blk, nb - 1) * te, te)
                yield blk < nb, pltpu.make_async_copy(zbuf, xbuf_hbm.at[pl.ds(start, te)], zsem)

        for cond, cp in fills():
            pl.when(cond)(cp.start)
        for cond, cp in fills():
            pl.when(cond)(cp.wait)

    rows, tile = sbuf.shape[1], x_ref.shape[0]
    eidx = route_ref[0:TOP_K, :]
    lpos = route_ref[TOP_K:, :]
    for e in range(N_EXPERTS):
        lpos = lpos + jnp.where(eidx == e, tab_ref[0, 0, e], 0)
    lpos_ref[...] = jnp.concatenate([lpos, jnp.zeros_like(lpos)], axis=0)
    pos = lax.broadcasted_iota(I32, (rows, tile), 0)
    hit = pos == lpos[0:1, :]
    for k in range(1, TOP_K):
        hit = jnp.logical_or(hit, pos == lpos[k:k + 1, :])
    perm = jnp.where(hit, 1.0, 0.0).astype(BF16)
    xb = x_ref[...].astype(BF16)
    half = xb.shape[1] // 2
    sbuf[slot, :, :half] = jnp.dot(perm, xb[:, :half], preferred_element_type=F32)
    sbuf[slot, :, half:] = jnp.dot(perm, xb[:, half:], preferred_element_type=F32)

    def pieces(buf_slot):
        def piece(off_l, off_g, size):
            return pltpu.make_async_copy(sbuf.at[buf_slot, pl.ds(off_l, size)],
                                         xbuf_hbm.at[pl.ds(off_g, size)], sems.at[buf_slot])
        return piece

    _start_segments(tab_ref, pieces(slot))
    pl.when(i > 0)(lambda: _wait_segments(tab_prev_ref, pieces(1 - slot)))
    pl.when(i == pl.num_programs(0) - 1)(lambda: _wait_segments(tab_ref, pieces(slot)))


def _dispatch(xn, tab, route, pend, padded, nb):
    n, width = xn.shape
    tile = n // tab.shape[0]
    te = EXPERT_TILE
    grid_spec = pltpu.PrefetchScalarGridSpec(
        num_scalar_prefetch=2,
        grid=(n // tile,),
        in_specs=[
            pl.BlockSpec((1, 1, 3 * N_EXPERTS), lambda i, pe, pa: (i, 0, 0), memory_space=pltpu.SMEM),
            pl.BlockSpec((1, 1, 3 * N_EXPERTS), lambda i, pe, pa: (jnp.maximum(i - 1, 0), 0, 0),
                         memory_space=pltpu.SMEM),
            pl.BlockSpec((2 * TOP_K, tile), lambda i, pe, pa: (0, i)),
            pl.BlockSpec((tile, width), lambda i, pe, pa: (i, 0)),
        ],
        out_specs=[pl.BlockSpec(memory_space=pl.ANY),
                   pl.BlockSpec((2 * TOP_K, tile), lambda i, pe, pa: (0, i))],
        scratch_shapes=[pltpu.VMEM((2, _sorted_rows(tile), width), F32), pltpu.VMEM((te, width), F32),
                        pltpu.SemaphoreType.DMA((2,)), pltpu.SemaphoreType.DMA(())],
    )
    return pl.pallas_call(
        functools.partial(_dispatch_kernel, te=te, nb=nb),
        grid_spec=grid_spec,
        out_shape=[jax.ShapeDtypeStruct((nb * te, width), F32),
                   jax.ShapeDtypeStruct((2 * TOP_K, n), I32)],
        compiler_params=pltpu.CompilerParams(
            dimension_semantics=("arbitrary",), has_side_effects=True, vmem_limit_bytes=VMEM_LIMIT),
        name="dispatch",
    )(pend, padded, tab, tab, route, xn)


def _expert_kernel(be_ref, nu_ref, nxt_ref, par_ref, x_ref, wg_hbm, bg_ref, wu_hbm, bu_ref, wd_hbm, bd_ref,
                   y_ref, wbuf, wg_bf, wu_bf, wd_bf, sems):
    i = pl.program_id(0)
    prev = be_ref[jnp.maximum(i - 1, 0)]
    fresh = jnp.logical_or(i == 0, be_ref[i] != prev)

    def weight_copies(e, slot):
        return [pltpu.make_async_copy(w_hbm.at[e], wbuf.at[slot, j], sems.at[slot])
                for j, w_hbm in enumerate((wg_hbm, wu_hbm, wd_hbm))]

    @pl.when(jnp.logical_and(fresh, i < nu_ref[0]))
    def _():
        slot = par_ref[i]

        @pl.when(i == 0)
        def _():
            for cp in weight_copies(be_ref[i], slot):
                cp.start()

        for cp in weight_copies(be_ref[i], slot):
            cp.wait()
        wg_bf[...] = wbuf[slot, 0].astype(BF16)
        wu_bf[...] = wbuf[slot, 1].astype(BF16)
        wd_bf[...] = wbuf[slot, 2].astype(BF16)

        @pl.when(nxt_ref[i] >= 0)
        def _():
            for cp in weight_copies(nxt_ref[i], 1 - slot):
                cp.start()

    @pl.when(i < nu_ref[0])
    def _():
        xb = x_ref[...].astype(BF16)

        def proj(w_bf, b_ref):
            return jnp.dot(xb, w_bf[...], preferred_element_type=F32) + b_ref[0]

        gt = jnp.minimum(proj(wg_bf, bg_ref), SWIGLU_LIMIT)
        up = jnp.clip(proj(wu_bf, bu_ref), -SWIGLU_LIMIT, SWIGLU_LIMIT)
        hdn = (up + 1.0) * (gt * jax.nn.sigmoid(SWIGLU_ALPHA * gt))
        y_ref[...] = jnp.dot(hdn.astype(BF16), wd_bf[...], preferred_element_type=F32) + bd_ref[0]

    @pl.when(i >= nu_ref[0])
    def _():
        y_ref[...] = jnp.zeros_like(y_ref)


def _experts(xbuf, block_expert, n_used, next_expert, parity, w_gate, b_gate, w_up, b_up, w_down, b_down):
    n_slots, width = xbuf.shape
    ne, d, de = w_gate.shape
    assert w_down.shape == (ne, d, de), "one (2, 3, d, de) staging buffer holds all three weights"
    te = EXPERT_TILE
    nb = n_slots // te
    last = lambda i, be, nu, nx, pa: jnp.minimum(i, nu[0] - 1)
    bspec = lambda width: pl.BlockSpec((1, 1, width), lambda i, be, nu, nx, pa: (be[i], 0, 0))
    hbm = pl.BlockSpec(memory_space=pl.ANY)
    grid_spec = pltpu.PrefetchScalarGridSpec(
        num_scalar_prefetch=4,
        grid=(nb,),
        in_specs=[
            pl.BlockSpec((te, width), lambda i, be, nu, nx, pa: (last(i, be, nu, nx, pa), 0)),
            hbm, bspec(de), hbm, bspec(de), hbm, bspec(d),
        ],
        out_specs=pl.BlockSpec((te, d), lambda i, be, nu, nx, pa: (i, 0)),
        scratch_shapes=[pltpu.VMEM((2, 3, d, de), F32), pltpu.VMEM((d, de), BF16), pltpu.VMEM((d, de), BF16),
                        pltpu.VMEM((de, d), BF16), pltpu.SemaphoreType.DMA((2,))],
    )
    return pl.pallas_call(
        _expert_kernel,
        grid_spec=grid_spec,
        out_shape=jax.ShapeDtypeStruct((n_slots, d), F32),
        compiler_params=pltpu.CompilerParams(
            dimension_semantics=("arbitrary",), vmem_limit_bytes=VMEM_LIMIT),
        name="experts",
    )(block_expert, n_used, next_expert, parity, xbuf, w_gate, b_gate.reshape(ne, 1, de), w_up,
      b_up.reshape(ne, 1, de), w_down, b_down.reshape(ne, 1, d))


def _combine_kernel(tab_ref, tab_next_ref, ybuf_hbm, h1_ref, lcol_ref, gate_ref, p_ref, gn_ref, wg_ref,
                    wp_ref, pn_ref, o_ref, ysort, sems):
    i = pl.program_id(0)
    slot = i % 2

    def pieces(buf_slot):
        def piece(off_l, off_g, size):
            return pltpu.make_async_copy(ybuf_hbm.at[pl.ds(off_g, size)],
                                         ysort.at[buf_slot, pl.ds(off_l, size)], sems.at[buf_slot])
        return piece

    @pl.when(i == 0)
    def _():
        ysort[...] = jnp.zeros_like(ysort)
        _start_segments(tab_ref, pieces(0))

    _wait_segments(tab_ref, pieces(slot))
    pl.when(i < pl.num_programs(0) - 1)(lambda: _start_segments(tab_next_ref, pieces(1 - slot)))

    tile, rows = h1_ref.shape[0], ysort.shape[1]
    pos = lax.broadcasted_iota(I32, (tile, rows), 1)
    lcol = lcol_ref[...]
    gates = gate_ref[...]
    g = jnp.zeros((tile, rows), F32)
    for k in range(TOP_K):
        g = g + jnp.where(pos == lcol[:, k:k + 1], gates[:, k:k + 1], 0.0)
    gb = g.astype(BF16)
    half = ysort.shape[2] // 2
    moe = jnp.concatenate(
        [jnp.dot(gb, ysort[slot, :, :half].astype(BF16), preferred_element_type=F32),
         jnp.dot(gb, ysort[slot, :, half:].astype(BF16), preferred_element_type=F32)], axis=1)
    h2 = h1_ref[...] + moe

    hn = h2 * lax.rsqrt(jnp.mean(h2 * h2, axis=-1, keepdims=True) + EPS) * gn_ref[...]
    gate = jax.nn.sigmoid(jnp.dot(hn.astype(BF16), wg_ref[...], preferred_element_type=F32))
    e = jnp.dot(p_ref[...].astype(BF16), wp_ref[...], preferred_element_type=F32)
    e = e * lax.rsqrt(jnp.mean(e * e, axis=-1, keepdims=True) + EPS) * pn_ref[...]
    o_ref[...] = h2 + gate * e


def _combine_ple(ybuf, tab, lcol, h1, gates, p2, ple_gate_norm, w_ple_gate, w_ple_proj, ple_post_norm):
    n, d = h1.shape
    tile = n // tab.shape[0]
    full = lambda shape: pl.BlockSpec(shape, lambda i: (0,) * len(shape))
    row = lambda width: pl.BlockSpec((tile, width), lambda i: (i, 0))
    return pl.pallas_call(
        _combine_kernel,
        grid=(n // tile,),
        in_specs=[
            pl.BlockSpec((1, 1, 3 * N_EXPERTS), lambda i: (i, 0, 0), memory_space=pltpu.SMEM),
            pl.BlockSpec((1, 1, 3 * N_EXPERTS), lambda i: (jnp.minimum(i + 1, n // tile - 1), 0, 0),
                         memory_space=pltpu.SMEM),
            pl.BlockSpec(memory_space=pl.ANY),
            row(d), row(TOP_K), row(TOP_K), row(p2.shape[1]),
            full((1, d)), full(w_ple_gate.shape), full(w_ple_proj.shape), full((1, d)),
        ],
        out_specs=row(d),
        out_shape=jax.ShapeDtypeStruct((n, d), F32),
        scratch_shapes=[pltpu.VMEM((2, _sorted_rows(tile), d), F32), pltpu.SemaphoreType.DMA((2,))],
        compiler_params=pltpu.CompilerParams(
            dimension_semantics=("arbitrary",), vmem_limit_bytes=VMEM_LIMIT),
        name="combine_ple",
    )(tab, tab, ybuf, h1, lcol, gates, p2, ple_gate_norm.astype(F32)[None, :],
      w_ple_gate.astype(BF16), w_ple_proj.astype(BF16), ple_post_norm.astype(F32)[None, :])


def _layer(h, p_i, positions, lambda_init, attn_norm, w_in, q_norm, k_norm, lam_q1, lam_k1, lam_q2,
           lam_k2, subln, w_pool, pool_scale, w_out, ffn_norm, w_router, b_router, w_gate, b_gate,
           w_up, b_up, w_down, b_down, ple_gate_norm, w_ple_gate, w_ple_proj, ple_post_norm):
    batch, seq, d = h.shape
    n = batch * seq
    x2 = h.reshape(n, d)
    freqs = ROPE_THETA ** (-jnp.arange(0, ROPE_DIM, 2, dtype=F32) / ROPE_DIM)

    z = _in_proj(x2, positions.reshape(n).astype(I32), freqs[:, None], attn_norm, w_in, q_norm, k_norm)
    a_out = _attention(z, batch, seq, lam_q1, lam_k1, lam_q2, lam_k2, subln, lambda_init)
    h1, xn, gates, route, cnt_lanes = _mix_and_route(
        x2, a_out, z, seq, w_pool, pool_scale, w_out, ffn_norm, w_router, b_router)

    te = EXPERT_TILE
    nt = cnt_lanes.shape[0] // N_EXPERTS
    cnt = cnt_lanes[:, 0].reshape(nt, N_EXPERTS)
    seg = (cnt + SEG_ALIGN - 1) // SEG_ALIGN * SEG_ALIGN
    lstart = jnp.cumsum(seg, axis=1) - seg
    total = jnp.sum(seg, axis=0)
    padded = (total + te - 1) // te * te
    pend = jnp.cumsum(padded).astype(I32)
    gstart = (pend - padded)[None, :] + jnp.cumsum(seg, axis=0) - seg
    tab = jnp.concatenate([lstart, seg, gstart], axis=1).astype(I32).reshape(nt, 1, 3 * N_EXPERTS)

    experts = jnp.arange(N_EXPERTS, dtype=I32)
    nb =-(-(n * TOP_K + nt * N_EXPERTS * (SEG_ALIGN - 1) + N_EXPERTS * (te - 1)) // te)
    n_used = pend[-1] // te
    first_row = jnp.minimum(jnp.arange(nb, dtype=I32), n_used - 1) * te
    block_expert = jnp.minimum(jnp.sum(pend[None, :] <= first_row[:, None], axis=1), N_EXPERTS - 1).astype(I32)

    owns = padded > 0
    later = jnp.where(owns[None, :] & (experts[None, :] > experts[:, None]), experts[None, :], N_EXPERTS)
    next_owner = jnp.min(later, axis=1)
    next_owner = jnp.where(next_owner < N_EXPERTS, next_owner, -1).astype(I32)
    next_expert = next_owner[block_expert]
    parity = (jnp.cumsum(owns.astype(I32)) - 1)[block_expert] % 2

    xbuf, lpos = _dispatch(xn, tab, route, pend, padded.astype(I32), nb)
    ybuf = _experts(xbuf, block_expert, n_used[None], next_expert, parity.astype(I32), w_gate, b_gate, w_up,
                    b_up, w_down, b_down)
    out = _combine_ple(ybuf, tab, lpos[:TOP_K].T, h1, gates[:TOP_K].T, p_i.reshape(n, -1), ple_gate_norm, w_ple_gate,
                       w_ple_proj, ple_post_norm)
    return out.reshape(batch, seq, d)


def kernel(x, p, positions, attn_norm, w_in, q_norm, k_norm, lam_q1, lam_k1, lam_q2, lam_k2, subln,
           w_pool, pool_scale, w_out, ffn_norm, w_router, b_router, w_gate, b_gate, w_up, b_up,
           w_down, b_down, ple_gate_norm, w_ple_gate, w_ple_proj, ple_post_norm):
    h = x
    for i in range(attn_norm.shape[0]):
        lambda_init = 0.8 - 0.6 * math.exp(-0.3 * i)
        h = _layer(h, p[i], positions, lambda_init, attn_norm[i], w_in[i], q_norm[i], k_norm[i],
                   lam_q1[i], lam_k1[i], lam_q2[i], lam_k2[i], subln[i], w_pool[i], pool_scale[i],
                   w_out[i], ffn_norm[i], w_router[i], b_router[i], w_gate[i], b_gate[i], w_up[i],
                   b_up[i], w_down[i], b_down[i], ple_gate_norm[i], w_ple_gate[i], w_ple_proj[i],
                   ple_post_norm[i])
    return h
```

```python
import functools
import math

import jax
import jax.numpy as jnp
from jax import lax
from jax.experimental import pallas as pl
from jax.experimental.pallas import tpu as pltpu

F32 = jnp.float32
BF16 = jnp.bfloat16
I32 = jnp.int32

N_HEADS = 4
HEAD_DIM = 64
V_DIM = 2 * HEAD_DIM
ATTN_W = N_HEADS * V_DIM
POOL_WINDOWS = (2, 4, 8, 16)
POOL_GROUP = 128
POOL_W = len(POOL_WINDOWS) * POOL_GROUP
POOL_HALO = 16
ROPE_DIM = HEAD_DIM // 4
ROPE_HALF = ROPE_DIM // 2
ROPE_THETA = 500000.0
N_EXPERTS = 32
TOP_K = 4
SWIGLU_LIMIT = 7.0
SWIGLU_ALPHA = 1.702
EPS = 1e-6
LANES = 128
MXU_DIM = 256
VMEM_LIMIT = 56 * 1024 * 1024

ROW_TILE = 512
Q_TILE = 512
EXPERT_TILE = 512
SEG_ALIGN = 8
SEG_SIZES = (512, 256, 128, 64, 32, 16, 8)
assert SEG_SIZES[0] >= ROW_TILE and SEG_SIZES[-1] == SEG_ALIGN


def _inproj_kernel(x_ref, pos_ref, freq_ref, an_ref, w_ref, qn_ref, kn_ref, g_ref, z_ref):
    x = x_ref[...]
    ms = jnp.mean(x * x, axis=-1, keepdims=True)
    hn = (x * lax.rsqrt(ms + EPS) * an_ref[...]).astype(BF16)

    tm = x.shape[0]
    ang = freq_ref[...] * pos_ref[0].astype(F32)
    packed = jnp.concatenate([jnp.cos(ang), jnp.sin(ang), jnp.zeros((LANES - ROPE_DIM, tm), F32)], axis=0)
    base = packed.T
    base = base + pltpu.roll(base, HEAD_DIM, 1)
    lane = lax.broadcasted_iota(I32, base.shape, 1) & (HEAD_DIM - 1)
    c = jnp.where(lane < ROPE_HALF, base, jnp.where(lane < ROPE_DIM, pltpu.roll(base, ROPE_HALF, 1), 1.0))
    s_dn = jnp.where((lane >= ROPE_HALF) & (lane < ROPE_DIM), base, 0.0)
    s_up = jnp.where(lane < ROPE_HALF, -pltpu.roll(base, LANES - ROPE_HALF, 1), 0.0)

    for part, (nrm_ref, scale) in enumerate(((qn_ref, 1.0 / math.sqrt(HEAD_DIM)), (kn_ref, 1.0))):
        zc = jnp.dot(hn, w_ref[:, part * ATTN_W:(part + 1) * ATTN_W], preferred_element_type=F32)
        for half in range(ATTN_W // 256):
            zh = zc[:, half * 256:(half + 1) * 256]
            ss = jnp.dot((zh * zh).astype(BF16), g_ref[...], preferred_element_type=F32)
            y = zh * lax.rsqrt(ss * (1.0 / HEAD_DIM) + EPS) * nrm_ref[...]
            for blk in range(2):
                yb = y[:, blk * LANES:(blk + 1) * LANES]
                r = yb * c + pltpu.roll(yb, ROPE_HALF, 1) * s_dn + pltpu.roll(yb, LANES - ROPE_HALF, 1) * s_up
                col = part * ATTN_W + half * 256 + blk * LANES
                z_ref[:, col:col + LANES] = (r * scale).astype(BF16)

    zc = jnp.dot(hn, w_ref[:, 2 * ATTN_W:], preferred_element_type=F32)
    z_ref[:, 2 * ATTN_W:] = zc.astype(BF16)


def _in_proj(x2, pos, freq, attn_norm, w_in, q_norm, k_norm):
    n, d = x2.shape
    d_in = w_in.shape[1]
    tm = min(ROW_TILE, n)
    pos3 = pos.reshape(n // tm, 1, tm)
    gid = jnp.arange(256) // HEAD_DIM
    group_ones = (gid[:, None] == gid[None, :]).astype(BF16)
    qn = jnp.tile(q_norm.astype(F32), 256 // HEAD_DIM)[None, :]
    kn = jnp.tile(k_norm.astype(F32), 256 // HEAD_DIM)[None, :]
    full = lambda shape: pl.BlockSpec(shape, lambda i: (0,) * len(shape))
    return pl.pallas_call(
        _inproj_kernel,
        grid=(n // tm,),
        in_specs=[
            pl.BlockSpec((tm, d), lambda i: (i, 0)),
            pl.BlockSpec((1, 1, tm), lambda i: (i, 0, 0)),
            full((ROPE_HALF, 1)),
            full((1, d)),
            full((d, d_in)),
            full((1, 256)),
            full((1, 256)),
            full((256, 256)),
        ],
        out_specs=pl.BlockSpec((tm, d_in), lambda i: (i, 0)),
        out_shape=jax.ShapeDtypeStruct((n, d_in), BF16),
        compiler_params=pltpu.CompilerParams(
            dimension_semantics=("arbitrary",), vmem_limit_bytes=VMEM_LIMIT),
        name="in_proj",
    )(x2, pos3, freq, attn_norm.astype(F32)[None, :], w_in.astype(BF16), qn, kn, group_ones)


def _attn_kernel(lq1_ref, lk1_ref, lq2_ref, lk2_ref, sub_ref, q_ref, k_ref, v_ref, o_ref,
                 acc_ref, sa_ref, sb_ref, *, tq, lambda_init):
    i = pl.program_id(2)
    lam = (jnp.exp(jnp.sum(lq1_ref[...] * lk1_ref[...], keepdims=True))
           - jnp.exp(jnp.sum(lq2_ref[...] * lk2_ref[...], keepdims=True)) + lambda_init)

    q = q_ref[...]
    lane = lax.broadcasted_iota(I32, q.shape, 1)
    zero = jnp.zeros_like(q)
    qs = jnp.concatenate([jnp.where(lane < HEAD_DIM, q, zero),
                          jnp.where(lane >= HEAD_DIM, q, zero)], axis=0)

    ones = jnp.ones((tq, V_DIM), BF16)
    acc_ref[...] = jnp.zeros(acc_ref.shape, F32)

    def scores(j, s_ref):
        k = k_ref[pl.ds(pl.multiple_of(j * tq, tq), tq), :]
        nt = (((1,), (1,)), ((), ()))
        s_ref[:, :tq // 2] = lax.dot_general(qs, k[:tq // 2], nt, preferred_element_type=F32)
        s_ref[:, tq // 2:] = lax.dot_general(qs, k[tq // 2:], nt, preferred_element_type=F32)

    def update(j, s_ref, m, masked=False):
        v1 = jnp.concatenate([v_ref[pl.ds(pl.multiple_of(j * tq, tq), tq), :], ones], axis=1)
        s = s_ref[...]
        if masked:
            row = lax.broadcasted_iota(I32, s.shape, 0)
            row = jnp.where(row >= tq, row - tq, row)
            col = lax.broadcasted_iota(I32, s.shape, 1)
            s = jnp.where(col <= row, s, -jnp.inf)
        m_new = jnp.maximum(m, jnp.max(s, axis=1, keepdims=True))
        p = jnp.exp(s - m_new).astype(BF16)
        alpha = jnp.exp(m - m_new)
        for c in range(2):
            rows = slice(c * tq, (c + 1) * tq)
            acc_ref[rows, :] = alpha[rows] * acc_ref[rows, :] + jnp.dot(p[rows], v1, preferred_element_type=F32)
        return m_new

    def pair(jj, m):
        j = 2 * jj
        scores(j + 1, sb_ref)
        m = update(j, sa_ref, m)
        scores(j + 2, sa_ref)
        return update(j + 1, sb_ref, m)

    scores(0, sa_ref)
    m = lax.fori_loop(0, i // 2, pair, jnp.full((2 * tq, 1), -jnp.inf, F32))

    @pl.when(i % 2 == 0)
    def _():
        update(i, sa_ref, m, masked=True)

    @pl.when(i % 2 == 1)
    def _():
        scores(i, sb_ref)
        update(i, sb_ref, update(i - 1, sa_ref, m), masked=True)
    o = acc_ref[:, :V_DIM] / acc_ref[:, V_DIM:]
    o = o[:tq] - lam * o[tq:]
    o = o * lax.rsqrt(jnp.mean(o * o, axis=-1, keepdims=True) + EPS) * sub_ref[...] * (1.0 - lambda_init)
    o_ref[...] = o.astype(o_ref.dtype)


def _attention(z, batch, seq, lam_q1, lam_k1, lam_q2, lam_k2, subln, lambda_init):
    n = z.shape[0]
    tq = min(Q_TILE, seq)
    nq = seq // tq
    kcol = ATTN_W // V_DIM
    vec = lambda: pl.BlockSpec((1, HEAD_DIM), lambda b, h, i: (0, 0))
    return pl.pallas_call(
        functools.partial(_attn_kernel, tq=tq, lambda_init=lambda_init),
        grid=(batch, N_HEADS, nq),
        in_specs=[
            vec(), vec(), vec(), vec(),
            pl.BlockSpec((1, V_DIM), lambda b, h, i: (0, 0)),
            pl.BlockSpec((tq, V_DIM), lambda b, h, i: (b * nq + i, h)),
            pl.BlockSpec((seq, V_DIM), lambda b, h, i: (b, kcol + h)),
            pl.BlockSpec((seq, V_DIM), lambda b, h, i: (b, 2 * kcol + h)),
        ],
        out_specs=pl.BlockSpec((tq, V_DIM), lambda b, h, i: (b * nq + i, h)),
        out_shape=jax.ShapeDtypeStruct((n, ATTN_W), BF16),
        scratch_shapes=[pltpu.VMEM((2 * tq, 2 * V_DIM), F32), pltpu.VMEM((2 * tq, tq), F32),
                        pltpu.VMEM((2 * tq, tq), F32)],
        compiler_params=pltpu.CompilerParams(
            dimension_semantics=("arbitrary", "arbitrary", "arbitrary"), vmem_limit_bytes=VMEM_LIMIT),
        name="diff_attn",
    )(lam_q1.astype(F32)[None, :], lam_k1.astype(F32)[None, :], lam_q2.astype(F32)[None, :],
      lam_k2.astype(F32)[None, :], subln.astype(F32)[None, :], z, z, z)


def _mix_kernel(x_ref, a_ref, u_ref, wpool_ref, pscale_ref, wout_ref, fn_ref, wr_ref, br_ref,
                h1_ref, xn_ref, gate_ref, route_ref, cnt_ref,
                ubuf, *, tm, tiles_per_seq):
    i = pl.program_id(0)

    @pl.when(i % tiles_per_seq == 0)
    def _():
        ubuf[0:POOL_HALO, :] = jnp.zeros((POOL_HALO, POOL_W), F32)

    ubuf[POOL_HALO:POOL_HALO + tm, :] = u_ref[...].astype(F32)

    t_seq = (i % tiles_per_seq) * tm + lax.broadcasted_iota(I32, (tm, 1), 0)
    mix = [a_ref[...]]
    for g, w in enumerate(POOL_WINDOWS):
        cols = slice(g * POOL_GROUP, (g + 1) * POOL_GROUP)
        cur = ubuf[POOL_HALO:POOL_HALO + tm, cols]
        win = cur
        for j in range(1, w):
            win = win + ubuf[POOL_HALO - j:POOL_HALO - j + tm, cols]
        count = jnp.minimum(t_seq + 1, w).astype(F32)
        pooled = win / count - cur
        pg = jnp.dot(pooled.astype(BF16), wpool_ref[g], preferred_element_type=F32) * pscale_ref[:, cols]
        mix.append(pg.astype(BF16))
    ubuf[0:POOL_HALO, :] = ubuf[tm:tm + POOL_HALO, :]
    mixed = jnp.dot(jnp.concatenate(mix, axis=1), wout_ref[...], preferred_element_type=F32)

    h1 = x_ref[...] + mixed
    h1_ref[...] = h1
    xn = h1 * lax.rsqrt(jnp.mean(h1 * h1, axis=-1, keepdims=True) + EPS) * fn_ref[...]
    xn_ref[...] = xn

    logits = lax.dot_general(wr_ref[...], xn.astype(BF16), (((1,), (1,)), ((), ())),
                             preferred_element_type=F32) + br_ref[...]
    expert = lax.broadcasted_iota(I32, logits.shape, 0).astype(F32)
    vals, idxs, hots = [], [], []
    rest = logits
    for _ in range(TOP_K):
        mx = jnp.max(rest, axis=0, keepdims=True)
        idx = jnp.min(jnp.where(rest == mx, expert, float(N_EXPERTS)), axis=0, keepdims=True)
        hot = expert == idx
        vals.append(mx)
        idxs.append(idx)
        hots.append(hot)
        rest = jnp.where(hot, -jnp.inf, rest)
    exps = [jnp.exp(v - vals[0]) for v in vals]
    denom = exps[0] + exps[1] + exps[2] + exps[3]

    chosen = jnp.zeros(logits.shape, F32)
    for hot in hots:
        chosen = chosen + hot.astype(F32)
    earlier = lax.broadcasted_iota(I32, (tm, tm), 0) < lax.broadcasted_iota(I32, (tm, tm), 1)
    before = jnp.dot(chosen.astype(BF16), jnp.where(earlier, 1.0, 0.0).astype(BF16), preferred_element_type=F32)

    ranks = [jnp.sum(jnp.where(hot, before, 0.0), axis=0, keepdims=True) for hot in hots]
    route_ref[...] = jnp.concatenate(idxs + ranks, axis=0).astype(I32)
    gate_ref[...] = jnp.concatenate([e / denom for e in exps] + [jnp.zeros_like(denom)] * TOP_K, axis=0)
    cnt_ref[...] = jnp.broadcast_to(jnp.sum(chosen, axis=1, keepdims=True), cnt_ref.shape).astype(I32)


def _mix_and_route(x2, a_out, z, seq, w_pool, pool_scale, w_out, ffn_norm, w_router, b_router):
    n, d = x2.shape
    tm = min(ROW_TILE, seq)
    ucol = (z.shape[1] - POOL_W) // POOL_W
    full = lambda shape: pl.BlockSpec(shape, lambda i: (0,) * len(shape))
    row = lambda width: pl.BlockSpec((tm, width), lambda i: (i, 0))
    return pl.pallas_call(
        functools.partial(_mix_kernel, tm=tm, tiles_per_seq=seq // tm),
        grid=(n // tm,),
        in_specs=[
            row(d),
            row(ATTN_W),
            pl.BlockSpec((tm, POOL_W), lambda i: (i, ucol)),
            full(w_pool.shape),
            full((1, POOL_W)),
            full(w_out.shape),
            full((1, d)),
            full((N_EXPERTS, d)),
            full((N_EXPERTS, 1)),
        ],
        out_specs=[row(d), row(d),
                   pl.BlockSpec((2 * TOP_K, tm), lambda i: (0, i)),
                   pl.BlockSpec((2 * TOP_K, tm), lambda i: (0, i)),
                   pl.BlockSpec((N_EXPERTS, LANES), lambda i: (i, 0))],
        out_shape=[
            jax.ShapeDtypeStruct((n, d), F32),
            jax.ShapeDtypeStruct((n, d), F32),
            jax.ShapeDtypeStruct((2 * TOP_K, n), F32),
            jax.ShapeDtypeStruct((2 * TOP_K, n), I32),
            jax.ShapeDtypeStruct((n // tm * N_EXPERTS, LANES), I32),
        ],
        scratch_shapes=[pltpu.VMEM((POOL_HALO + tm, POOL_W), F32)],
        compiler_params=pltpu.CompilerParams(
            dimension_semantics=("arbitrary",), vmem_limit_bytes=VMEM_LIMIT),
        name="mix_route",
    )(x2, a_out, z, w_pool.astype(BF16), pool_scale.astype(F32)[None, :], w_out.astype(BF16),
      ffn_norm.astype(F32)[None, :], w_router.T.astype(BF16), b_router.astype(F32)[:, None])


def _sorted_rows(tile):
    rows = TOP_K * tile + N_EXPERTS * (SEG_ALIGN - 1)
    return -(-rows // MXU_DIM) * MXU_DIM


def _segment_copies(tab_ref, make_copy):
    for e in range(N_EXPERTS):
        off_l = tab_ref[0, 0, e]
        length = tab_ref[0, 0, N_EXPERTS + e]
        off_g = tab_ref[0, 0, 2 * N_EXPERTS + e]
        for size in SEG_SIZES:
            take = (length & size) != 0
            yield take, make_copy(pl.multiple_of(off_l, SEG_ALIGN), pl.multiple_of(off_g, SEG_ALIGN), size)
            step = jnp.where(take, size, 0)
            off_l = off_l + step
            off_g = off_g + step


def _start_segments(tab_ref, piece):
    for cond, cp in _segment_copies(tab_ref, piece):
        pl.when(cond)(cp.start)


def _wait_segments(tab_ref, piece):
    for e in range(N_EXPERTS):
        length = tab_ref[0, 0, N_EXPERTS + e]
        for size in SEG_SIZES:
            pl.when((length & size) != 0)(piece(0, 0, size).wait)


def _dispatch_kernel(pend_ref, padded_ref, tab_ref, tab_prev_ref, route_ref, x_ref, xbuf_hbm, lpos_ref,
                     sbuf, zbuf, sems, zsem, *, te, nb):
    i = pl.program_id(0)
    slot = i % 2

    @pl.when(i == 0)
    def _():
        zbuf[...] = jnp.zeros_like(zbuf)
        n_used = pend_ref[N_EXPERTS - 1] // te

        def fills():
            for e in range(N_EXPERTS):
                start = pl.multiple_of(jnp.maximum(pend_ref[e] - te, 0), te)
                yield padded_ref[e] > 0, pltpu.make_async_copy(zbuf, xbuf_hbm.at[pl.ds(start, te)], zsem)
            for e in range(N_EXPERTS):
                blk = n_used + e
                start = pl.multiple_of(jnp.minimum(blk, nb - 1) * te, te)
                yield blk < nb, pltpu.make_async_copy(zbuf, xbuf_hbm.at[pl.ds(start, te)], zsem)

        for cond, cp in fills():
            pl.when(cond)(cp.start)
        for cond, cp in fills():
            pl.when(cond)(cp.wait)

    rows, tile = sbuf.shape[1], x_ref.shape[0]
    eidx = route_ref[0:TOP_K, :]
    lpos = route_ref[TOP_K:, :]
    for e in range(N_EXPERTS):
        lpos = lpos + jnp.where(eidx == e, tab_ref[0, 0, e], 0)
    lpos_ref[...] = jnp.concatenate([lpos, jnp.zeros_like(lpos)], axis=0)
    pos = lax.broadcasted_iota(I32, (rows, tile), 0)
    hit = pos == lpos[0:1, :]
    for k in range(1, TOP_K):
        hit = jnp.logical_or(hit, pos == lpos[k:k + 1, :])
    perm = jnp.where(hit, 1.0, 0.0).astype(BF16)
    xb = x_ref[...].astype(BF16)
    half = xb.shape[1] // 2
    sbuf[slot, :, :half] = jnp.dot(perm, xb[:, :half], preferred_element_type=F32)
    sbuf[slot, :, half:] = jnp.dot(perm, xb[:, half:], preferred_element_type=F32)

    def pieces(buf_slot):
        def piece(off_l, off_g, size):
            return pltpu.make_async_copy(sbuf.at[buf_slot, pl.ds(off_l, size)],
                                         xbuf_hbm.at[pl.ds(off_g, size)], sems.at[buf_slot])
        return piece

    _start_segments(tab_ref, pieces(slot))
    pl.when(i > 0)(lambda: _wait_segments(tab_prev_ref, pieces(1 - slot)))
    pl.when(i == pl.num_programs(0) - 1)(lambda: _wait_segments(tab_ref, pieces(slot)))


def _dispatch(xn, tab, route, pend, padded, nb):
    n, width = xn.shape
    tile = n // tab.shape[0]
    te = EXPERT_TILE
    grid_spec = pltpu.PrefetchScalarGridSpec(
        num_scalar_prefetch=2,
        grid=(n // tile,),
        in_specs=[
            pl.BlockSpec((1, 1, 3 * N_EXPERTS), lambda i, pe, pa: (i, 0, 0), memory_space=pltpu.SMEM),
            pl.BlockSpec((1, 1, 3 * N_EXPERTS), lambda i, pe, pa: (jnp.maximum(i - 1, 0), 0, 0),
                         memory_space=pltpu.SMEM),
            pl.BlockSpec((2 * TOP_K, tile), lambda i, pe, pa: (0, i)),
            pl.BlockSpec((tile, width), lambda i, pe, pa: (i, 0)),
        ],
        out_specs=[pl.BlockSpec(memory_space=pl.ANY),
                   pl.BlockSpec((2 * TOP_K, tile), lambda i, pe, pa: (0, i))],
        scratch_shapes=[pltpu.VMEM((2, _sorted_rows(tile), width), F32), pltpu.VMEM((te, width), F32),
                        pltpu.SemaphoreType.DMA((2,)), pltpu.SemaphoreType.DMA(())],
    )
    return pl.pallas_call(
        functools.partial(_dispatch_kernel, te=te, nb=nb),
        grid_spec=grid_spec,
        out_shape=[jax.ShapeDtypeStruct((nb * te, width), F32),
                   jax.ShapeDtypeStruct((2 * TOP_K, n), I32)],
        compiler_params=pltpu.CompilerParams(
            dimension_semantics=("arbitrary",), has_side_effects=True, vmem_limit_bytes=VMEM_LIMIT),
        name="dispatch",
    )(pend, padded, tab, tab, route, xn)


def _expert_kernel(be_ref, nu_ref, nxt_ref, par_ref, x_ref, wg_hbm, bg_ref, wu_hbm, bu_ref, wd_hbm, bd_ref,
                   y_ref, wbuf, wg_bf, wu_bf, wd_bf, sems):
    i = pl.program_id(0)
    prev = be_ref[jnp.maximum(i - 1, 0)]
    fresh = jnp.logical_or(i == 0, be_ref[i] != prev)

    def weight_copies(e, slot):
        return [pltpu.make_async_copy(w_hbm.at[e], wbuf.at[slot, j], sems.at[slot])
                for j, w_hbm in enumerate((wg_hbm, wu_hbm, wd_hbm))]

    @pl.when(jnp.logical_and(fresh, i < nu_ref[0]))
    def _():
        slot = par_ref[i]

        @pl.when(i == 0)
        def _():
            for cp in weight_copies(be_ref[i], slot):
                cp.start()

        for cp in weight_copies(be_ref[i], slot):
            cp.wait()
        wg_bf[...] = wbuf[slot, 0].astype(BF16)
        wu_bf[...] = wbuf[slot, 1].astype(BF16)
        wd_bf[...] = wbuf[slot, 2].astype(BF16)

        @pl.when(nxt_ref[i] >= 0)
        def _():
            for cp in weight_copies(nxt_ref[i], 1 - slot):
                cp.start()

    @pl.when(i < nu_ref[0])
    def _():
        xb = x_ref[...].astype(BF16)

        def proj(w_bf, b_ref):
            return jnp.dot(xb, w_bf[...], preferred_element_type=F32) + b_ref[0]

        gt = jnp.minimum(proj(wg_bf, bg_ref), SWIGLU_LIMIT)
        up = jnp.clip(proj(wu_bf, bu_ref), -SWIGLU_LIMIT, SWIGLU_LIMIT)
        hdn = (up + 1.0) * (gt * jax.nn.sigmoid(SWIGLU_ALPHA * gt))
        y_ref[...] = jnp.dot(hdn.astype(BF16), wd_bf[...], preferred_element_type=F32) + bd_ref[0]

    @pl.when(i >= nu_ref[0])
    def _():
        y_ref[...] = jnp.zeros_like(y_ref)


def _experts(xbuf, block_expert, n_used, next_expert, parity, w_gate, b_gate, w_up, b_up, w_down, b_down):
    n_slots, width = xbuf.shape
    ne, d, de = w_gate.shape
    assert w_down.shape == (ne, d, de), "one (2, 3, d, de) staging buffer holds all three weights"
    te = EXPERT_TILE
    nb = n_slots // te
    last = lambda i, be, nu, nx, pa: jnp.minimum(i, nu[0] - 1)
    bspec = lambda width: pl.BlockSpec((1, 1, width), lambda i, be, nu, nx, pa: (be[i], 0, 0))
    hbm = pl.BlockSpec(memory_space=pl.ANY)
    grid_spec = pltpu.PrefetchScalarGridSpec(
        num_scalar_prefetch=4,
        grid=(nb,),
        in_specs=[
            pl.BlockSpec((te, width), lambda i, be, nu, nx, pa: (last(i, be, nu, nx, pa), 0)),
            hbm, bspec(de), hbm, bspec(de), hbm, bspec(d),
        ],
        out_specs=pl.BlockSpec((te, d), lambda i, be, nu, nx, pa: (i, 0)),
        scratch_shapes=[pltpu.VMEM((2, 3, d, de), F32), pltpu.VMEM((d, de), BF16), pltpu.VMEM((d, de), BF16),
                        pltpu.VMEM((de, d), BF16), pltpu.SemaphoreType.DMA((2,))],
    )
    return pl.pallas_call(
        _expert_kernel,
        grid_spec=grid_spec,
        out_shape=jax.ShapeDtypeStruct((n_slots, d), F32),
        compiler_params=pltpu.CompilerParams(
            dimension_semantics=("arbitrary",), vmem_limit_bytes=VMEM_LIMIT),
        name="experts",
    )(block_expert, n_used, next_expert, parity, xbuf, w_gate, b_gate.reshape(ne, 1, de), w_up,
      b_up.reshape(ne, 1, de), w_down, b_down.reshape(ne, 1, d))


def _combine_kernel(tab_ref, tab_next_ref, ybuf_hbm, h1_ref, lpos_ref, gate_ref, p_ref, gn_ref, wg_ref,
                    wp_ref, pn_ref, o_ref, ysort, sems):
    i = pl.program_id(0)
    slot = i % 2

    def pieces(buf_slot):
        def piece(off_l, off_g, size):
            return pltpu.make_async_copy(ybuf_hbm.at[pl.ds(off_g, size)],
                                         ysort.at[buf_slot, pl.ds(off_l, size)], sems.at[buf_slot])
        return piece

    @pl.when(i == 0)
    def _():
        ysort[...] = jnp.zeros_like(ysort)
        _start_segments(tab_ref, pieces(0))

    _wait_segments(tab_ref, pieces(slot))
    pl.when(i < pl.num_programs(0) - 1)(lambda: _start_segments(tab_next_ref, pieces(1 - slot)))

    tile, rows = h1_ref.shape[0], ysort.shape[1]
    pos = lax.broadcasted_iota(I32, (tile, rows), 1)
    cols = jnp.concatenate([lpos_ref[0:TOP_K, :].astype(F32), gate_ref[0:TOP_K, :],
                            jnp.zeros((LANES - 2 * TOP_K, tile), F32)], axis=0).T
    g = jnp.zeros((tile, rows), F32)
    for k in range(TOP_K):
        g = jnp.where(pos == cols[:, k:k + 1].astype(I32), cols[:, TOP_K + k:TOP_K + k + 1], g)
    gb = g.astype(BF16)
    half = ysort.shape[2] // 2
    moe = jnp.concatenate(
        [jnp.dot(gb, ysort[slot, :, :half].astype(BF16), preferred_element_type=F32),
         jnp.dot(gb, ysort[slot, :, half:].astype(BF16), preferred_element_type=F32)], axis=1)
    h2 = h1_ref[...] + moe

    hn = h2 * lax.rsqrt(jnp.mean(h2 * h2, axis=-1, keepdims=True) + EPS) * gn_ref[...]
    gate = jax.nn.sigmoid(jnp.dot(hn.astype(BF16), wg_ref[...], preferred_element_type=F32))
    e = jnp.dot(p_ref[...].astype(BF16), wp_ref[...], preferred_element_type=F32)
    e = e * lax.rsqrt(jnp.mean(e * e, axis=-1, keepdims=True) + EPS) * pn_ref[...]
    o_ref[...] = h2 + gate * e


def _combine_ple(ybuf, tab, lpos, h1, gates, p2, ple_gate_norm, w_ple_gate, w_ple_proj, ple_post_norm):
    n, d = h1.shape
    tile = n // tab.shape[0]
    full = lambda shape: pl.BlockSpec(shape, lambda i: (0,) * len(shape))
    row = lambda width: pl.BlockSpec((tile, width), lambda i: (i, 0))
    lanes = pl.BlockSpec((2 * TOP_K, tile), lambda i: (0, i))
    return pl.pallas_call(
        _combine_kernel,
        grid=(n // tile,),
        in_specs=[
            pl.BlockSpec((1, 1, 3 * N_EXPERTS), lambda i: (i, 0, 0), memory_space=pltpu.SMEM),
            pl.BlockSpec((1, 1, 3 * N_EXPERTS), lambda i: (jnp.minimum(i + 1, n // tile - 1), 0, 0),
                         memory_space=pltpu.SMEM),
            pl.BlockSpec(memory_space=pl.ANY),
            row(d), lanes, lanes, row(p2.shape[1]),
            full((1, d)), full(w_ple_gate.shape), full(w_ple_proj.shape), full((1, d)),
        ],
        out_specs=row(d),
        out_shape=jax.ShapeDtypeStruct((n, d), F32),
        scratch_shapes=[pltpu.VMEM((2, _sorted_rows(tile), d), F32), pltpu.SemaphoreType.DMA((2,))],
        compiler_params=pltpu.CompilerParams(
            dimension_semantics=("arbitrary",), vmem_limit_bytes=VMEM_LIMIT),
        name="combine_ple",
    )(tab, tab, ybuf, h1, lpos, gates, p2, ple_gate_norm.astype(F32)[None, :],
      w_ple_gate.astype(BF16), w_ple_proj.astype(BF16), ple_post_norm.astype(F32)[None, :])


def _layer(h, p_i, positions, lambda_init, attn_norm, w_in, q_norm, k_norm, lam_q1, lam_k1, lam_q2,
           lam_k2, subln, w_pool, pool_scale, w_out, ffn_norm, w_router, b_router, w_gate, b_gate,
           w_up, b_up, w_down, b_down, ple_gate_norm, w_ple_gate, w_ple_proj, ple_post_norm):
    batch, seq, d = h.shape
    n = batch * seq
    x2 = h.reshape(n, d)
    freqs = ROPE_THETA ** (-jnp.arange(0, ROPE_DIM, 2, dtype=F32) / ROPE_DIM)

    z = _in_proj(x2, positions.reshape(n).astype(I32), freqs[:, None], attn_norm, w_in, q_norm, k_norm)
    a_out = _attention(z, batch, seq, lam_q1, lam_k1, lam_q2, lam_k2, subln, lambda_init)
    h1, xn, gates, route, cnt_lanes = _mix_and_route(
        x2, a_out, z, seq, w_pool, pool_scale, w_out, ffn_norm, w_router, b_router)

    te = EXPERT_TILE
    nt = cnt_lanes.shape[0] // N_EXPERTS
    cnt = cnt_lanes[:, 0].reshape(nt, N_EXPERTS)
    seg = (cnt + SEG_ALIGN - 1) // SEG_ALIGN * SEG_ALIGN
    lstart = jnp.cumsum(seg, axis=1) - seg
    total = jnp.sum(seg, axis=0)
    padded = (total + te - 1) // te * te
    pend = jnp.cumsum(padded).astype(I32)
    gstart = (pend - padded)[None, :] + jnp.cumsum(seg, axis=0) - seg
    tab = jnp.concatenate([lstart, seg, gstart], axis=1).astype(I32).reshape(nt, 1, 3 * N_EXPERTS)

    experts = jnp.arange(N_EXPERTS, dtype=I32)
    nb =-(-(n * TOP_K + nt * N_EXPERTS * (SEG_ALIGN - 1) + N_EXPERTS * (te - 1)) // te)
    n_used = pend[-1] // te
    first_row = jnp.minimum(jnp.arange(nb, dtype=I32), n_used - 1) * te
    block_expert = jnp.minimum(jnp.sum(pend[None, :] <= first_row[:, None], axis=1), N_EXPERTS - 1).astype(I32)

    owns = padded > 0
    later = jnp.where(owns[None, :] & (experts[None, :] > experts[:, None]), experts[None, :], N_EXPERTS)
    next_owner = jnp.min(later, axis=1)
    next_owner = jnp.where(next_owner < N_EXPERTS, next_owner, -1).astype(I32)
    next_expert = next_owner[block_expert]
    parity = (jnp.cumsum(owns.astype(I32)) - 1)[block_expert] % 2

    xbuf, lpos = _dispatch(xn, tab, route, pend, padded.astype(I32), nb)
    ybuf = _experts(xbuf, block_expert, n_used[None], next_expert, parity.astype(I32), w_gate, b_gate, w_up,
                    b_up, w_down, b_down)
    out = _combine_ple(ybuf, tab, lpos, h1, gates, p_i.reshape(n, -1), ple_gate_norm, w_ple_gate,
                       w_ple_proj, ple_post_norm)
    return out.reshape(batch, seq, d)


def kernel(x, p, positions, attn_norm, w_in, q_norm, k_norm, lam_q1, lam_k1, lam_q2, lam_k2, subln,
           w_pool, pool_scale, w_out, ffn_norm, w_router, b_router, w_gate, b_gate, w_up, b_up,
           w_down, b_down, ple_gate_norm, w_ple_gate, w_ple_proj, ple_post_norm):
    h = x
    for i in range(attn_norm.shape[0]):
        lambda_init = 0.8 - 0.6 * math.exp(-0.3 * i)
        h = _layer(h, p[i], positions, lambda_init, attn_norm[i], w_in[i], q_norm[i], k_norm[i],
                   lam_q1[i], lam_k1[i], lam_q2[i], lam_k2[i], subln[i], w_pool[i], pool_scale[i],
                   w_out[i], ffn_norm[i], w_router[i], b_router[i], w_gate[i], b_gate[i], w_up[i],
                   b_up[i], w_down[i], b_down[i], ple_gate_norm[i], w_ple_gate[i], w_ple_proj[i],
                   ple_post_norm[i])
    return h
```

```python
import functools
import math

import jax
import jax.numpy as jnp
from jax import lax
from jax.experimental import pallas as pl
from jax.experimental.pallas import tpu as pltpu

F32 = jnp.float32
BF16 = jnp.bfloat16
I32 = jnp.int32

N_HEADS = 4
HEAD_DIM = 64
V_DIM = 2 * HEAD_DIM
ATTN_W = N_HEADS * V_DIM
POOL_WINDOWS = (2, 4, 8, 16)
POOL_GROUP = 128
POOL_W = len(POOL_WINDOWS) * POOL_GROUP
POOL_HALO = 16
ROPE_DIM = HEAD_DIM // 4
ROPE_HALF = ROPE_DIM // 2
ROPE_THETA = 500000.0
N_EXPERTS = 32
TOP_K = 4
SWIGLU_LIMIT = 7.0
SWIGLU_ALPHA = 1.702
EPS = 1e-6
LANES = 128
MXU_DIM = 256
VMEM_LIMIT = 56 * 1024 * 1024

ROW_TILE = 512
Q_TILE = 512
EXPERT_TILE = 512
SEG_ALIGN = 8
SEG_SIZES = (512, 256, 128, 64, 32, 16, 8)
assert SEG_SIZES[0] >= ROW_TILE and SEG_SIZES[-1] == SEG_ALIGN


def _inproj_kernel(x_ref, pos_ref, freq_ref, an_ref, w_ref, qn_ref, kn_ref, g_ref, z_ref):
    x = x_ref[...]
    ms = jnp.mean(x * x, axis=-1, keepdims=True)
    hn = (x * lax.rsqrt(ms + EPS) * an_ref[...]).astype(BF16)

    tm = x.shape[0]
    ang = freq_ref[...] * pos_ref[0].astype(F32)
    packed = jnp.concatenate([jnp.cos(ang), jnp.sin(ang), jnp.zeros((LANES - ROPE_DIM, tm), F32)], axis=0)
    base = packed.T
    base = base + pltpu.roll(base, HEAD_DIM, 1)
    lane = lax.broadcasted_iota(I32, base.shape, 1) & (HEAD_DIM - 1)
    c = jnp.where(lane < ROPE_HALF, base, jnp.where(lane < ROPE_DIM, pltpu.roll(base, ROPE_HALF, 1), 1.0))
    s_dn = jnp.where((lane >= ROPE_HALF) & (lane < ROPE_DIM), base, 0.0)
    s_up = jnp.where(lane < ROPE_HALF, -pltpu.roll(base, LANES - ROPE_HALF, 1), 0.0)

    for part, (nrm_ref, scale) in enumerate(((qn_ref, 1.0 / math.sqrt(HEAD_DIM)), (kn_ref, 1.0))):
        zc = jnp.dot(hn, w_ref[:, part * ATTN_W:(part + 1) * ATTN_W], preferred_element_type=F32)
        for half in range(ATTN_W // 256):
            zh = zc[:, half * 256:(half + 1) * 256]
            ss = jnp.dot((zh * zh).astype(BF16), g_ref[...], preferred_element_type=F32)
            y = zh * lax.rsqrt(ss * (1.0 / HEAD_DIM) + EPS) * nrm_ref[...]
            for blk in range(2):
                yb = y[:, blk * LANES:(blk + 1) * LANES]
                r = yb * c + pltpu.roll(yb, ROPE_HALF, 1) * s_dn + pltpu.roll(yb, LANES - ROPE_HALF, 1) * s_up
                col = part * ATTN_W + half * 256 + blk * LANES
                z_ref[:, col:col + LANES] = (r * scale).astype(BF16)

    zc = jnp.dot(hn, w_ref[:, 2 * ATTN_W:], preferred_element_type=F32)
    z_ref[:, 2 * ATTN_W:] = zc.astype(BF16)


def _in_proj(x2, pos, freq, attn_norm, w_in, q_norm, k_norm):
    n, d = x2.shape
    d_in = w_in.shape[1]
    tm = min(ROW_TILE, n)
    pos3 = pos.reshape(n // tm, 1, tm)
    gid = jnp.arange(256) // HEAD_DIM
    group_ones = (gid[:, None] == gid[None, :]).astype(BF16)
    qn = jnp.tile(q_norm.astype(F32), 256 // HEAD_DIM)[None, :]
    kn = jnp.tile(k_norm.astype(F32), 256 // HEAD_DIM)[None, :]
    full = lambda shape: pl.BlockSpec(shape, lambda i: (0,) * len(shape))
    return pl.pallas_call(
        _inproj_kernel,
        grid=(n // tm,),
        in_specs=[
            pl.BlockSpec((tm, d), lambda i: (i, 0)),
            pl.BlockSpec((1, 1, tm), lambda i: (i, 0, 0)),
            full((ROPE_HALF, 1)),
            full((1, d)),
            full((d, d_in)),
            full((1, 256)),
            full((1, 256)),
            full((256, 256)),
        ],
        out_specs=pl.BlockSpec((tm, d_in), lambda i: (i, 0)),
        out_shape=jax.ShapeDtypeStruct((n, d_in), BF16),
        compiler_params=pltpu.CompilerParams(
            dimension_semantics=("arbitrary",), vmem_limit_bytes=VMEM_LIMIT),
        name="in_proj",
    )(x2, pos3, freq, attn_norm.astype(F32)[None, :], w_in.astype(BF16), qn, kn, group_ones)


def _attn_kernel(lq1_ref, lk1_ref, lq2_ref, lk2_ref, sub_ref, q_ref, k_ref, v_ref, o_ref,
                 acc_ref, sa_ref, sb_ref, *, tq, lambda_init):
    i = pl.program_id(2)
    lam = (jnp.exp(jnp.sum(lq1_ref[...] * lk1_ref[...], keepdims=True))
           - jnp.exp(jnp.sum(lq2_ref[...] * lk2_ref[...], keepdims=True)) + lambda_init)

    q = q_ref[...]
    lane = lax.broadcasted_iota(I32, q.shape, 1)
    zero = jnp.zeros_like(q)
    qs = jnp.concatenate([jnp.where(lane < HEAD_DIM, q, zero),
                          jnp.where(lane >= HEAD_DIM, q, zero)], axis=0)

    ones = jnp.ones((tq, V_DIM), BF16)
    acc_ref[...] = jnp.zeros(acc_ref.shape, F32)

    def scores(j, s_ref):
        k = k_ref[pl.ds(pl.multiple_of(j * tq, tq), tq), :]
        nt = (((1,), (1,)), ((), ()))
        s_ref[:, :tq // 2] = lax.dot_general(qs, k[:tq // 2], nt, preferred_element_type=F32)
        s_ref[:, tq // 2:] = lax.dot_general(qs, k[tq // 2:], nt, preferred_element_type=F32)

    def update(j, s_ref, m, masked=False):
        v1 = jnp.concatenate([v_ref[pl.ds(pl.multiple_of(j * tq, tq), tq), :], ones], axis=1)
        s = s_ref[...]
        if masked:
            row = lax.broadcasted_iota(I32, s.shape, 0)
            row = jnp.where(row >= tq, row - tq, row)
            col = lax.broadcasted_iota(I32, s.shape, 1)
            s = jnp.where(col <= row, s, -jnp.inf)
        m_new = jnp.maximum(m, jnp.max(s, axis=1, keepdims=True))
        p = jnp.exp(s - m_new).astype(BF16)
        alpha = jnp.exp(m - m_new)
        for c in range(2):
            rows = slice(c * tq, (c + 1) * tq)
            acc_ref[rows, :] = alpha[rows] * acc_ref[rows, :] + jnp.dot(p[rows], v1, preferred_element_type=F32)
        return m_new

    def pair(jj, m):
        j = 2 * jj
        scores(j + 1, sb_ref)
        m = update(j, sa_ref, m)
        scores(j + 2, sa_ref)
        return update(j + 1, sb_ref, m)

    scores(0, sa_ref)
    m = lax.fori_loop(0, i // 2, pair, jnp.full((2 * tq, 1), -jnp.inf, F32))

    @pl.when(i % 2 == 0)
    def _():
        update(i, sa_ref, m, masked=True)

    @pl.when(i % 2 == 1)
    def _():
        scores(i, sb_ref)
        update(i, sb_ref, update(i - 1, sa_ref, m), masked=True)
    o = acc_ref[:, :V_DIM] / acc_ref[:, V_DIM:]
    o = o[:tq] - lam * o[tq:]
    o = o * lax.rsqrt(jnp.mean(o * o, axis=-1, keepdims=True) + EPS) * sub_ref[...] * (1.0 - lambda_init)
    o_ref[...] = o.astype(o_ref.dtype)


def _attention(z, batch, seq, lam_q1, lam_k1, lam_q2, lam_k2, subln, lambda_init):
    n = z.shape[0]
    tq = min(Q_TILE, seq)
    nq = seq // tq
    kcol = ATTN_W // V_DIM
    vec = lambda: pl.BlockSpec((1, HEAD_DIM), lambda b, h, i: (0, 0))
    return pl.pallas_call(
        functools.partial(_attn_kernel, tq=tq, lambda_init=lambda_init),
        grid=(batch, N_HEADS, nq),
        in_specs=[
            vec(), vec(), vec(), vec(),
            pl.BlockSpec((1, V_DIM), lambda b, h, i: (0, 0)),
            pl.BlockSpec((tq, V_DIM), lambda b, h, i: (b * nq + i, h)),
            pl.BlockSpec((seq, V_DIM), lambda b, h, i: (b, kcol + h)),
            pl.BlockSpec((seq, V_DIM), lambda b, h, i: (b, 2 * kcol + h)),
        ],
        out_specs=pl.BlockSpec((tq, V_DIM), lambda b, h, i: (b * nq + i, h)),
        out_shape=jax.ShapeDtypeStruct((n, ATTN_W), BF16),
        scratch_shapes=[pltpu.VMEM((2 * tq, 2 * V_DIM), F32), pltpu.VMEM((2 * tq, tq), F32),
                        pltpu.VMEM((2 * tq, tq), F32)],
        compiler_params=pltpu.CompilerParams(
            dimension_semantics=("arbitrary", "arbitrary", "arbitrary"), vmem_limit_bytes=VMEM_LIMIT),
        name="diff_attn",
    )(lam_q1.astype(F32)[None, :], lam_k1.astype(F32)[None, :], lam_q2.astype(F32)[None, :],
      lam_k2.astype(F32)[None, :], subln.astype(F32)[None, :], z, z, z)


def _mix_kernel(x_ref, a_ref, u_ref, wpool_ref, pscale_ref, wout_ref, fn_ref, wr_ref, br_ref,
                h1_ref, xn_ref, gate_ref, route_ref, cnt_ref,
                ubuf, *, tm, tiles_per_seq):
    i = pl.program_id(0)

    @pl.when(i % tiles_per_seq == 0)
    def _():
        ubuf[0:POOL_HALO, :] = jnp.zeros((POOL_HALO, POOL_W), F32)

    ubuf[POOL_HALO:POOL_HALO + tm, :] = u_ref[...].astype(F32)

    t_seq = (i % tiles_per_seq) * tm + lax.broadcasted_iota(I32, (tm, 1), 0)
    mix = [a_ref[...]]
    for g, w in enumerate(POOL_WINDOWS):
        cols = slice(g * POOL_GROUP, (g + 1) * POOL_GROUP)
        cur = ubuf[POOL_HALO:POOL_HALO + tm, cols]
        win = cur
        for j in range(1, w):
            win = win + ubuf[POOL_HALO - j:POOL_HALO - j + tm, cols]
        count = jnp.minimum(t_seq + 1, w).astype(F32)
        pooled = win / count - cur
        pg = jnp.dot(pooled.astype(BF16), wpool_ref[g], preferred_element_type=F32) * pscale_ref[:, cols]
        mix.append(pg.astype(BF16))
    ubuf[0:POOL_HALO, :] = ubuf[tm:tm + POOL_HALO, :]
    mixed = jnp.dot(jnp.concatenate(mix, axis=1), wout_ref[...], preferred_element_type=F32)

    h1 = x_ref[...] + mixed
    h1_ref[...] = h1
    xn = h1 * lax.rsqrt(jnp.mean(h1 * h1, axis=-1, keepdims=True) + EPS) * fn_ref[...]
    xn_ref[...] = xn

    logits = lax.dot_general(wr_ref[...], xn.astype(BF16), (((1,), (1,)), ((), ())),
                             preferred_element_type=F32) + br_ref[...]
    expert = lax.broadcasted_iota(I32, logits.shape, 0).astype(F32)
    vals, idxs, hots = [], [], []
    rest = logits
    for _ in range(TOP_K):
        mx = jnp.max(rest, axis=0, keepdims=True)
        idx = jnp.min(jnp.where(rest == mx, expert, float(N_EXPERTS)), axis=0, keepdims=True)
        hot = expert == idx
        vals.append(mx)
        idxs.append(idx)
        hots.append(hot)
        rest = jnp.where(hot, -jnp.inf, rest)
    exps = [jnp.exp(v - vals[0]) for v in vals]
    denom = exps[0] + exps[1] + exps[2] + exps[3]

    chosen = jnp.zeros(logits.shape, F32)
    for hot in hots:
        chosen = chosen + hot.astype(F32)
    earlier = lax.broadcasted_iota(I32, (tm, tm), 0) < lax.broadcasted_iota(I32, (tm, tm), 1)
    before = jnp.dot(chosen.astype(BF16), jnp.where(earlier, 1.0, 0.0).astype(BF16), preferred_element_type=F32)

    ranks = [jnp.sum(jnp.where(hot, before, 0.0), axis=0, keepdims=True) for hot in hots]
    route_ref[...] = jnp.concatenate(idxs + ranks, axis=0).astype(I32)
    gate_ref[...] = jnp.concatenate([e / denom for e in exps] + [jnp.zeros_like(denom)] * TOP_K, axis=0)
    cnt_ref[...] = jnp.broadcast_to(jnp.sum(chosen, axis=1, keepdims=True), cnt_ref.shape).astype(I32)


def _mix_and_route(x2, a_out, z, seq, w_pool, pool_scale, w_out, ffn_norm, w_router, b_router):
    n, d = x2.shape
    tm = min(ROW_TILE, seq)
    ucol = (z.shape[1] - POOL_W) // POOL_W
    full = lambda shape: pl.BlockSpec(shape, lambda i: (0,) * len(shape))
    row = lambda width: pl.BlockSpec((tm, width), lambda i: (i, 0))
    return pl.pallas_call(
        functools.partial(_mix_kernel, tm=tm, tiles_per_seq=seq // tm),
        grid=(n // tm,),
        in_specs=[
            row(d),
            row(ATTN_W),
            pl.BlockSpec((tm, POOL_W), lambda i: (i, ucol)),
            full(w_pool.shape),
            full((1, POOL_W)),
            full(w_out.shape),
            full((1, d)),
            full((N_EXPERTS, d)),
            full((N_EXPERTS, 1)),
        ],
        out_specs=[row(d), row(d),
                   pl.BlockSpec((2 * TOP_K, tm), lambda i: (0, i)),
                   pl.BlockSpec((2 * TOP_K, tm), lambda i: (0, i)),
                   pl.BlockSpec((N_EXPERTS, LANES), lambda i: (i, 0))],
        out_shape=[
            jax.ShapeDtypeStruct((n, d), F32),
            jax.ShapeDtypeStruct((n, d), F32),
            jax.ShapeDtypeStruct((2 * TOP_K, n), F32),
            jax.ShapeDtypeStruct((2 * TOP_K, n), I32),
            jax.ShapeDtypeStruct((n // tm * N_EXPERTS, LANES), I32),
        ],
        scratch_shapes=[pltpu.VMEM((POOL_HALO + tm, POOL_W), F32)],
        compiler_params=pltpu.CompilerParams(
            dimension_semantics=("arbitrary",), vmem_limit_bytes=VMEM_LIMIT),
        name="mix_route",
    )(x2, a_out, z, w_pool.astype(BF16), pool_scale.astype(F32)[None, :], w_out.astype(BF16),
      ffn_norm.astype(F32)[None, :], w_router.T.astype(BF16), b_router.astype(F32)[:, None])


def _sorted_rows(tile):
    rows = TOP_K * tile + N_EXPERTS * (SEG_ALIGN - 1)
    return -(-rows // MXU_DIM) * MXU_DIM


def _segment_copies(tab_ref, make_copy):
    for e in range(N_EXPERTS):
        off_l = tab_ref[0, 0, e]
        length = tab_ref[0, 0, N_EXPERTS + e]
        off_g = tab_ref[0, 0, 2 * N_EXPERTS + e]
        for size in SEG_SIZES:
            take = (length & size) != 0
            yield take, make_copy(pl.multiple_of(off_l, SEG_ALIGN), pl.multiple_of(off_g, SEG_ALIGN), size)
            step = jnp.where(take, size, 0)
            off_l = off_l + step
            off_g = off_g + step


def _start_segments(tab_ref, piece):
    for cond, cp in _segment_copies(tab_ref, piece):
        pl.when(cond)(cp.start)


def _wait_segments(tab_ref, piece):
    for e in range(N_EXPERTS):
        length = tab_ref[0, 0, N_EXPERTS + e]
        for size in SEG_SIZES:
            pl.when((length & size) != 0)(piece(0, 0, size).wait)


def _dispatch_kernel(pend_ref, padded_ref, tab_ref, tab_prev_ref, route_ref, x_ref, xbuf_hbm, lpos_ref,
                     sbuf, zbuf, sems, zsem, *, te, nb):
    i = pl.program_id(0)
    slot = i % 2

    @pl.when(i == 0)
    def _():
        zbuf[...] = jnp.zeros_like(zbuf)
        n_used = pend_ref[N_EXPERTS - 1] // te

        def fills():
            for e in range(N_EXPERTS):
                start = pl.multiple_of(jnp.maximum(pend_ref[e] - te, 0), te)
                yield padded_ref[e] > 0, pltpu.make_async_copy(zbuf, xbuf_hbm.at[pl.ds(start, te)], zsem)
            for e in range(N_EXPERTS):
                blk = n_used + e
                start = pl.multiple_of(jnp.minimum(blk, nb - 1) * te, te)
                yield blk < nb, pltpu.make_async_copy(zbuf, xbuf_hbm.at[pl.ds(start, te)], zsem)

        for cond, cp in fills():
            pl.when(cond)(cp.start)
        for cond, cp in fills():
            pl.when(cond)(cp.wait)

    rows, tile = sbuf.shape[1], x_ref.shape[0]
    eidx = route_ref[0:TOP_K, :]
    lpos = route_ref[TOP_K:, :]
    for e in range(N_EXPERTS):
        lpos = lpos + jnp.where(eidx == e, tab_ref[0, 0, e], 0)
    lpos_ref[...] = jnp.concatenate([lpos, jnp.zeros_like(lpos)], axis=0)
    pos = lax.broadcasted_iota(I32, (rows, tile), 0)
    hit = pos == lpos[0:1, :]
    for k in range(1, TOP_K):
        hit = jnp.logical_or(hit, pos == lpos[k:k + 1, :])
    perm = jnp.where(hit, 1.0, 0.0).astype(BF16)
    xb = x_ref[...].astype(BF16)
    half = xb.shape[1] // 2
    sbuf[slot, :, :half] = jnp.dot(perm, xb[:, :half], preferred_element_type=F32)
    sbuf[slot, :, half:] = jnp.dot(perm, xb[:, half:], preferred_element_type=F32)

    def pieces(buf_slot):
        def piece(off_l, off_g, size):
            return pltpu.make_async_copy(sbuf.at[buf_slot, pl.ds(off_l, size)],
                                         xbuf_hbm.at[pl.ds(off_g, size)], sems.at[buf_slot])
        return piece

    _start_segments(tab_ref, pieces(slot))
    pl.when(i > 0)(lambda: _wait_segments(tab_prev_ref, pieces(1 - slot)))
    pl.when(i == pl.num_programs(0) - 1)(lambda: _wait_segments(tab_ref, pieces(slot)))


def _dispatch(xn, tab, route, pend, padded, nb):
    n, width = xn.shape
    tile = n // tab.shape[0]
    te = EXPERT_TILE
    grid_spec = pltpu.PrefetchScalarGridSpec(
        num_scalar_prefetch=2,
        grid=(n // tile,),
        in_specs=[
            pl.BlockSpec((1, 1, 3 * N_EXPERTS), lambda i, pe, pa: (i, 0, 0), memory_space=pltpu.SMEM),
            pl.BlockSpec((1, 1, 3 * N_EXPERTS), lambda i, pe, pa: (jnp.maximum(i - 1, 0), 0, 0),
                         memory_space=pltpu.SMEM),
            pl.BlockSpec((2 * TOP_K, tile), lambda i, pe, pa: (0, i)),
            pl.BlockSpec((tile, width), lambda i, pe, pa: (i, 0)),
        ],
        out_specs=[pl.BlockSpec(memory_space=pl.ANY),
                   pl.BlockSpec((2 * TOP_K, tile), lambda i, pe, pa: (0, i))],
        scratch_shapes=[pltpu.VMEM((2, _sorted_rows(tile), width), F32), pltpu.VMEM((te, width), F32),
                        pltpu.SemaphoreType.DMA((2,)), pltpu.SemaphoreType.DMA(())],
    )
    return pl.pallas_call(
        functools.partial(_dispatch_kernel, te=te, nb=nb),
        grid_spec=grid_spec,
        out_shape=[jax.ShapeDtypeStruct((nb * te, width), F32),
                   jax.ShapeDtypeStruct((2 * TOP_K, n), I32)],
        compiler_params=pltpu.CompilerParams(
            dimension_semantics=("arbitrary",), has_side_effects=True, vmem_limit_bytes=VMEM_LIMIT),
        name="dispatch",
    )(pend, padded, tab, tab, route, xn)


def _expert_kernel(be_ref, nu_ref, nxt_ref, par_ref, x_ref, wg_hbm, bg_ref, wu_hbm, bu_ref, wd_hbm, bd_ref,
                   y_ref, wbuf, wg_bf, wu_bf, wd_bf, sems):
    i = pl.program_id(0)
    prev = be_ref[jnp.maximum(i - 1, 0)]
    fresh = jnp.logical_or(i == 0, be_ref[i] != prev)

    def weight_copies(e, slot):
        return [pltpu.make_async_copy(w_hbm.at[e], wbuf.at[slot, j], sems.at[slot])
                for j, w_hbm in enumerate((wg_hbm, wu_hbm, wd_hbm))]

    @pl.when(jnp.logical_and(fresh, i < nu_ref[0]))
    def _():
        e = be_ref[i]
        slot = par_ref[e]
        nxt = nxt_ref[e]

        @pl.when(i == 0)
        def _():
            for cp in weight_copies(e, slot):
                cp.start()

        for cp in weight_copies(e, slot):
            cp.wait()
        wg_bf[...] = wbuf[slot, 0].astype(BF16)
        wu_bf[...] = wbuf[slot, 1].astype(BF16)
        wd_bf[...] = wbuf[slot, 2].astype(BF16)

        @pl.when(nxt >= 0)
        def _():
            for cp in weight_copies(nxt, 1 - slot):
                cp.start()

    @pl.when(i < nu_ref[0])
    def _():
        xb = x_ref[...].astype(BF16)

        def proj(w_bf, b_ref):
            return jnp.dot(xb, w_bf[...], preferred_element_type=F32) + b_ref[0]

        gt = jnp.minimum(proj(wg_bf, bg_ref), SWIGLU_LIMIT)
        up = jnp.clip(proj(wu_bf, bu_ref), -SWIGLU_LIMIT, SWIGLU_LIMIT)
        hdn = (up + 1.0) * (gt * jax.nn.sigmoid(SWIGLU_ALPHA * gt))
        y_ref[...] = jnp.dot(hdn.astype(BF16), wd_bf[...], preferred_element_type=F32) + bd_ref[0]

    @pl.when(i >= nu_ref[0])
    def _():
        y_ref[...] = jnp.zeros_like(y_ref)


def _experts(xbuf, block_expert, n_used, next_owner, parity, w_gate, b_gate, w_up, b_up, w_down, b_down):
    n_slots, width = xbuf.shape
    ne, d, de = w_gate.shape
    assert w_down.shape == (ne, d, de), "one (2, 3, d, de) staging buffer holds all three weights"
    te = EXPERT_TILE
    nb = n_slots // te
    last = lambda i, be, nu, nx, pa: jnp.minimum(i, nu[0] - 1)
    bspec = lambda width: pl.BlockSpec((1, 1, width), lambda i, be, nu, nx, pa: (be[i], 0, 0))
    hbm = pl.BlockSpec(memory_space=pl.ANY)
    grid_spec = pltpu.PrefetchScalarGridSpec(
        num_scalar_prefetch=4,
        grid=(nb,),
        in_specs=[
            pl.BlockSpec((te, width), lambda i, be, nu, nx, pa: (last(i, be, nu, nx, pa), 0)),
            hbm, bspec(de), hbm, bspec(de), hbm, bspec(d),
        ],
        out_specs=pl.BlockSpec((te, d), lambda i, be, nu, nx, pa: (i, 0)),
        scratch_shapes=[pltpu.VMEM((2, 3, d, de), F32), pltpu.VMEM((d, de), BF16), pltpu.VMEM((d, de), BF16),
                        pltpu.VMEM((de, d), BF16), pltpu.SemaphoreType.DMA((2,))],
    )
    return pl.pallas_call(
        _expert_kernel,
        grid_spec=grid_spec,
        out_shape=jax.ShapeDtypeStruct((n_slots, d), F32),
        compiler_params=pltpu.CompilerParams(
            dimension_semantics=("arbitrary",), vmem_limit_bytes=VMEM_LIMIT),
        name="experts",
    )(block_expert, n_used, next_owner, parity, xbuf, w_gate, b_gate.reshape(ne, 1, de), w_up,
      b_up.reshape(ne, 1, de), w_down, b_down.reshape(ne, 1, d))


def _combine_kernel(tab_ref, tab_next_ref, ybuf_hbm, h1_ref, lpos_ref, gate_ref, p_ref, gn_ref, wg_ref,
                    wp_ref, pn_ref, o_ref, ysort, sems):
    i = pl.program_id(0)
    slot = i % 2

    def pieces(buf_slot):
        def piece(off_l, off_g, size):
            return pltpu.make_async_copy(ybuf_hbm.at[pl.ds(off_g, size)],
                                         ysort.at[buf_slot, pl.ds(off_l, size)], sems.at[buf_slot])
        return piece

    @pl.when(i == 0)
    def _():
        ysort[...] = jnp.zeros_like(ysort)
        _start_segments(tab_ref, pieces(0))

    _wait_segments(tab_ref, pieces(slot))
    pl.when(i < pl.num_programs(0) - 1)(lambda: _start_segments(tab_next_ref, pieces(1 - slot)))

    tile, rows = h1_ref.shape[0], ysort.shape[1]
    pos = lax.broadcasted_iota(I32, (tile, rows), 1)
    cols = jnp.concatenate([lpos_ref[0:TOP_K, :].astype(F32), gate_ref[0:TOP_K, :],
                            jnp.zeros((LANES - 2 * TOP_K, tile), F32)], axis=0).T
    g = jnp.zeros((tile, rows), F32)
    for k in range(TOP_K):
        g = jnp.where(pos == cols[:, k:k + 1].astype(I32), cols[:, TOP_K + k:TOP_K + k + 1], g)
    gb = g.astype(BF16)
    half = ysort.shape[2] // 2
    moe = jnp.concatenate(
        [jnp.dot(gb, ysort[slot, :, :half].astype(BF16), preferred_element_type=F32),
         jnp.dot(gb, ysort[slot, :, half:].astype(BF16), preferred_element_type=F32)], axis=1)
    h2 = h1_ref[...] + moe

    hn = h2 * lax.rsqrt(jnp.mean(h2 * h2, axis=-1, keepdims=True) + EPS) * gn_ref[...]
    gate = jax.nn.sigmoid(jnp.dot(hn.astype(BF16), wg_ref[...], preferred_element_type=F32))
    e = jnp.dot(p_ref[...].astype(BF16), wp_ref[...], preferred_element_type=F32)
    e = e * lax.rsqrt(jnp.mean(e * e, axis=-1, keepdims=True) + EPS) * pn_ref[...]
    o_ref[...] = h2 + gate * e


def _combine_ple(ybuf, tab, lpos, h1, gates, p2, ple_gate_norm, w_ple_gate, w_ple_proj, ple_post_norm):
    n, d = h1.shape
    tile = n // tab.shape[0]
    full = lambda shape: pl.BlockSpec(shape, lambda i: (0,) * len(shape))
    row = lambda width: pl.BlockSpec((tile, width), lambda i: (i, 0))
    lanes = pl.BlockSpec((2 * TOP_K, tile), lambda i: (0, i))
    return pl.pallas_call(
        _combine_kernel,
        grid=(n // tile,),
        in_specs=[
            pl.BlockSpec((1, 1, 3 * N_EXPERTS), lambda i: (i, 0, 0), memory_space=pltpu.SMEM),
            pl.BlockSpec((1, 1, 3 * N_EXPERTS), lambda i: (jnp.minimum(i + 1, n // tile - 1), 0, 0),
                         memory_space=pltpu.SMEM),
            pl.BlockSpec(memory_space=pl.ANY),
            row(d), lanes, lanes, row(p2.shape[1]),
            full((1, d)), full(w_ple_gate.shape), full(w_ple_proj.shape), full((1, d)),
        ],
        out_specs=row(d),
        out_shape=jax.ShapeDtypeStruct((n, d), F32),
        scratch_shapes=[pltpu.VMEM((2, _sorted_rows(tile), d), F32), pltpu.SemaphoreType.DMA((2,))],
        compiler_params=pltpu.CompilerParams(
            dimension_semantics=("arbitrary",), vmem_limit_bytes=VMEM_LIMIT),
        name="combine_ple",
    )(tab, tab, ybuf, h1, lpos, gates, p2, ple_gate_norm.astype(F32)[None, :],
      w_ple_gate.astype(BF16), w_ple_proj.astype(BF16), ple_post_norm.astype(F32)[None, :])


def _layer(h, p_i, positions, lambda_init, attn_norm, w_in, q_norm, k_norm, lam_q1, lam_k1, lam_q2,
           lam_k2, subln, w_pool, pool_scale, w_out, ffn_norm, w_router, b_router, w_gate, b_gate,
           w_up, b_up, w_down, b_down, ple_gate_norm, w_ple_gate, w_ple_proj, ple_post_norm):
    batch, seq, d = h.shape
    n = batch * seq
    x2 = h.reshape(n, d)
    freqs = ROPE_THETA ** (-jnp.arange(0, ROPE_DIM, 2, dtype=F32) / ROPE_DIM)

    z = _in_proj(x2, positions.reshape(n).astype(I32), freqs[:, None], attn_norm, w_in, q_norm, k_norm)
    a_out = _attention(z, batch, seq, lam_q1, lam_k1, lam_q2, lam_k2, subln, lambda_init)
    h1, xn, gates, route, cnt_lanes = _mix_and_route(
        x2, a_out, z, seq, w_pool, pool_scale, w_out, ffn_norm, w_router, b_router)

    te = EXPERT_TILE
    nt = cnt_lanes.shape[0] // N_EXPERTS
    cnt = cnt_lanes[:, 0].reshape(nt, N_EXPERTS)
    seg = (cnt + SEG_ALIGN - 1) // SEG_ALIGN * SEG_ALIGN
    lstart = jnp.cumsum(seg, axis=1) - seg
    total = jnp.sum(seg, axis=0)
    padded = (total + te - 1) // te * te
    pend = jnp.cumsum(padded).astype(I32)
    gstart = (pend - padded)[None, :] + jnp.cumsum(seg, axis=0) - seg
    tab = jnp.concatenate([lstart, seg, gstart], axis=1).astype(I32).reshape(nt, 1, 3 * N_EXPERTS)

    experts = jnp.arange(N_EXPERTS, dtype=I32)
    nb =-(-(n * TOP_K + nt * N_EXPERTS * (SEG_ALIGN - 1) + N_EXPERTS * (te - 1)) // te)
    n_used = pend[-1] // te
    first_row = jnp.minimum(jnp.arange(nb, dtype=I32), n_used - 1) * te
    block_expert = jnp.minimum(jnp.sum(pend[None, :] <= first_row[:, None], axis=1), N_EXPERTS - 1).astype(I32)

    owns = padded > 0
    later = jnp.where(owns[None, :] & (experts[None, :] > experts[:, None]), experts[None, :], N_EXPERTS)
    next_owner = jnp.min(later, axis=1)
    next_owner = jnp.where(next_owner < N_EXPERTS, next_owner, -1).astype(I32)
    parity = (jnp.cumsum(owns.astype(I32)) - 1) % 2

    xbuf, lpos = _dispatch(xn, tab, route, pend, padded.astype(I32), nb)
    ybuf = _experts(xbuf, block_expert, n_used[None], next_owner, parity.astype(I32), w_gate, b_gate, w_up,
                    b_up, w_down, b_down)
    out = _combine_ple(ybuf, tab, lpos, h1, gates, p_i.reshape(n, -1), ple_gate_norm, w_ple_gate,
                       w_ple_proj, ple_post_norm)
    return out.reshape(batch, seq, d)


def kernel(x, p, positions, attn_norm, w_in, q_norm, k_norm, lam_q1, lam_k1, lam_q2, lam_k2, subln,
           w_pool, pool_scale, w_out, ffn_norm, w_router, b_router, w_gate, b_gate, w_up, b_up,
           w_down, b_down, ple_gate_norm, w_ple_gate, w_ple_proj, ple_post_norm):
    h = x
    for i in range(attn_norm.shape[0]):
        lambda_init = 0.8 - 0.6 * math.exp(-0.3 * i)
        h = _layer(h, p[i], positions, lambda_init, attn_norm[i], w_in[i], q_norm[i], k_norm[i],
                   lam_q1[i], lam_k1[i], lam_q2[i], lam_k2[i], subln[i], w_pool[i], pool_scale[i],
                   w_out[i], ffn_norm[i], w_router[i], b_router[i], w_gate[i], b_gate[i], w_up[i],
                   b_up[i], w_down[i], b_down[i], ple_gate_norm[i], w_ple_gate[i], w_ple_proj[i],
                   ple_post_norm[i])
    return h
```

```python
import functools
import math

import jax
import jax.numpy as jnp
from jax import lax
from jax.experimental import pallas as pl
from jax.experimental.pallas import tpu as pltpu

F32 = jnp.float32
BF16 = jnp.bfloat16
I32 = jnp.int32

N_HEADS = 4
HEAD_DIM = 64
V_DIM = 2 * HEAD_DIM
ATTN_W = N_HEADS * V_DIM
POOL_WINDOWS = (2, 4, 8, 16)
POOL_GROUP = 128
POOL_W = len(POOL_WINDOWS) * POOL_GROUP
POOL_HALO = 16
ROPE_DIM = HEAD_DIM // 4
ROPE_HALF = ROPE_DIM // 2
ROPE_THETA = 500000.0
N_EXPERTS = 32
TOP_K = 4
SWIGLU_LIMIT = 7.0
SWIGLU_ALPHA = 1.702
EPS = 1e-6
LANES = 128
MXU_DIM = 256
VMEM_LIMIT = 56 * 1024 * 1024

ROW_TILE = 512
Q_TILE = 512
EXPERT_TILE = 512
SEG_ALIGN = 8
SEG_SIZES = (512, 256, 128, 64, 32, 16, 8)
assert SEG_SIZES[0] >= ROW_TILE and SEG_SIZES[-1] == SEG_ALIGN


def _inproj_kernel(x_ref, pos_ref, freq_ref, an_ref, w_ref, qn_ref, kn_ref, g_ref, z_ref):
    x = x_ref[...]
    ms = jnp.mean(x * x, axis=-1, keepdims=True)
    hn = (x * lax.rsqrt(ms + EPS) * an_ref[...]).astype(BF16)

    tm = x.shape[0]
    ang = freq_ref[...] * pos_ref[0].astype(F32)
    packed = jnp.concatenate([jnp.cos(ang), jnp.sin(ang), jnp.zeros((LANES - ROPE_DIM, tm), F32)], axis=0)
    base = packed.T
    base = base + pltpu.roll(base, HEAD_DIM, 1)
    lane = lax.broadcasted_iota(I32, base.shape, 1) & (HEAD_DIM - 1)
    c = jnp.where(lane < ROPE_HALF, base, jnp.where(lane < ROPE_DIM, pltpu.roll(base, ROPE_HALF, 1), 1.0))
    s_dn = jnp.where((lane >= ROPE_HALF) & (lane < ROPE_DIM), base, 0.0)
    s_up = jnp.where(lane < ROPE_HALF, -pltpu.roll(base, LANES - ROPE_HALF, 1), 0.0)

    for part, (nrm_ref, scale) in enumerate(((qn_ref, 1.0 / math.sqrt(HEAD_DIM)), (kn_ref, 1.0))):
        zc = jnp.dot(hn, w_ref[:, part * ATTN_W:(part + 1) * ATTN_W], preferred_element_type=F32)
        for half in range(ATTN_W // 256):
            zh = zc[:, half * 256:(half + 1) * 256]
            ss = jnp.dot((zh * zh).astype(BF16), g_ref[...], preferred_element_type=F32)
            y = zh * lax.rsqrt(ss * (1.0 / HEAD_DIM) + EPS) * nrm_ref[...]
            for blk in range(2):
                yb = y[:, blk * LANES:(blk + 1) * LANES]
                r = yb * c + pltpu.roll(yb, ROPE_HALF, 1) * s_dn + pltpu.roll(yb, LANES - ROPE_HALF, 1) * s_up
                col = part * ATTN_W + half * 256 + blk * LANES
                z_ref[:, col:col + LANES] = (r * scale).astype(BF16)

    zc = jnp.dot(hn, w_ref[:, 2 * ATTN_W:], preferred_element_type=F32)
    z_ref[:, 2 * ATTN_W:] = zc.astype(BF16)


def _in_proj(x2, pos, freq, attn_norm, w_in, q_norm, k_norm):
    n, d = x2.shape
    d_in = w_in.shape[1]
    tm = min(ROW_TILE, n)
    pos3 = pos.reshape(n // tm, 1, tm)
    gid = jnp.arange(256) // HEAD_DIM
    group_ones = (gid[:, None] == gid[None, :]).astype(BF16)
    qn = jnp.tile(q_norm.astype(F32), 256 // HEAD_DIM)[None, :]
    kn = jnp.tile(k_norm.astype(F32), 256 // HEAD_DIM)[None, :]
    full = lambda shape: pl.BlockSpec(shape, lambda i: (0,) * len(shape))
    return pl.pallas_call(
        _inproj_kernel,
        grid=(n // tm,),
        in_specs=[
            pl.BlockSpec((tm, d), lambda i: (i, 0)),
            pl.BlockSpec((1, 1, tm), lambda i: (i, 0, 0)),
            full((ROPE_HALF, 1)),
            full((1, d)),
            full((d, d_in)),
            full((1, 256)),
            full((1, 256)),
            full((256, 256)),
        ],
        out_specs=pl.BlockSpec((tm, d_in), lambda i: (i, 0)),
        out_shape=jax.ShapeDtypeStruct((n, d_in), BF16),
        compiler_params=pltpu.CompilerParams(
            dimension_semantics=("arbitrary",), vmem_limit_bytes=VMEM_LIMIT),
        name="in_proj",
    )(x2, pos3, freq, attn_norm.astype(F32)[None, :], w_in.astype(BF16), qn, kn, group_ones)


def _attn_kernel(lq1_ref, lk1_ref, lq2_ref, lk2_ref, sub_ref, q_ref, k_ref, v_ref, o_ref,
                 acc_ref, sa_ref, sb_ref, *, tq, lambda_init):
    i = pl.program_id(1)
    lam = (jnp.exp(jnp.sum(lq1_ref[...] * lk1_ref[...], keepdims=True))
           - jnp.exp(jnp.sum(lq2_ref[...] * lk2_ref[...], keepdims=True)) + lambda_init)
    ones = jnp.ones((tq, V_DIM), BF16)

    for h in range(N_HEADS):
        cols = slice(h * V_DIM, (h + 1) * V_DIM)
        q = q_ref[:, cols]
        lane = lax.broadcasted_iota(I32, q.shape, 1)
        zero = jnp.zeros_like(q)
        qs = jnp.concatenate([jnp.where(lane < HEAD_DIM, q, zero),
                              jnp.where(lane >= HEAD_DIM, q, zero)], axis=0)
        acc_ref[...] = jnp.zeros(acc_ref.shape, F32)

        def scores(j, s_ref, qs=qs, cols=cols):
            k = k_ref[pl.ds(pl.multiple_of(j * tq, tq), tq), cols]
            nt = (((1,), (1,)), ((), ()))
            s_ref[:, :tq // 2] = lax.dot_general(qs, k[:tq // 2], nt, preferred_element_type=F32)
            s_ref[:, tq // 2:] = lax.dot_general(qs, k[tq // 2:], nt, preferred_element_type=F32)

        def update(j, s_ref, m, masked=False, cols=cols):
            v1 = jnp.concatenate([v_ref[pl.ds(pl.multiple_of(j * tq, tq), tq), cols], ones], axis=1)
            s = s_ref[...]
            if masked:
                row = lax.broadcasted_iota(I32, s.shape, 0)
                row = jnp.where(row >= tq, row - tq, row)
                col = lax.broadcasted_iota(I32, s.shape, 1)
                s = jnp.where(col <= row, s, -jnp.inf)
            m_new = jnp.maximum(m, jnp.max(s, axis=1, keepdims=True))
            p = jnp.exp(s - m_new).astype(BF16)
            alpha = jnp.exp(m - m_new)
            for c in range(2):
                rows = slice(c * tq, (c + 1) * tq)
                acc_ref[rows, :] = (alpha[rows] * acc_ref[rows, :]
                                    + jnp.dot(p[rows], v1, preferred_element_type=F32))
            return m_new

        def pair(jj, m, scores=scores, update=update):
            j = 2 * jj
            scores(j + 1, sb_ref)
            m = update(j, sa_ref, m)
            scores(j + 2, sa_ref)
            return update(j + 1, sb_ref, m)

        scores(0, sa_ref)
        m = lax.fori_loop(0, i // 2, pair, jnp.full((2 * tq, 1), -jnp.inf, F32))

        @pl.when(i % 2 == 0)
        def _(m=m, update=update):
            update(i, sa_ref, m, masked=True)

        @pl.when(i % 2 == 1)
        def _(m=m, scores=scores, update=update):
            scores(i, sb_ref)
            update(i, sb_ref, update(i - 1, sa_ref, m), masked=True)

        o = acc_ref[:, :V_DIM] / acc_ref[:, V_DIM:]
        o = o[:tq] - lam * o[tq:]
        o = o * lax.rsqrt(jnp.mean(o * o, axis=-1, keepdims=True) + EPS) * sub_ref[...] * (1.0 - lambda_init)
        o_ref[:, cols] = o.astype(o_ref.dtype)


def _attention(z, batch, seq, lam_q1, lam_k1, lam_q2, lam_k2, subln, lambda_init):
    n = z.shape[0]
    tq = min(Q_TILE, seq)
    nq = seq // tq
    vec = lambda: pl.BlockSpec((1, HEAD_DIM), lambda b, i: (0, 0))
    return pl.pallas_call(
        functools.partial(_attn_kernel, tq=tq, lambda_init=lambda_init),
        grid=(batch, nq),
        in_specs=[
            vec(), vec(), vec(), vec(),
            pl.BlockSpec((1, V_DIM), lambda b, i: (0, 0)),
            pl.BlockSpec((tq, ATTN_W), lambda b, i: (b * nq + i, 0)),
            pl.BlockSpec((seq, ATTN_W), lambda b, i: (b, 1)),
            pl.BlockSpec((seq, ATTN_W), lambda b, i: (b, 2)),
        ],
        out_specs=pl.BlockSpec((tq, ATTN_W), lambda b, i: (b * nq + i, 0)),
        out_shape=jax.ShapeDtypeStruct((n, ATTN_W), BF16),
        scratch_shapes=[pltpu.VMEM((2 * tq, 2 * V_DIM), F32), pltpu.VMEM((2 * tq, tq), F32),
                        pltpu.VMEM((2 * tq, tq), F32)],
        compiler_params=pltpu.CompilerParams(
            dimension_semantics=("arbitrary", "arbitrary"), vmem_limit_bytes=VMEM_LIMIT),
        name="diff_attn",
    )(lam_q1.astype(F32)[None, :], lam_k1.astype(F32)[None, :], lam_q2.astype(F32)[None, :],
      lam_k2.astype(F32)[None, :], subln.astype(F32)[None, :], z, z, z)


def _mix_kernel(x_ref, a_ref, u_ref, wpool_ref, pscale_ref, wout_ref, fn_ref, wr_ref, br_ref,
                h1_ref, xn_ref, gate_ref, route_ref, cnt_ref,
                ubuf, *, tm, tiles_per_seq):
    i = pl.program_id(0)

    @pl.when(i % tiles_per_seq == 0)
    def _():
        ubuf[0:POOL_HALO, :] = jnp.zeros((POOL_HALO, POOL_W), F32)

    ubuf[POOL_HALO:POOL_HALO + tm, :] = u_ref[...].astype(F32)

    t_seq = (i % tiles_per_seq) * tm + lax.broadcasted_iota(I32, (tm, 1), 0)
    mix = [a_ref[...]]
    for g, w in enumerate(POOL_WINDOWS):
        cols = slice(g * POOL_GROUP, (g + 1) * POOL_GROUP)
        cur = ubuf[POOL_HALO:POOL_HALO + tm, cols]
        win = cur
        for j in range(1, w):
            win = win + ubuf[POOL_HALO - j:POOL_HALO - j + tm, cols]
        count = jnp.minimum(t_seq + 1, w).astype(F32)
        pooled = win / count - cur
        pg = jnp.dot(pooled.astype(BF16), wpool_ref[g], preferred_element_type=F32) * pscale_ref[:, cols]
        mix.append(pg.astype(BF16))
    ubuf[0:POOL_HALO, :] = ubuf[tm:tm + POOL_HALO, :]
    mixed = jnp.dot(jnp.concatenate(mix, axis=1), wout_ref[...], preferred_element_type=F32)

    h1 = x_ref[...] + mixed
    h1_ref[...] = h1
    xn = h1 * lax.rsqrt(jnp.mean(h1 * h1, axis=-1, keepdims=True) + EPS) * fn_ref[...]
    xn_ref[...] = xn

    logits = lax.dot_general(wr_ref[...], xn.astype(BF16), (((1,), (1,)), ((), ())),
                             preferred_element_type=F32) + br_ref[...]
    expert = lax.broadcasted_iota(I32, logits.shape, 0).astype(F32)
    vals, idxs, hots = [], [], []
    rest = logits
    for _ in range(TOP_K):
        mx = jnp.max(rest, axis=0, keepdims=True)
        idx = jnp.min(jnp.where(rest == mx, expert, float(N_EXPERTS)), axis=0, keepdims=True)
        hot = expert == idx
        vals.append(mx)
        idxs.append(idx)
        hots.append(hot)
        rest = jnp.where(hot, -jnp.inf, rest)
    exps = [jnp.exp(v - vals[0]) for v in vals]
    denom = exps[0] + exps[1] + exps[2] + exps[3]

    chosen = jnp.zeros(logits.shape, F32)
    for hot in hots:
        chosen = chosen + hot.astype(F32)
    earlier = lax.broadcasted_iota(I32, (tm, tm), 0) < lax.broadcasted_iota(I32, (tm, tm), 1)
    before = jnp.dot(chosen.astype(BF16), jnp.where(earlier, 1.0, 0.0).astype(BF16), preferred_element_type=F32)

    ranks = [jnp.sum(jnp.where(hot, before, 0.0), axis=0, keepdims=True) for hot in hots]
    route_ref[...] = jnp.concatenate(idxs + ranks, axis=0).astype(I32)
    gate_ref[...] = jnp.concatenate([e / denom for e in exps] + [jnp.zeros_like(denom)] * TOP_K, axis=0)
    cnt_ref[...] = jnp.broadcast_to(jnp.sum(chosen, axis=1, keepdims=True), cnt_ref.shape).astype(I32)


def _mix_and_route(x2, a_out, z, seq, w_pool, pool_scale, w_out, ffn_norm, w_router, b_router):
    n, d = x2.shape
    tm = min(ROW_TILE, seq)
    ucol = (z.shape[1] - POOL_W) // POOL_W
    full = lambda shape: pl.BlockSpec(shape, lambda i: (0,) * len(shape))
    row = lambda width: pl.BlockSpec((tm, width), lambda i: (i, 0))
    return pl.pallas_call(
        functools.partial(_mix_kernel, tm=tm, tiles_per_seq=seq // tm),
        grid=(n // tm,),
        in_specs=[
            row(d),
            row(ATTN_W),
            pl.BlockSpec((tm, POOL_W), lambda i: (i, ucol)),
            full(w_pool.shape),
            full((1, POOL_W)),
            full(w_out.shape),
            full((1, d)),
            full((N_EXPERTS, d)),
            full((N_EXPERTS, 1)),
        ],
        out_specs=[row(d), row(d),
                   pl.BlockSpec((2 * TOP_K, tm), lambda i: (0, i)),
                   pl.BlockSpec((2 * TOP_K, tm), lambda i: (0, i)),
                   pl.BlockSpec((N_EXPERTS, LANES), lambda i: (i, 0))],
        out_shape=[
            jax.ShapeDtypeStruct((n, d), F32),
            jax.ShapeDtypeStruct((n, d), F32),
            jax.ShapeDtypeStruct((2 * TOP_K, n), F32),
            jax.ShapeDtypeStruct((2 * TOP_K, n), I32),
            jax.ShapeDtypeStruct((n // tm * N_EXPERTS, LANES), I32),
        ],
        scratch_shapes=[pltpu.VMEM((POOL_HALO + tm, POOL_W), F32)],
        compiler_params=pltpu.CompilerParams(
            dimension_semantics=("arbitrary",), vmem_limit_bytes=VMEM_LIMIT),
        name="mix_route",
    )(x2, a_out, z, w_pool.astype(BF16), pool_scale.astype(F32)[None, :], w_out.astype(BF16),
      ffn_norm.astype(F32)[None, :], w_router.T.astype(BF16), b_router.astype(F32)[:, None])


def _sorted_rows(tile):
    rows = TOP_K * tile + N_EXPERTS * (SEG_ALIGN - 1)
    return -(-rows // MXU_DIM) * MXU_DIM


def _segment_copies(tab_ref, make_copy):
    for e in range(N_EXPERTS):
        off_l = tab_ref[0, 0, e]
        length = tab_ref[0, 0, N_EXPERTS + e]
        off_g = tab_ref[0, 0, 2 * N_EXPERTS + e]
        for size in SEG_SIZES:
            take = (length & size) != 0
            yield take, make_copy(pl.multiple_of(off_l, SEG_ALIGN), pl.multiple_of(off_g, SEG_ALIGN), size)
            step = jnp.where(take, size, 0)
            off_l = off_l + step
            off_g = off_g + step


def _start_segments(tab_ref, piece):
    for cond, cp in _segment_copies(tab_ref, piece):
        pl.when(cond)(cp.start)


def _wait_segments(tab_ref, piece):
    for e in range(N_EXPERTS):
        length = tab_ref[0, 0, N_EXPERTS + e]
        for size in SEG_SIZES:
            pl.when((length & size) != 0)(piece(0, 0, size).wait)


def _dispatch_kernel(pend_ref, padded_ref, tab_ref, tab_prev_ref, route_ref, x_ref, xbuf_hbm, lpos_ref,
                     sbuf, zbuf, sems, zsem, *, te, nb):
    i = pl.program_id(0)
    slot = i % 2

    @pl.when(i == 0)
    def _():
        zbuf[...] = jnp.zeros_like(zbuf)
        n_used = pend_ref[N_EXPERTS - 1] // te

        def fills():
            for e in range(N_EXPERTS):
                start = pl.multiple_of(jnp.maximum(pend_ref[e] - te, 0), te)
                yield padded_ref[e] > 0, pltpu.make_async_copy(zbuf, xbuf_hbm.at[pl.ds(start, te)], zsem)
            for e in range(N_EXPERTS):
                blk = n_used + e
                start = pl.multiple_of(jnp.minimum(blk, nb - 1) * te, te)
                yield blk < nb, pltpu.make_async_copy(zbuf, xbuf_hbm.at[pl.ds(start, te)], zsem)

        for cond, cp in fills():
            pl.when(cond)(cp.start)
        for cond, cp in fills():
            pl.when(cond)(cp.wait)

    rows, tile = sbuf.shape[1], x_ref.shape[0]
    eidx = route_ref[0:TOP_K, :]
    lpos = route_ref[TOP_K:, :]
    for e in range(N_EXPERTS):
        lpos = lpos + jnp.where(eidx == e, tab_ref[0, 0, e], 0)
    lpos_ref[...] = jnp.concatenate([lpos, jnp.zeros_like(lpos)], axis=0)
    pos = lax.broadcasted_iota(I32, (rows, tile), 0)
    hit = pos == lpos[0:1, :]
    for k in range(1, TOP_K):
        hit = jnp.logical_or(hit, pos == lpos[k:k + 1, :])
    perm = jnp.where(hit, 1.0, 0.0).astype(BF16)
    xb = x_ref[...].astype(BF16)
    half = xb.shape[1] // 2
    sbuf[slot, :, :half] = jnp.dot(perm, xb[:, :half], preferred_element_type=F32)
    sbuf[slot, :, half:] = jnp.dot(perm, xb[:, half:], preferred_element_type=F32)

    def pieces(buf_slot):
        def piece(off_l, off_g, size):
            return pltpu.make_async_copy(sbuf.at[buf_slot, pl.ds(off_l, size)],
                                         xbuf_hbm.at[pl.ds(off_g, size)], sems.at[buf_slot])
        return piece

    _start_segments(tab_ref, pieces(slot))
    pl.when(i > 0)(lambda: _wait_segments(tab_prev_ref, pieces(1 - slot)))
    pl.when(i == pl.num_programs(0) - 1)(lambda: _wait_segments(tab_ref, pieces(slot)))


def _dispatch(xn, tab, route, pend, padded, nb):
    n, width = xn.shape
    tile = n // tab.shape[0]
    te = EXPERT_TILE
    grid_spec = pltpu.PrefetchScalarGridSpec(
        num_scalar_prefetch=2,
        grid=(n // tile,),
        in_specs=[
            pl.BlockSpec((1, 1, 3 * N_EXPERTS), lambda i, pe, pa: (i, 0, 0), memory_space=pltpu.SMEM),
            pl.BlockSpec((1, 1, 3 * N_EXPERTS), lambda i, pe, pa: (jnp.maximum(i - 1, 0), 0, 0),
                         memory_space=pltpu.SMEM),
            pl.BlockSpec((2 * TOP_K, tile), lambda i, pe, pa: (0, i)),
            pl.BlockSpec((tile, width), lambda i, pe, pa: (i, 0)),
        ],
        out_specs=[pl.BlockSpec(memory_space=pl.ANY),
                   pl.BlockSpec((2 * TOP_K, tile), lambda i, pe, pa: (0, i))],
        scratch_shapes=[pltpu.VMEM((2, _sorted_rows(tile), width), F32), pltpu.VMEM((te, width), F32),
                        pltpu.SemaphoreType.DMA((2,)), pltpu.SemaphoreType.DMA(())],
    )
    return pl.pallas_call(
        functools.partial(_dispatch_kernel, te=te, nb=nb),
        grid_spec=grid_spec,
        out_shape=[jax.ShapeDtypeStruct((nb * te, width), F32),
                   jax.ShapeDtypeStruct((2 * TOP_K, n), I32)],
        compiler_params=pltpu.CompilerParams(
            dimension_semantics=("arbitrary",), has_side_effects=True, vmem_limit_bytes=VMEM_LIMIT),
        name="dispatch",
    )(pend, padded, tab, tab, route, xn)


def _expert_kernel(be_ref, nu_ref, nxt_ref, par_ref, x_ref, wg_hbm, bg_ref, wu_hbm, bu_ref, wd_hbm, bd_ref,
                   y_ref, wbuf, wg_bf, wu_bf, wd_bf, sems):
    i = pl.program_id(0)
    prev = be_ref[jnp.maximum(i - 1, 0)]
    fresh = jnp.logical_or(i == 0, be_ref[i] != prev)

    def weight_copies(e, slot):
        return [pltpu.make_async_copy(w_hbm.at[e], wbuf.at[slot, j], sems.at[slot])
                for j, w_hbm in enumerate((wg_hbm, wu_hbm, wd_hbm))]

    @pl.when(jnp.logical_and(fresh, i < nu_ref[0]))
    def _():
        e = be_ref[i]
        slot = par_ref[e]
        nxt = nxt_ref[e]

        @pl.when(i == 0)
        def _():
            for cp in weight_copies(e, slot):
                cp.start()

        for cp in weight_copies(e, slot):
            cp.wait()
        wg_bf[...] = wbuf[slot, 0].astype(BF16)
        wu_bf[...] = wbuf[slot, 1].astype(BF16)
        wd_bf[...] = wbuf[slot, 2].astype(BF16)

        @pl.when(nxt >= 0)
        def _():
            for cp in weight_copies(nxt, 1 - slot):
                cp.start()

    @pl.when(i < nu_ref[0])
    def _():
        xb = x_ref[...].astype(BF16)

        def proj(w_bf, b_ref):
            return jnp.dot(xb, w_bf[...], preferred_element_type=F32) + b_ref[0]

        gt = jnp.minimum(proj(wg_bf, bg_ref), SWIGLU_LIMIT)
        up = jnp.clip(proj(wu_bf, bu_ref), -SWIGLU_LIMIT, SWIGLU_LIMIT)
        hdn = (up + 1.0) * (gt * jax.nn.sigmoid(SWIGLU_ALPHA * gt))
        y_ref[...] = jnp.dot(hdn.astype(BF16), wd_bf[...], preferred_element_type=F32) + bd_ref[0]

    @pl.when(i >= nu_ref[0])
    def _():
        y_ref[...] = jnp.zeros_like(y_ref)


def _experts(xbuf, block_expert, n_used, next_owner, parity, w_gate, b_gate, w_up, b_up, w_down, b_down):
    n_slots, width = xbuf.shape
    ne, d, de = w_gate.shape
    assert w_down.shape == (ne, d, de), "one (2, 3, d, de) staging buffer holds all three weights"
    te = EXPERT_TILE
    nb = n_slots // te
    last = lambda i, be, nu, nx, pa: jnp.minimum(i, nu[0] - 1)
    bspec = lambda width: pl.BlockSpec((1, 1, width), lambda i, be, nu, nx, pa: (be[i], 0, 0))
    hbm = pl.BlockSpec(memory_space=pl.ANY)
    grid_spec = pltpu.PrefetchScalarGridSpec(
        num_scalar_prefetch=4,
        grid=(nb,),
        in_specs=[
            pl.BlockSpec((te, width), lambda i, be, nu, nx, pa: (last(i, be, nu, nx, pa), 0)),
            hbm, bspec(de), hbm, bspec(de), hbm, bspec(d),
        ],
        out_specs=pl.BlockSpec((te, d), lambda i, be, nu, nx, pa: (i, 0)),
        scratch_shapes=[pltpu.VMEM((2, 3, d, de), F32), pltpu.VMEM((d, de), BF16), pltpu.VMEM((d, de), BF16),
                        pltpu.VMEM((de, d), BF16), pltpu.SemaphoreType.DMA((2,))],
    )
    return pl.pallas_call(
        _expert_kernel,
        grid_spec=grid_spec,
        out_shape=jax.ShapeDtypeStruct((n_slots, d), F32),
        compiler_params=pltpu.CompilerParams(
            dimension_semantics=("arbitrary",), vmem_limit_bytes=VMEM_LIMIT),
        name="experts",
    )(block_expert, n_used, next_owner, parity, xbuf, w_gate, b_gate.reshape(ne, 1, de), w_up,
      b_up.reshape(ne, 1, de), w_down, b_down.reshape(ne, 1, d))


def _combine_kernel(tab_ref, tab_next_ref, ybuf_hbm, h1_ref, lpos_ref, gate_ref, p_ref, gn_ref, wg_ref,
                    wp_ref, pn_ref, o_ref, ysort, sems):
    i = pl.program_id(0)
    slot = i % 2

    def pieces(buf_slot):
        def piece(off_l, off_g, size):
            return pltpu.make_async_copy(ybuf_hbm.at[pl.ds(off_g, size)],
                                         ysort.at[buf_slot, pl.ds(off_l, size)], sems.at[buf_slot])
        return piece

    @pl.when(i == 0)
    def _():
        ysort[...] = jnp.zeros_like(ysort)
        _start_segments(tab_ref, pieces(0))

    _wait_segments(tab_ref, pieces(slot))
    pl.when(i < pl.num_programs(0) - 1)(lambda: _start_segments(tab_next_ref, pieces(1 - slot)))

    tile, rows = h1_ref.shape[0], ysort.shape[1]
    pos = lax.broadcasted_iota(I32, (tile, rows), 1)
    cols = jnp.concatenate([lpos_ref[0:TOP_K, :].astype(F32), gate_ref[0:TOP_K, :],
                            jnp.zeros((LANES - 2 * TOP_K, tile), F32)], axis=0).T
    g = jnp.zeros((tile, rows), F32)
    for k in range(TOP_K):
        g = jnp.where(pos == cols[:, k:k + 1].astype(I32), cols[:, TOP_K + k:TOP_K + k + 1], g)
    gb = g.astype(BF16)
    half = ysort.shape[2] // 2
    moe = jnp.concatenate(
        [jnp.dot(gb, ysort[slot, :, :half].astype(BF16), preferred_element_type=F32),
         jnp.dot(gb, ysort[slot, :, half:].astype(BF16), preferred_element_type=F32)], axis=1)
    h2 = h1_ref[...] + moe

    hn = h2 * lax.rsqrt(jnp.mean(h2 * h2, axis=-1, keepdims=True) + EPS) * gn_ref[...]
    gate = jax.nn.sigmoid(jnp.dot(hn.astype(BF16), wg_ref[...], preferred_element_type=F32))
    e = jnp.dot(p_ref[...].astype(BF16), wp_ref[...], preferred_element_type=F32)
    e = e * lax.rsqrt(jnp.mean(e * e, axis=-1, keepdims=True) + EPS) * pn_ref[...]
    o_ref[...] = h2 + gate * e


def _combine_ple(ybuf, tab, lpos, h1, gates, p2, ple_gate_norm, w_ple_gate, w_ple_proj, ple_post_norm):
    n, d = h1.shape
    tile = n // tab.shape[0]
    full = lambda shape: pl.BlockSpec(shape, lambda i: (0,) * len(shape))
    row = lambda width: pl.BlockSpec((tile, width), lambda i: (i, 0))
    lanes = pl.BlockSpec((2 * TOP_K, tile), lambda i: (0, i))
    return pl.pallas_call(
        _combine_kernel,
        grid=(n // tile,),
        in_specs=[
            pl.BlockSpec((1, 1, 3 * N_EXPERTS), lambda i: (i, 0, 0), memory_space=pltpu.SMEM),
            pl.BlockSpec((1, 1, 3 * N_EXPERTS), lambda i: (jnp.minimum(i + 1, n // tile - 1), 0, 0),
                         memory_space=pltpu.SMEM),
            pl.BlockSpec(memory_space=pl.ANY),
            row(d), lanes, lanes, row(p2.shape[1]),
            full((1, d)), full(w_ple_gate.shape), full(w_ple_proj.shape), full((1, d)),
        ],
        out_specs=row(d),
        out_shape=jax.ShapeDtypeStruct((n, d), F32),
        scratch_shapes=[pltpu.VMEM((2, _sorted_rows(tile), d), F32), pltpu.SemaphoreType.DMA((2,))],
        compiler_params=pltpu.CompilerParams(
            dimension_semantics=("arbitrary",), vmem_limit_bytes=VMEM_LIMIT),
        name="combine_ple",
    )(tab, tab, ybuf, h1, lpos, gates, p2, ple_gate_norm.astype(F32)[None, :],
      w_ple_gate.astype(BF16), w_ple_proj.astype(BF16), ple_post_norm.astype(F32)[None, :])


def _layer(h, p_i, positions, lambda_init, attn_norm, w_in, q_norm, k_norm, lam_q1, lam_k1, lam_q2,
           lam_k2, subln, w_pool, pool_scale, w_out, ffn_norm, w_router, b_router, w_gate, b_gate,
           w_up, b_up, w_down, b_down, ple_gate_norm, w_ple_gate, w_ple_proj, ple_post_norm):
    batch, seq, d = h.shape
    n = batch * seq
    x2 = h.reshape(n, d)
    freqs = ROPE_THETA ** (-jnp.arange(0, ROPE_DIM, 2, dtype=F32) / ROPE_DIM)

    z = _in_proj(x2, positions.reshape(n).astype(I32), freqs[:, None], attn_norm, w_in, q_norm, k_norm)
    a_out = _attention(z, batch, seq, lam_q1, lam_k1, lam_q2, lam_k2, subln, lambda_init)
    h1, xn, gates, route, cnt_lanes = _mix_and_route(
        x2, a_out, z, seq, w_pool, pool_scale, w_out, ffn_norm, w_router, b_router)

    te = EXPERT_TILE
    nt = cnt_lanes.shape[0] // N_EXPERTS
    cnt = cnt_lanes[:, 0].reshape(nt, N_EXPERTS)
    seg = (cnt + SEG_ALIGN - 1) // SEG_ALIGN * SEG_ALIGN
    lstart = jnp.cumsum(seg, axis=1) - seg
    total = jnp.sum(seg, axis=0)
    padded = (total + te - 1) // te * te
    pend = jnp.cumsum(padded).astype(I32)
    gstart = (pend - padded)[None, :] + jnp.cumsum(seg, axis=0) - seg
    tab = jnp.concatenate([lstart, seg, gstart], axis=1).astype(I32).reshape(nt, 1, 3 * N_EXPERTS)

    experts = jnp.arange(N_EXPERTS, dtype=I32)
    nb =-(-(n * TOP_K + nt * N_EXPERTS * (SEG_ALIGN - 1) + N_EXPERTS * (te - 1)) // te)
    n_used = pend[-1] // te
    first_row = jnp.minimum(jnp.arange(nb, dtype=I32), n_used - 1) * te
    block_expert = jnp.minimum(jnp.sum(pend[None, :] <= first_row[:, None], axis=1), N_EXPERTS - 1).astype(I32)

    owns = padded > 0
    later = jnp.where(owns[None, :] & (experts[None, :] > experts[:, None]), experts[None, :], N_EXPERTS)
    next_owner = jnp.min(later, axis=1)
    next_owner = jnp.where(next_owner < N_EXPERTS, next_owner, -1).astype(I32)
    parity = (jnp.cumsum(owns.astype(I32)) - 1) % 2

    xbuf, lpos = _dispatch(xn, tab, route, pend, padded.astype(I32), nb)
    ybuf = _experts(xbuf, block_expert, n_used[None], next_owner, parity.astype(I32), w_gate, b_gate, w_up,
                    b_up, w_down, b_down)
    out = _combine_ple(ybuf, tab, lpos, h1, gates, p_i.reshape(n, -1), ple_gate_norm, w_ple_gate,
                       w_ple_proj, ple_post_norm)
    return out.reshape(batch, seq, d)


def kernel(x, p, positions, attn_norm, w_in, q_norm, k_norm, lam_q1, lam_k1, lam_q2, lam_k2, subln,
           w_pool, pool_scale, w_out, ffn_norm, w_router, b_router, w_gate, b_gate, w_up, b_up,
           w_down, b_down, ple_gate_norm, w_ple_gate, w_ple_proj, ple_post_norm):
    h = x
    for i in range(attn_norm.shape[0]):
        lambda_init = 0.8 - 0.6 * math.exp(-0.3 * i)
        h = _layer(h, p[i], positions, lambda_init, attn_norm[i], w_in[i], q_norm[i], k_norm[i],
                   lam_q1[i], lam_k1[i], lam_q2[i], lam_k2[i], subln[i], w_pool[i], pool_scale[i],
                   w_out[i], ffn_norm[i], w_router[i], b_router[i], w_gate[i], b_gate[i], w_up[i],
                   b_up[i], w_down[i], b_down[i], ple_gate_norm[i], w_ple_gate[i], w_ple_proj[i],
                   ple_post_norm[i])
    return h
```

```python
import functools
import math

import jax
import jax.numpy as jnp
from jax import lax
from jax.experimental import pallas as pl
from jax.experimental.pallas import tpu as pltpu

F32 = jnp.float32
BF16 = jnp.bfloat16
I32 = jnp.int32

N_HEADS = 4
HEAD_DIM = 64
V_DIM = 2 * HEAD_DIM
ATTN_W = N_HEADS * V_DIM
POOL_WINDOWS = (2, 4, 8, 16)
POOL_GROUP = 128
POOL_W = len(POOL_WINDOWS) * POOL_GROUP
POOL_HALO = 16
ROPE_DIM = HEAD_DIM // 4
ROPE_HALF = ROPE_DIM // 2
ROPE_THETA = 500000.0
N_EXPERTS = 32
TOP_K = 4
SWIGLU_LIMIT = 7.0
SWIGLU_ALPHA = 1.702
EPS = 1e-6
LANES = 128
MXU_DIM = 256
VMEM_LIMIT = 56 * 1024 * 1024

ROW_TILE = 512
Q_TILE = 512
EXPERT_TILE = 512
SEG_ALIGN = 8
SEG_SIZES = (512, 256, 128, 64, 32, 16, 8)
assert SEG_SIZES[0] >= ROW_TILE and SEG_SIZES[-1] == SEG_ALIGN


def _inproj_kernel(x_ref, pos_ref, freq_ref, an_ref, w_ref, qn_ref, kn_ref, g_ref, z_ref):
    x = x_ref[...]
    ms = jnp.mean(x * x, axis=-1, keepdims=True)
    hn = (x * lax.rsqrt(ms + EPS) * an_ref[...]).astype(BF16)

    tm = x.shape[0]
    ang = freq_ref[...] * pos_ref[0].astype(F32)
    packed = jnp.concatenate([jnp.cos(ang), jnp.sin(ang), jnp.zeros((LANES - ROPE_DIM, tm), F32)], axis=0)
    base = packed.T
    base = base + pltpu.roll(base, HEAD_DIM, 1)
    lane = lax.broadcasted_iota(I32, base.shape, 1) & (HEAD_DIM - 1)
    c = jnp.where(lane < ROPE_HALF, base, jnp.where(lane < ROPE_DIM, pltpu.roll(base, ROPE_HALF, 1), 1.0))
    s_dn = jnp.where((lane >= ROPE_HALF) & (lane < ROPE_DIM), base, 0.0)
    s_up = jnp.where(lane < ROPE_HALF, -pltpu.roll(base, LANES - ROPE_HALF, 1), 0.0)

    for part, (nrm_ref, scale) in enumerate(((qn_ref, 1.0 / math.sqrt(HEAD_DIM)), (kn_ref, 1.0))):
        zc = jnp.dot(hn, w_ref[:, part * ATTN_W:(part + 1) * ATTN_W], preferred_element_type=F32)
        for half in range(ATTN_W // 256):
            zh = zc[:, half * 256:(half + 1) * 256]
            ss = jnp.dot((zh * zh).astype(BF16), g_ref[...], preferred_element_type=F32)
            y = zh * lax.rsqrt(ss * (1.0 / HEAD_DIM) + EPS) * nrm_ref[...]
            for blk in range(2):
                yb = y[:, blk * LANES:(blk + 1) * LANES]
                r = yb * c + pltpu.roll(yb, ROPE_HALF, 1) * s_dn + pltpu.roll(yb, LANES - ROPE_HALF, 1) * s_up
                col = part * ATTN_W + half * 256 + blk * LANES
                z_ref[:, col:col + LANES] = (r * scale).astype(BF16)

    zc = jnp.dot(hn, w_ref[:, 2 * ATTN_W:], preferred_element_type=F32)
    z_ref[:, 2 * ATTN_W:] = zc.astype(BF16)


def _in_proj(x2, pos, freq, attn_norm, w_in, q_norm, k_norm):
    n, d = x2.shape
    d_in = w_in.shape[1]
    tm = min(ROW_TILE, n)
    pos3 = pos.reshape(n // tm, 1, tm)
    gid = jnp.arange(256) // HEAD_DIM
    group_ones = (gid[:, None] == gid[None, :]).astype(BF16)
    qn = jnp.tile(q_norm.astype(F32), 256 // HEAD_DIM)[None, :]
    kn = jnp.tile(k_norm.astype(F32), 256 // HEAD_DIM)[None, :]
    full = lambda shape: pl.BlockSpec(shape, lambda i: (0,) * len(shape))
    return pl.pallas_call(
        _inproj_kernel,
        grid=(n // tm,),
        in_specs=[
            pl.BlockSpec((tm, d), lambda i: (i, 0)),
            pl.BlockSpec((1, 1, tm), lambda i: (i, 0, 0)),
            full((ROPE_HALF, 1)),
            full((1, d)),
            full((d, d_in)),
            full((1, 256)),
            full((1, 256)),
            full((256, 256)),
        ],
        out_specs=pl.BlockSpec((tm, d_in), lambda i: (i, 0)),
        out_shape=jax.ShapeDtypeStruct((n, d_in), BF16),
        compiler_params=pltpu.CompilerParams(
            dimension_semantics=("arbitrary",), vmem_limit_bytes=VMEM_LIMIT),
        name="in_proj",
    )(x2, pos3, freq, attn_norm.astype(F32)[None, :], w_in.astype(BF16), qn, kn, group_ones)


def _attn_kernel(lq1_ref, lk1_ref, lq2_ref, lk2_ref, sub_ref, q_ref, k_ref, v_ref, o_ref,
                 acc_ref, sa_ref, sb_ref, *, tq, lambda_init):
    i = pl.program_id(1)
    lam = (jnp.exp(jnp.sum(lq1_ref[...] * lk1_ref[...], keepdims=True))
           - jnp.exp(jnp.sum(lq2_ref[...] * lk2_ref[...], keepdims=True)) + lambda_init)
    ones = jnp.ones((tq, V_DIM), BF16)

    for h in range(N_HEADS):
        cols = slice(h * V_DIM, (h + 1) * V_DIM)
        q = q_ref[:, cols]
        lane = lax.broadcasted_iota(I32, q.shape, 1)
        zero = jnp.zeros_like(q)
        qs = jnp.concatenate([jnp.where(lane < HEAD_DIM, q, zero),
                              jnp.where(lane >= HEAD_DIM, q, zero)], axis=0)
        acc_ref[...] = jnp.zeros(acc_ref.shape, F32)

        def scores(j, s_ref, qs=qs, cols=cols):
            k = k_ref[pl.ds(pl.multiple_of(j * tq, tq), tq), cols]
            nt = (((1,), (1,)), ((), ()))
            s_ref[:, :tq // 2] = lax.dot_general(qs, k[:tq // 2], nt, preferred_element_type=F32)
            s_ref[:, tq // 2:] = lax.dot_general(qs, k[tq // 2:], nt, preferred_element_type=F32)

        def update(j, s_ref, m, masked=False, cols=cols):
            v1 = jnp.concatenate([v_ref[pl.ds(pl.multiple_of(j * tq, tq), tq), cols], ones], axis=1)
            s = s_ref[...]
            if masked:
                row = lax.broadcasted_iota(I32, s.shape, 0)
                row = jnp.where(row >= tq, row - tq, row)
                col = lax.broadcasted_iota(I32, s.shape, 1)
                s = jnp.where(col <= row, s, -jnp.inf)
            m_new = jnp.maximum(m, jnp.max(s, axis=1, keepdims=True))
            p = jnp.exp(s - m_new).astype(BF16)
            alpha = jnp.exp(m - m_new)
            for c in range(2):
                rows = slice(c * tq, (c + 1) * tq)
                acc_ref[rows, :] = (alpha[rows] * acc_ref[rows, :]
                                    + jnp.dot(p[rows], v1, preferred_element_type=F32))
            return m_new

        def pair(jj, m, scores=scores, update=update):
            j = 2 * jj
            scores(j + 1, sb_ref)
            m = update(j, sa_ref, m)
            scores(j + 2, sa_ref)
            return update(j + 1, sb_ref, m)

        scores(0, sa_ref)
        m = lax.fori_loop(0, i // 2, pair, jnp.full((2 * tq, 1), -jnp.inf, F32))

        @pl.when(i % 2 == 0)
        def _(m=m, update=update):
            update(i, sa_ref, m, masked=True)

        @pl.when(i % 2 == 1)
        def _(m=m, scores=scores, update=update):
            scores(i, sb_ref)
            update(i, sb_ref, update(i - 1, sa_ref, m), masked=True)

        o = acc_ref[:, :V_DIM] / acc_ref[:, V_DIM:]
        o = o[:tq] - lam * o[tq:]
        o = o * lax.rsqrt(jnp.mean(o * o, axis=-1, keepdims=True) + EPS) * sub_ref[...] * (1.0 - lambda_init)
        o_ref[:, cols] = o.astype(o_ref.dtype)


def _attention(z, batch, seq, lam_q1, lam_k1, lam_q2, lam_k2, subln, lambda_init):
    n = z.shape[0]
    tq = min(Q_TILE, seq)
    nq = seq // tq
    vec = lambda: pl.BlockSpec((1, HEAD_DIM), lambda b, i: (0, 0))
    return pl.pallas_call(
        functools.partial(_attn_kernel, tq=tq, lambda_init=lambda_init),
        grid=(batch, nq),
        in_specs=[
            vec(), vec(), vec(), vec(),
            pl.BlockSpec((1, V_DIM), lambda b, i: (0, 0)),
            pl.BlockSpec((tq, ATTN_W), lambda b, i: (b * nq + i, 0)),
            pl.BlockSpec((seq, ATTN_W), lambda b, i: (b, 1)),
            pl.BlockSpec((seq, ATTN_W), lambda b, i: (b, 2)),
        ],
        out_specs=pl.BlockSpec((tq, ATTN_W), lambda b, i: (b * nq + i, 0)),
        out_shape=jax.ShapeDtypeStruct((n, ATTN_W), BF16),
        scratch_shapes=[pltpu.VMEM((2 * tq, 2 * V_DIM), F32), pltpu.VMEM((2 * tq, tq), F32),
                        pltpu.VMEM((2 * tq, tq), F32)],
        compiler_params=pltpu.CompilerParams(
            dimension_semantics=("arbitrary", "arbitrary"), vmem_limit_bytes=VMEM_LIMIT),
        name="diff_attn",
    )(lam_q1.astype(F32)[None, :], lam_k1.astype(F32)[None, :], lam_q2.astype(F32)[None, :],
      lam_k2.astype(F32)[None, :], subln.astype(F32)[None, :], z, z, z)


def _mix_kernel(x_ref, a_ref, u_ref, wpool_ref, pscale_ref, wout_ref, fn_ref, wr_ref, br_ref,
                h1_ref, xn_ref, gate_ref, route_ref, cnt_ref,
                ubuf, *, tm, tiles_per_seq):
    i = pl.program_id(0)

    @pl.when(i % tiles_per_seq == 0)
    def _():
        ubuf[0:POOL_HALO, :] = jnp.zeros((POOL_HALO, POOL_W), F32)

    ubuf[POOL_HALO:POOL_HALO + tm, :] = u_ref[...].astype(F32)

    t_seq = (i % tiles_per_seq) * tm + lax.broadcasted_iota(I32, (tm, 1), 0)
    mix = [a_ref[...]]
    for g, w in enumerate(POOL_WINDOWS):
        cols = slice(g * POOL_GROUP, (g + 1) * POOL_GROUP)
        cur = ubuf[POOL_HALO:POOL_HALO + tm, cols]
        win = cur
        for j in range(1, w):
            win = win + ubuf[POOL_HALO - j:POOL_HALO - j + tm, cols]
        count = jnp.minimum(t_seq + 1, w).astype(F32)
        pooled = win / count - cur
        pg = jnp.dot(pooled.astype(BF16), wpool_ref[g], preferred_element_type=F32) * pscale_ref[:, cols]
        mix.append(pg.astype(BF16))
    ubuf[0:POOL_HALO, :] = ubuf[tm:tm + POOL_HALO, :]
    mixed = jnp.dot(jnp.concatenate(mix, axis=1), wout_ref[...], preferred_element_type=F32)

    h1 = x_ref[...] + mixed
    h1_ref[...] = h1
    xn = h1 * lax.rsqrt(jnp.mean(h1 * h1, axis=-1, keepdims=True) + EPS) * fn_ref[...]
    xn_ref[...] = xn.astype(BF16)

    logits = lax.dot_general(wr_ref[...], xn.astype(BF16), (((1,), (1,)), ((), ())),
                             preferred_element_type=F32) + br_ref[...]
    expert = lax.broadcasted_iota(I32, logits.shape, 0).astype(F32)
    vals, idxs, hots = [], [], []
    rest = logits
    for _ in range(TOP_K):
        mx = jnp.max(rest, axis=0, keepdims=True)
        idx = jnp.min(jnp.where(rest == mx, expert, float(N_EXPERTS)), axis=0, keepdims=True)
        hot = expert == idx
        vals.append(mx)
        idxs.append(idx)
        hots.append(hot)
        rest = jnp.where(hot, -jnp.inf, rest)
    exps = [jnp.exp(v - vals[0]) for v in vals]
    denom = exps[0] + exps[1] + exps[2] + exps[3]

    chosen = jnp.zeros(logits.shape, F32)
    for hot in hots:
        chosen = chosen + hot.astype(F32)
    earlier = lax.broadcasted_iota(I32, (tm, tm), 0) < lax.broadcasted_iota(I32, (tm, tm), 1)
    before = jnp.dot(chosen.astype(BF16), jnp.where(earlier, 1.0, 0.0).astype(BF16), preferred_element_type=F32)

    ranks = [jnp.sum(jnp.where(hot, before, 0.0), axis=0, keepdims=True) for hot in hots]
    route_ref[...] = jnp.concatenate(idxs + ranks, axis=0).astype(I32)
    gate_ref[...] = jnp.concatenate([e / denom for e in exps] + [jnp.zeros_like(denom)] * TOP_K, axis=0)
    cnt_ref[...] = jnp.broadcast_to(jnp.sum(chosen, axis=1, keepdims=True), cnt_ref.shape).astype(I32)


def _mix_and_route(x2, a_out, z, seq, w_pool, pool_scale, w_out, ffn_norm, w_router, b_router):
    n, d = x2.shape
    tm = min(ROW_TILE, seq)
    ucol = (z.shape[1] - POOL_W) // POOL_W
    full = lambda shape: pl.BlockSpec(shape, lambda i: (0,) * len(shape))
    row = lambda width: pl.BlockSpec((tm, width), lambda i: (i, 0))
    return pl.pallas_call(
        functools.partial(_mix_kernel, tm=tm, tiles_per_seq=seq // tm),
        grid=(n // tm,),
        in_specs=[
            row(d),
            row(ATTN_W),
            pl.BlockSpec((tm, POOL_W), lambda i: (i, ucol)),
            full(w_pool.shape),
            full((1, POOL_W)),
            full(w_out.shape),
            full((1, d)),
            full((N_EXPERTS, d)),
            full((N_EXPERTS, 1)),
        ],
        out_specs=[row(d), row(d),
                   pl.BlockSpec((2 * TOP_K, tm), lambda i: (0, i)),
                   pl.BlockSpec((2 * TOP_K, tm), lambda i: (0, i)),
                   pl.BlockSpec((N_EXPERTS, LANES), lambda i: (i, 0))],
        out_shape=[
            jax.ShapeDtypeStruct((n, d), F32),
            jax.ShapeDtypeStruct((n, d), BF16),
            jax.ShapeDtypeStruct((2 * TOP_K, n), F32),
            jax.ShapeDtypeStruct((2 * TOP_K, n), I32),
            jax.ShapeDtypeStruct((n // tm * N_EXPERTS, LANES), I32),
        ],
        scratch_shapes=[pltpu.VMEM((POOL_HALO + tm, POOL_W), F32)],
        compiler_params=pltpu.CompilerParams(
            dimension_semantics=("arbitrary",), vmem_limit_bytes=VMEM_LIMIT),
        name="mix_route",
    )(x2, a_out, z, w_pool.astype(BF16), pool_scale.astype(F32)[None, :], w_out.astype(BF16),
      ffn_norm.astype(F32)[None, :], w_router.T.astype(BF16), b_router.astype(F32)[:, None])


def _sorted_rows(tile):
    rows = TOP_K * tile + N_EXPERTS * (SEG_ALIGN - 1)
    return -(-rows // MXU_DIM) * MXU_DIM


def _segment_copies(tab_ref, make_copy):
    for e in range(N_EXPERTS):
        off_l = tab_ref[0, 0, e]
        length = tab_ref[0, 0, N_EXPERTS + e]
        off_g = tab_ref[0, 0, 2 * N_EXPERTS + e]
        for size in SEG_SIZES:
            take = (length & size) != 0
            yield take, make_copy(pl.multiple_of(off_l, SEG_ALIGN), pl.multiple_of(off_g, SEG_ALIGN), size)
            step = jnp.where(take, size, 0)
            off_l = off_l + step
            off_g = off_g + step


def _start_segments(tab_ref, piece):
    for cond, cp in _segment_copies(tab_ref, piece):
        pl.when(cond)(cp.start)


def _wait_segments(tab_ref, piece):
    for e in range(N_EXPERTS):
        length = tab_ref[0, 0, N_EXPERTS + e]
        for size in SEG_SIZES:
            pl.when((length & size) != 0)(piece(0, 0, size).wait)


def _dispatch_kernel(pend_ref, padded_ref, tab_ref, tab_prev_ref, route_ref, x_ref, xbuf_hbm, lpos_ref,
                     sbuf, zbuf, sems, zsem, *, te, nb, max_tail):
    i = pl.program_id(0)
    slot = i % 2

    @pl.when(i == 0)
    def _():
        zbuf[...] = jnp.zeros_like(zbuf)
        n_used = pend_ref[N_EXPERTS - 1] // te

        def fills():
            for e in range(N_EXPERTS):
                start = pl.multiple_of(jnp.maximum(pend_ref[e] - te, 0), te)
                yield padded_ref[e] > 0, pltpu.make_async_copy(zbuf, xbuf_hbm.at[pl.ds(start, te)], zsem)
            for t in range(max_tail):
                blk = n_used + t
                start = pl.multiple_of(jnp.minimum(blk, nb - 1) * te, te)
                yield blk < nb, pltpu.make_async_copy(zbuf, xbuf_hbm.at[pl.ds(start, te)], zsem)

        for cond, cp in fills():
            pl.when(cond)(cp.start)
        for cond, cp in fills():
            pl.when(cond)(cp.wait)

    rows, tile = sbuf.shape[1], x_ref.shape[0]
    eidx = route_ref[0:TOP_K, :]
    lpos = route_ref[TOP_K:, :]
    for e in range(N_EXPERTS):
        lpos = lpos + jnp.where(eidx == e, tab_ref[0, 0, e], 0)
    lpos_ref[...] = jnp.concatenate([lpos, jnp.zeros_like(lpos)], axis=0)
    pos = lax.broadcasted_iota(I32, (rows, tile), 0)
    hit = pos == lpos[0:1, :]
    for k in range(1, TOP_K):
        hit = jnp.logical_or(hit, pos == lpos[k:k + 1, :])
    perm = jnp.where(hit, 1.0, 0.0).astype(BF16)
    xb = x_ref[...].astype(BF16)
    half = xb.shape[1] // 2
    sbuf[slot, :, :half] = jnp.dot(perm, xb[:, :half], preferred_element_type=F32)
    sbuf[slot, :, half:] = jnp.dot(perm, xb[:, half:], preferred_element_type=F32)

    def pieces(buf_slot):
        def piece(off_l, off_g, size):
            return pltpu.make_async_copy(sbuf.at[buf_slot, pl.ds(off_l, size)],
                                         xbuf_hbm.at[pl.ds(off_g, size)], sems.at[buf_slot])
        return piece

    _start_segments(tab_ref, pieces(slot))
    pl.when(i > 0)(lambda: _wait_segments(tab_prev_ref, pieces(1 - slot)))
    pl.when(i == pl.num_programs(0) - 1)(lambda: _wait_segments(tab_ref, pieces(slot)))


def _dispatch(xn, tab, route, pend, padded, nb):
    n, width = xn.shape
    tile = n // tab.shape[0]
    te = EXPERT_TILE
    grid_spec = pltpu.PrefetchScalarGridSpec(
        num_scalar_prefetch=2,
        grid=(n // tile,),
        in_specs=[
            pl.BlockSpec((1, 1, 3 * N_EXPERTS), lambda i, pe, pa: (i, 0, 0), memory_space=pltpu.SMEM),
            pl.BlockSpec((1, 1, 3 * N_EXPERTS), lambda i, pe, pa: (jnp.maximum(i - 1, 0), 0, 0),
                         memory_space=pltpu.SMEM),
            pl.BlockSpec((2 * TOP_K, tile), lambda i, pe, pa: (0, i)),
            pl.BlockSpec((tile, width), lambda i, pe, pa: (i, 0)),
        ],
        out_specs=[pl.BlockSpec(memory_space=pl.ANY),
                   pl.BlockSpec((2 * TOP_K, tile), lambda i, pe, pa: (0, i))],
        scratch_shapes=[pltpu.VMEM((2, _sorted_rows(tile), width), F32), pltpu.VMEM((te, width), F32),
                        pltpu.SemaphoreType.DMA((2,)), pltpu.SemaphoreType.DMA(())],
    )
    return pl.pallas_call(
        functools.partial(_dispatch_kernel, te=te, nb=nb, max_tail=nb - (n * TOP_K) // te),
        grid_spec=grid_spec,
        out_shape=[jax.ShapeDtypeStruct((nb * te, width), F32),
                   jax.ShapeDtypeStruct((2 * TOP_K, n), I32)],
        compiler_params=pltpu.CompilerParams(
            dimension_semantics=("arbitrary",), has_side_effects=True, vmem_limit_bytes=VMEM_LIMIT),
        name="dispatch",
    )(pend, padded, tab, tab, route, xn)


BLOCKS_PER_STEP = 2


def _expert_kernel(be_ref, nu_ref, nxt_ref, par_ref, x_ref, wg_hbm, bg_ref, wu_hbm, bu_ref, wd_hbm, bd_ref,
                   y_ref, wbuf, wg_bf, wu_bf, wd_bf, sems, *, te):
    def weight_copies(e, slot):
        return [pltpu.make_async_copy(w_hbm.at[e], wbuf.at[slot, j], sems.at[slot])
                for j, w_hbm in enumerate((wg_hbm, wu_hbm, wd_hbm))]

    for sub in range(BLOCKS_PER_STEP):
        blk = pl.program_id(0) * BLOCKS_PER_STEP + sub
        rows = slice(sub * te, (sub + 1) * te)
        e = be_ref[blk]
        fresh = jnp.logical_or(blk == 0, e != be_ref[jnp.maximum(blk - 1, 0)])
        used = blk < nu_ref[0]

        @pl.when(jnp.logical_and(fresh, used))
        def _(blk=blk, e=e):
            slot = par_ref[e]
            nxt = nxt_ref[e]

            @pl.when(blk == 0)
            def _():
                for cp in weight_copies(e, slot):
                    cp.start()

            for cp in weight_copies(e, slot):
                cp.wait()
            wg_bf[...] = wbuf[slot, 0].astype(BF16)
            wu_bf[...] = wbuf[slot, 1].astype(BF16)
            wd_bf[...] = wbuf[slot, 2].astype(BF16)

            @pl.when(nxt >= 0)
            def _():
                for cp in weight_copies(nxt, 1 - slot):
                    cp.start()

        @pl.when(used)
        def _(e=e, rows=rows):
            xb = x_ref[rows, :].astype(BF16)

            def proj(w_bf, b_ref):
                return jnp.dot(xb, w_bf[...], preferred_element_type=F32) + b_ref[e]

            gt = jnp.minimum(proj(wg_bf, bg_ref), SWIGLU_LIMIT)
            up = jnp.clip(proj(wu_bf, bu_ref), -SWIGLU_LIMIT, SWIGLU_LIMIT)
            hdn = (up + 1.0) * (gt * jax.nn.sigmoid(SWIGLU_ALPHA * gt))
            y_ref[rows, :] = jnp.dot(hdn.astype(BF16), wd_bf[...], preferred_element_type=F32) + bd_ref[e]

        @pl.when(jnp.logical_not(used))
        def _(rows=rows):
            y_ref[rows, :] = jnp.zeros((te, y_ref.shape[1]), F32)


def _experts(xbuf, block_expert, n_used, next_owner, parity, w_gate, b_gate, w_up, b_up, w_down, b_down):
    n_slots, width = xbuf.shape
    ne, d, de = w_gate.shape
    assert w_down.shape == (ne, d, de), "one (2, 3, d, de) staging buffer holds all three weights"
    te = EXPERT_TILE
    step_rows = BLOCKS_PER_STEP * te
    assert n_slots % step_rows == 0
    last = lambda i, be, nu, nx, pa: jnp.minimum(i, (nu[0] + BLOCKS_PER_STEP - 1) // BLOCKS_PER_STEP - 1)
    bias = lambda width: pl.BlockSpec((ne, 1, width), lambda i, be, nu, nx, pa: (0, 0, 0))
    hbm = pl.BlockSpec(memory_space=pl.ANY)
    grid_spec = pltpu.PrefetchScalarGridSpec(
        num_scalar_prefetch=4,
        grid=(n_slots // step_rows,),
        in_specs=[
            pl.BlockSpec((step_rows, width), lambda i, be, nu, nx, pa: (last(i, be, nu, nx, pa), 0)),
            hbm, bias(de), hbm, bias(de), hbm, bias(d),
        ],
        out_specs=pl.BlockSpec((step_rows, d), lambda i, be, nu, nx, pa: (i, 0)),
        scratch_shapes=[pltpu.VMEM((2, 3, d, de), F32), pltpu.VMEM((d, de), BF16), pltpu.VMEM((d, de), BF16),
                        pltpu.VMEM((de, d), BF16), pltpu.SemaphoreType.DMA((2,))],
    )
    return pl.pallas_call(
        functools.partial(_expert_kernel, te=te),
        grid_spec=grid_spec,
        out_shape=jax.ShapeDtypeStruct((n_slots, d), F32),
        compiler_params=pltpu.CompilerParams(
            dimension_semantics=("arbitrary",), vmem_limit_bytes=VMEM_LIMIT),
        name="experts",
    )(block_expert, n_used, next_owner, parity, xbuf, w_gate, b_gate.reshape(ne, 1, de), w_up,
      b_up.reshape(ne, 1, de), w_down, b_down.reshape(ne, 1, d))


def _combine_kernel(tab_ref, tab_next_ref, ybuf_hbm, h1_ref, lpos_ref, gate_ref, p_ref, gn_ref, wg_ref,
                    wp_ref, pn_ref, o_ref, ysort, sems):
    i = pl.program_id(0)
    slot = i % 2

    def pieces(buf_slot):
        def piece(off_l, off_g, size):
            return pltpu.make_async_copy(ybuf_hbm.at[pl.ds(off_g, size)],
                                         ysort.at[buf_slot, pl.ds(off_l, size)], sems.at[buf_slot])
        return piece

    @pl.when(i == 0)
    def _():
        ysort[...] = jnp.zeros_like(ysort)
        _start_segments(tab_ref, pieces(0))

    _wait_segments(tab_ref, pieces(slot))
    pl.when(i < pl.num_programs(0) - 1)(lambda: _start_segments(tab_next_ref, pieces(1 - slot)))

    tile, rows = h1_ref.shape[0], ysort.shape[1]
    pos = lax.broadcasted_iota(I32, (tile, rows), 1)
    cols = jnp.concatenate([lpos_ref[0:TOP_K, :].astype(F32), gate_ref[0:TOP_K, :],
                            jnp.zeros((LANES - 2 * TOP_K, tile), F32)], axis=0).T
    g = jnp.zeros((tile, rows), F32)
    for k in range(TOP_K):
        g = jnp.where(pos == cols[:, k:k + 1].astype(I32), cols[:, TOP_K + k:TOP_K + k + 1], g)
    gb = g.astype(BF16)
    half = ysort.shape[2] // 2
    moe = jnp.concatenate(
        [jnp.dot(gb, ysort[slot, :, :half].astype(BF16), preferred_element_type=F32),
         jnp.dot(gb, ysort[slot, :, half:].astype(BF16), preferred_element_type=F32)], axis=1)
    h2 = h1_ref[...] + moe

    hn = h2 * lax.rsqrt(jnp.mean(h2 * h2, axis=-1, keepdims=True) + EPS) * gn_ref[...]
    gate = jax.nn.sigmoid(jnp.dot(hn.astype(BF16), wg_ref[...], preferred_element_type=F32))
    e = jnp.dot(p_ref[...].astype(BF16), wp_ref[...], preferred_element_type=F32)
    e = e * lax.rsqrt(jnp.mean(e * e, axis=-1, keepdims=True) + EPS) * pn_ref[...]
    o_ref[...] = h2 + gate * e


def _combine_ple(ybuf, tab, lpos, h1, gates, p2, ple_gate_norm, w_ple_gate, w_ple_proj, ple_post_norm):
    n, d = h1.shape
    tile = n // tab.shape[0]
    full = lambda shape: pl.BlockSpec(shape, lambda i: (0,) * len(shape))
    row = lambda width: pl.BlockSpec((tile, width), lambda i: (i, 0))
    lanes = pl.BlockSpec((2 * TOP_K, tile), lambda i: (0, i))
    return pl.pallas_call(
        _combine_kernel,
        grid=(n // tile,),
        in_specs=[
            pl.BlockSpec((1, 1, 3 * N_EXPERTS), lambda i: (i, 0, 0), memory_space=pltpu.SMEM),
            pl.BlockSpec((1, 1, 3 * N_EXPERTS), lambda i: (jnp.minimum(i + 1, n // tile - 1), 0, 0),
                         memory_space=pltpu.SMEM),
            pl.BlockSpec(memory_space=pl.ANY),
            row(d), lanes, lanes, row(p2.shape[1]),
            full((1, d)), full(w_ple_gate.shape), full(w_ple_proj.shape), full((1, d)),
        ],
        out_specs=row(d),
        out_shape=jax.ShapeDtypeStruct((n, d), F32),
        scratch_shapes=[pltpu.VMEM((2, _sorted_rows(tile), d), F32), pltpu.SemaphoreType.DMA((2,))],
        compiler_params=pltpu.CompilerParams(
            dimension_semantics=("arbitrary",), vmem_limit_bytes=VMEM_LIMIT),
        name="combine_ple",
    )(tab, tab, ybuf, h1, lpos, gates, p2, ple_gate_norm.astype(F32)[None, :],
      w_ple_gate.astype(BF16), w_ple_proj.astype(BF16), ple_post_norm.astype(F32)[None, :])


def _layer(h, p_i, positions, lambda_init, attn_norm, w_in, q_norm, k_norm, lam_q1, lam_k1, lam_q2,
           lam_k2, subln, w_pool, pool_scale, w_out, ffn_norm, w_router, b_router, w_gate, b_gate,
           w_up, b_up, w_down, b_down, ple_gate_norm, w_ple_gate, w_ple_proj, ple_post_norm):
    batch, seq, d = h.shape
    n = batch * seq
    x2 = h.reshape(n, d)
    freqs = ROPE_THETA ** (-jnp.arange(0, ROPE_DIM, 2, dtype=F32) / ROPE_DIM)

    z = _in_proj(x2, positions.reshape(n).astype(I32), freqs[:, None], attn_norm, w_in, q_norm, k_norm)
    a_out = _attention(z, batch, seq, lam_q1, lam_k1, lam_q2, lam_k2, subln, lambda_init)
    h1, xn, gates, route, cnt_lanes = _mix_and_route(
        x2, a_out, z, seq, w_pool, pool_scale, w_out, ffn_norm, w_router, b_router)

    te = EXPERT_TILE
    nt = cnt_lanes.shape[0] // N_EXPERTS
    cnt = cnt_lanes[:, 0].reshape(nt, N_EXPERTS)
    seg = (cnt + SEG_ALIGN - 1) // SEG_ALIGN * SEG_ALIGN
    lstart = jnp.cumsum(seg, axis=1) - seg
    total = jnp.sum(seg, axis=0)
    padded = (total + te - 1) // te * te
    pend = jnp.cumsum(padded).astype(I32)
    gstart = (pend - padded)[None, :] + jnp.cumsum(seg, axis=0) - seg
    tab = jnp.concatenate([lstart, seg, gstart], axis=1).astype(I32).reshape(nt, 1, 3 * N_EXPERTS)

    experts = jnp.arange(N_EXPERTS, dtype=I32)
    nb = -(-(n * TOP_K + nt * N_EXPERTS * (SEG_ALIGN - 1) + N_EXPERTS * (te - 1)) // te)
    nb = -(-nb // BLOCKS_PER_STEP) * BLOCKS_PER_STEP
    n_used = pend[-1] // te
    first_row = jnp.minimum(jnp.arange(nb, dtype=I32), n_used - 1) * te
    block_expert = jnp.minimum(jnp.sum(pend[None, :] <= first_row[:, None], axis=1), N_EXPERTS - 1).astype(I32)

    owns = padded > 0
    later = jnp.where(owns[None, :] & (experts[None, :] > experts[:, None]), experts[None, :], N_EXPERTS)
    next_owner = jnp.min(later, axis=1)
    next_owner = jnp.where(next_owner < N_EXPERTS, next_owner, -1).astype(I32)
    parity = (jnp.cumsum(owns.astype(I32)) - 1) % 2

    xbuf, lpos = _dispatch(xn, tab, route, pend, padded.astype(I32), nb)
    ybuf = _experts(xbuf, block_expert, n_used[None], next_owner, parity.astype(I32), w_gate, b_gate, w_up,
                    b_up, w_down, b_down)
    out = _combine_ple(ybuf, tab, lpos, h1, gates, p_i.reshape(n, -1), ple_gate_norm, w_ple_gate,
                       w_ple_proj, ple_post_norm)
    return out.reshape(batch, seq, d)


def kernel(x, p, positions, attn_norm, w_in, q_norm, k_norm, lam_q1, lam_k1, lam_q2, lam_k2, subln,
           w_pool, pool_scale, w_out, ffn_norm, w_router, b_router, w_gate, b_gate, w_up, b_up,
           w_down, b_down, ple_gate_norm, w_ple_gate, w_ple_proj, ple_post_norm):
    h = x
    for i in range(attn_norm.shape[0]):
        lambda_init = 0.8 - 0.6 * math.exp(-0.3 * i)
        h = _layer(h, p[i], positions, lambda_init, attn_norm[i], w_in[i], q_norm[i], k_norm[i],
                   lam_q1[i], lam_k1[i], lam_q2[i], lam_k2[i], subln[i], w_pool[i], pool_scale[i],
                   w_out[i], ffn_norm[i], w_router[i], b_router[i], w_gate[i], b_gate[i], w_up[i],
                   b_up[i], w_down[i], b_down[i], ple_gate_norm[i], w_ple_gate[i], w_ple_proj[i],
                   ple_post_norm[i])
    return h
```

```python
import functools
import math

import jax
import jax.numpy as jnp
from jax import lax
from jax.experimental import pallas as pl
from jax.experimental.pallas import tpu as pltpu

F32 = jnp.float32
BF16 = jnp.bfloat16
I32 = jnp.int32

N_HEADS = 4
HEAD_DIM = 64
V_DIM = 2 * HEAD_DIM
ATTN_W = N_HEADS * V_DIM
POOL_WINDOWS = (2, 4, 8, 16)
POOL_GROUP = 128
POOL_W = len(POOL_WINDOWS) * POOL_GROUP
POOL_HALO = 16
ROPE_DIM = HEAD_DIM // 4
ROPE_HALF = ROPE_DIM // 2
ROPE_THETA = 500000.0
N_EXPERTS = 32
TOP_K = 4
SWIGLU_LIMIT = 7.0
SWIGLU_ALPHA = 1.702
EPS = 1e-6
LANES = 128
MXU_DIM = 256
VMEM_LIMIT = 56 * 1024 * 1024

IN_TILE = 1024
ROW_TILE = 512
Q_TILE = 512
EXPERT_TILE = 512
SEG_ALIGN = 8
SEG_SIZES = (512, 256, 128, 64, 32, 16, 8)
assert SEG_SIZES[0] >= ROW_TILE and SEG_SIZES[-1] == SEG_ALIGN


def _inproj_kernel(x_ref, pos_ref, freq_ref, an_ref, w_ref, qn_ref, kn_ref, g_ref, z_ref):
    x = x_ref[...]
    ms = jnp.mean(x * x, axis=-1, keepdims=True)
    hn = (x * lax.rsqrt(ms + EPS) * an_ref[...]).astype(BF16)

    tm = x.shape[0]
    ang = freq_ref[...] * pos_ref[0].astype(F32)
    packed = jnp.concatenate([jnp.cos(ang), jnp.sin(ang), jnp.zeros((LANES - ROPE_DIM, tm), F32)], axis=0)
    base = packed.T
    base = base + pltpu.roll(base, HEAD_DIM, 1)
    lane = lax.broadcasted_iota(I32, base.shape, 1) & (HEAD_DIM - 1)
    c = jnp.where(lane < ROPE_HALF, base, jnp.where(lane < ROPE_DIM, pltpu.roll(base, ROPE_HALF, 1), 1.0))
    s_dn = jnp.where((lane >= ROPE_HALF) & (lane < ROPE_DIM), base, 0.0)
    s_up = jnp.where(lane < ROPE_HALF, -pltpu.roll(base, LANES - ROPE_HALF, 1), 0.0)

    for part, (nrm_ref, scale) in enumerate(((qn_ref, 1.0 / math.sqrt(HEAD_DIM)), (kn_ref, 1.0))):
        zc = jnp.dot(hn, w_ref[:, part * ATTN_W:(part + 1) * ATTN_W], preferred_element_type=F32)
        for half in range(ATTN_W // MXU_DIM):
            zh = zc[:, half * MXU_DIM:(half + 1) * MXU_DIM]
            ss = jnp.dot((zh * zh).astype(BF16), g_ref[...], preferred_element_type=F32)
            y = zh * lax.rsqrt(ss * (1.0 / HEAD_DIM) + EPS) * nrm_ref[...]
            for blk in range(MXU_DIM // LANES):
                yb = y[:, blk * LANES:(blk + 1) * LANES]
                r = yb * c + pltpu.roll(yb, ROPE_HALF, 1) * s_dn + pltpu.roll(yb, LANES - ROPE_HALF, 1) * s_up
                col = part * ATTN_W + half * MXU_DIM + blk * LANES
                z_ref[:, col:col + LANES] = (r * scale).astype(BF16)

    zc = jnp.dot(hn, w_ref[:, 2 * ATTN_W:], preferred_element_type=F32)
    z_ref[:, 2 * ATTN_W:] = zc.astype(BF16)


def _in_proj(x2, pos, freq, attn_norm, w_in, q_norm, k_norm):
    n, d = x2.shape
    d_in = w_in.shape[1]
    tm = min(IN_TILE, n)
    pos3 = pos.reshape(n // tm, 1, tm)
    gid = jnp.arange(MXU_DIM) // HEAD_DIM
    group_ones = (gid[:, None] == gid[None, :]).astype(BF16)
    qn = jnp.tile(q_norm.astype(F32), MXU_DIM // HEAD_DIM)[None, :]
    kn = jnp.tile(k_norm.astype(F32), MXU_DIM // HEAD_DIM)[None, :]
    full = lambda shape: pl.BlockSpec(shape, lambda i: (0,) * len(shape))
    return pl.pallas_call(
        _inproj_kernel,
        grid=(n // tm,),
        in_specs=[
            pl.BlockSpec((tm, d), lambda i: (i, 0)),
            pl.BlockSpec((1, 1, tm), lambda i: (i, 0, 0)),
            full((ROPE_HALF, 1)),
            full((1, d)),
            full((d, d_in)),
            full((1, MXU_DIM)),
            full((1, MXU_DIM)),
            full((MXU_DIM, MXU_DIM)),
        ],
        out_specs=pl.BlockSpec((tm, d_in), lambda i: (i, 0)),
        out_shape=jax.ShapeDtypeStruct((n, d_in), BF16),
        compiler_params=pltpu.CompilerParams(
            dimension_semantics=("arbitrary",), vmem_limit_bytes=VMEM_LIMIT),
        name="in_proj",
    )(x2, pos3, freq, attn_norm.astype(F32)[None, :], w_in.astype(BF16), qn, kn, group_ones)


def _attn_kernel(lq1_ref, lk1_ref, lq2_ref, lk2_ref, sub_ref, q_ref, k_ref, v_ref, o_ref,
                 acc_ref, sa_ref, sb_ref, *, tq, lambda_init):
    i = pl.program_id(1)
    lam = (jnp.exp(jnp.sum(lq1_ref[...] * lk1_ref[...], keepdims=True))
           - jnp.exp(jnp.sum(lq2_ref[...] * lk2_ref[...], keepdims=True)) + lambda_init)
    ones = jnp.ones((tq, V_DIM), BF16)

    for h in range(N_HEADS):
        cols = slice(h * V_DIM, (h + 1) * V_DIM)
        q = q_ref[:, cols]
        lane = lax.broadcasted_iota(I32, q.shape, 1)
        zero = jnp.zeros_like(q)
        qs = jnp.concatenate([jnp.where(lane < HEAD_DIM, q, zero),
                              jnp.where(lane >= HEAD_DIM, q, zero)], axis=0)
        acc_ref[...] = jnp.zeros(acc_ref.shape, F32)

        def scores(j, s_ref, qs=qs, cols=cols):
            k = k_ref[pl.ds(pl.multiple_of(j * tq, tq), tq), cols]
            nt = (((1,), (1,)), ((), ()))
            s_ref[:, :tq // 2] = lax.dot_general(qs, k[:tq // 2], nt, preferred_element_type=F32)
            s_ref[:, tq // 2:] = lax.dot_general(qs, k[tq // 2:], nt, preferred_element_type=F32)

        def update(j, s_ref, m, masked=False, cols=cols):
            v1 = jnp.concatenate([v_ref[pl.ds(pl.multiple_of(j * tq, tq), tq), cols], ones], axis=1)
            s = s_ref[...]
            if masked:
                row = lax.broadcasted_iota(I32, s.shape, 0)
                row = jnp.where(row >= tq, row - tq, row)
                col = lax.broadcasted_iota(I32, s.shape, 1)
                s = jnp.where(col <= row, s, -jnp.inf)
            m_new = jnp.maximum(m, jnp.max(s, axis=1, keepdims=True))
            p = jnp.exp(s - m_new).astype(BF16)
            alpha = jnp.exp(m - m_new)
            for c in range(2):
                rows = slice(c * tq, (c + 1) * tq)
                acc_ref[rows, :] = (alpha[rows] * acc_ref[rows, :]
                                    + jnp.dot(p[rows], v1, preferred_element_type=F32))
            return m_new

        def pair(jj, m, scores=scores, update=update):
            j = 2 * jj
            scores(j + 1, sb_ref)
            m = update(j, sa_ref, m)
            scores(j + 2, sa_ref)
            return update(j + 1, sb_ref, m)

        scores(0, sa_ref)
        m = lax.fori_loop(0, i // 2, pair, jnp.full((2 * tq, 1), -jnp.inf, F32))

        @pl.when(i % 2 == 0)
        def _(m=m, update=update):
            update(i, sa_ref, m, masked=True)

        @pl.when(i % 2 == 1)
        def _(m=m, scores=scores, update=update):
            scores(i, sb_ref)
            update(i, sb_ref, update(i - 1, sa_ref, m), masked=True)

        o = acc_ref[:, :V_DIM] / acc_ref[:, V_DIM:]
        o = o[:tq] - lam * o[tq:]
        o = o * lax.rsqrt(jnp.mean(o * o, axis=-1, keepdims=True) + EPS) * sub_ref[...] * (1.0 - lambda_init)
        o_ref[:, cols] = o.astype(o_ref.dtype)


def _attention(z, batch, seq, lam_q1, lam_k1, lam_q2, lam_k2, subln, lambda_init):
    n = z.shape[0]
    tq = min(Q_TILE, seq)
    nq = seq // tq
    vec = lambda: pl.BlockSpec((1, HEAD_DIM), lambda b, i: (0, 0))
    return pl.pallas_call(
        functools.partial(_attn_kernel, tq=tq, lambda_init=lambda_init),
        grid=(batch, nq),
        in_specs=[
            vec(), vec(), vec(), vec(),
            pl.BlockSpec((1, V_DIM), lambda b, i: (0, 0)),
            pl.BlockSpec((tq, ATTN_W), lambda b, i: (b * nq + i, 0)),
            pl.BlockSpec((seq, ATTN_W), lambda b, i: (b, 1)),
            pl.BlockSpec((seq, ATTN_W), lambda b, i: (b, 2)),
        ],
        out_specs=pl.BlockSpec((tq, ATTN_W), lambda b, i: (b * nq + i, 0)),
        out_shape=jax.ShapeDtypeStruct((n, ATTN_W), BF16),
        scratch_shapes=[pltpu.VMEM((2 * tq, 2 * V_DIM), F32), pltpu.VMEM((2 * tq, tq), F32),
                        pltpu.VMEM((2 * tq, tq), F32)],
        compiler_params=pltpu.CompilerParams(
            dimension_semantics=("arbitrary", "arbitrary"), vmem_limit_bytes=VMEM_LIMIT),
        name="diff_attn",
    )(lam_q1.astype(F32)[None, :], lam_k1.astype(F32)[None, :], lam_q2.astype(F32)[None, :],
      lam_k2.astype(F32)[None, :], subln.astype(F32)[None, :], z, z, z)


def _mix_kernel(x_ref, a_ref, u_ref, wpool_ref, pscale_ref, wout_ref, fn_ref, wr_ref, br_ref,
                h1_ref, xn_ref, gate_ref, route_ref, cnt_ref,
                ubuf, *, tm, tiles_per_seq):
    i = pl.program_id(0)

    @pl.when(i % tiles_per_seq == 0)
    def _():
        ubuf[0:POOL_HALO, :] = jnp.zeros((POOL_HALO, POOL_W), F32)

    ubuf[POOL_HALO:POOL_HALO + tm, :] = u_ref[...].astype(F32)

    t_seq = (i % tiles_per_seq) * tm + lax.broadcasted_iota(I32, (tm, 1), 0)
    mix = [a_ref[...]]
    for g, w in enumerate(POOL_WINDOWS):
        cols = slice(g * POOL_GROUP, (g + 1) * POOL_GROUP)
        cur = ubuf[POOL_HALO:POOL_HALO + tm, cols]
        win = cur
        for j in range(1, w):
            win = win + ubuf[POOL_HALO - j:POOL_HALO - j + tm, cols]
        count = jnp.minimum(t_seq + 1, w).astype(F32)
        pooled = win / count - cur
        pg = jnp.dot(pooled.astype(BF16), wpool_ref[g], preferred_element_type=F32) * pscale_ref[:, cols]
        mix.append(pg.astype(BF16))
    ubuf[0:POOL_HALO, :] = ubuf[tm:tm + POOL_HALO, :]
    mixed = jnp.dot(jnp.concatenate(mix, axis=1), wout_ref[...], preferred_element_type=F32)

    h1 = x_ref[...] + mixed
    h1_ref[...] = h1
    xn = h1 * lax.rsqrt(jnp.mean(h1 * h1, axis=-1, keepdims=True) + EPS) * fn_ref[...]
    xn_ref[...] = xn.astype(BF16)

    logits = lax.dot_general(wr_ref[...], xn.astype(BF16), (((1,), (1,)), ((), ())),
                             preferred_element_type=F32) + br_ref[...]
    expert = lax.broadcasted_iota(I32, logits.shape, 0).astype(F32)
    vals, idxs, hots = [], [], []
    rest = logits
    for _ in range(TOP_K):
        mx = jnp.max(rest, axis=0, keepdims=True)
        idx = jnp.min(jnp.where(rest == mx, expert, float(N_EXPERTS)), axis=0, keepdims=True)
        hot = expert == idx
        vals.append(mx)
        idxs.append(idx)
        hots.append(hot)
        rest = jnp.where(hot, -jnp.inf, rest)
    exps = [jnp.exp(v - vals[0]) for v in vals]
    denom = exps[0] + exps[1] + exps[2] + exps[3]

    chosen = jnp.zeros(logits.shape, F32)
    for hot in hots:
        chosen = chosen + hot.astype(F32)
    earlier = lax.broadcasted_iota(I32, (tm, tm), 0) < lax.broadcasted_iota(I32, (tm, tm), 1)
    before = jnp.dot(chosen.astype(BF16), jnp.where(earlier, 1.0, 0.0).astype(BF16), preferred_element_type=F32)

    ranks = [jnp.sum(jnp.where(hot, before, 0.0), axis=0, keepdims=True) for hot in hots]
    route_ref[...] = jnp.concatenate(idxs + ranks, axis=0).astype(I32)
    gate_ref[...] = jnp.concatenate([e / denom for e in exps] + [jnp.zeros_like(denom)] * TOP_K, axis=0)
    cnt_ref[...] = jnp.broadcast_to(jnp.sum(chosen, axis=1, keepdims=True), cnt_ref.shape).astype(I32)


def _mix_and_route(x2, a_out, z, seq, w_pool, pool_scale, w_out, ffn_norm, w_router, b_router):
    n, d = x2.shape
    tm = min(ROW_TILE, seq)
    ucol = (z.shape[1] - POOL_W) // POOL_W
    full = lambda shape: pl.BlockSpec(shape, lambda i: (0,) * len(shape))
    row = lambda width: pl.BlockSpec((tm, width), lambda i: (i, 0))
    return pl.pallas_call(
        functools.partial(_mix_kernel, tm=tm, tiles_per_seq=seq // tm),
        grid=(n // tm,),
        in_specs=[
            row(d),
            row(ATTN_W),
            pl.BlockSpec((tm, POOL_W), lambda i: (i, ucol)),
            full(w_pool.shape),
            full((1, POOL_W)),
            full(w_out.shape),
            full((1, d)),
            full((N_EXPERTS, d)),
            full((N_EXPERTS, 1)),
        ],
        out_specs=[row(d), row(d),
                   pl.BlockSpec((2 * TOP_K, tm), lambda i: (0, i)),
                   pl.BlockSpec((2 * TOP_K, tm), lambda i: (0, i)),
                   pl.BlockSpec((N_EXPERTS, LANES), lambda i: (i, 0))],
        out_shape=[
            jax.ShapeDtypeStruct((n, d), F32),
            jax.ShapeDtypeStruct((n, d), BF16),
            jax.ShapeDtypeStruct((2 * TOP_K, n), F32),
            jax.ShapeDtypeStruct((2 * TOP_K, n), I32),
            jax.ShapeDtypeStruct((n // tm * N_EXPERTS, LANES), I32),
        ],
        scratch_shapes=[pltpu.VMEM((POOL_HALO + tm, POOL_W), F32)],
        compiler_params=pltpu.CompilerParams(
            dimension_semantics=("arbitrary",), vmem_limit_bytes=VMEM_LIMIT),
        name="mix_route",
    )(x2, a_out, z, w_pool.astype(BF16), pool_scale.astype(F32)[None, :], w_out.astype(BF16),
      ffn_norm.astype(F32)[None, :], w_router.T.astype(BF16), b_router.astype(F32)[:, None])


def _sorted_rows(tile):
    rows = TOP_K * tile + N_EXPERTS * (SEG_ALIGN - 1)
    return -(-rows // MXU_DIM) * MXU_DIM


def _segment_copies(tab_ref, make_copy):
    for e in range(N_EXPERTS):
        off_l = tab_ref[0, 0, e]
        length = tab_ref[0, 0, N_EXPERTS + e]
        off_g = tab_ref[0, 0, 2 * N_EXPERTS + e]
        for size in SEG_SIZES:
            take = (length & size) != 0
            yield take, make_copy(pl.multiple_of(off_l, SEG_ALIGN), pl.multiple_of(off_g, SEG_ALIGN), size)
            step = jnp.where(take, size, 0)
            off_l = off_l + step
            off_g = off_g + step


def _start_segments(tab_ref, piece):
    for cond, cp in _segment_copies(tab_ref, piece):
        pl.when(cond)(cp.start)


def _wait_segments(tab_ref, piece):
    for e in range(N_EXPERTS):
        length = tab_ref[0, 0, N_EXPERTS + e]
        for size in SEG_SIZES:
            pl.when((length & size) != 0)(piece(0, 0, size).wait)


def _dispatch_kernel(pend_ref, padded_ref, tab_ref, tab_prev_ref, route_ref, x_ref, xbuf_hbm, lpos_ref,
                     sbuf, zbuf, sems, zsem, *, te, nb, max_tail):
    i = pl.program_id(0)
    slot = i % 2

    @pl.when(i == 0)
    def _():
        zbuf[...] = jnp.zeros_like(zbuf)
        n_used = pend_ref[N_EXPERTS - 1] // te

        def fills():
            for e in range(N_EXPERTS):
                start = pl.multiple_of(jnp.maximum(pend_ref[e] - te, 0), te)
                yield padded_ref[e] > 0, pltpu.make_async_copy(zbuf, xbuf_hbm.at[pl.ds(start, te)], zsem)
            for t in range(max_tail):
                blk = n_used + t
                start = pl.multiple_of(jnp.minimum(blk, nb - 1) * te, te)
                yield blk < nb, pltpu.make_async_copy(zbuf, xbuf_hbm.at[pl.ds(start, te)], zsem)

        for cond, cp in fills():
            pl.when(cond)(cp.start)
        for cond, cp in fills():
            pl.when(cond)(cp.wait)

    rows, tile = sbuf.shape[1], x_ref.shape[0]
    eidx = route_ref[0:TOP_K, :]
    lpos = route_ref[TOP_K:, :]
    for e in range(N_EXPERTS):
        lpos = lpos + jnp.where(eidx == e, tab_ref[0, 0, e], 0)
    lpos_ref[...] = jnp.concatenate([lpos, jnp.zeros_like(lpos)], axis=0)
    pos = lax.broadcasted_iota(I32, (rows, tile), 0)
    hit = pos == lpos[0:1, :]
    for k in range(1, TOP_K):
        hit = jnp.logical_or(hit, pos == lpos[k:k + 1, :])
    perm = jnp.where(hit, 1.0, 0.0).astype(BF16)
    xb = x_ref[...].astype(BF16)
    half = xb.shape[1] // 2
    sbuf[slot, :, :half] = jnp.dot(perm, xb[:, :half], preferred_element_type=F32)
    sbuf[slot, :, half:] = jnp.dot(perm, xb[:, half:], preferred_element_type=F32)

    def pieces(buf_slot):
        def piece(off_l, off_g, size):
            return pltpu.make_async_copy(sbuf.at[buf_slot, pl.ds(off_l, size)],
                                         xbuf_hbm.at[pl.ds(off_g, size)], sems.at[buf_slot])
        return piece

    _start_segments(tab_ref, pieces(slot))
    pl.when(i > 0)(lambda: _wait_segments(tab_prev_ref, pieces(1 - slot)))
    pl.when(i == pl.num_programs(0) - 1)(lambda: _wait_segments(tab_ref, pieces(slot)))


def _dispatch(xn, tab, route, pend, padded, nb):
    n, width = xn.shape
    tile = n // tab.shape[0]
    te = EXPERT_TILE
    grid_spec = pltpu.PrefetchScalarGridSpec(
        num_scalar_prefetch=2,
        grid=(n // tile,),
        in_specs=[
            pl.BlockSpec((1, 1, 3 * N_EXPERTS), lambda i, pe, pa: (i, 0, 0), memory_space=pltpu.SMEM),
            pl.BlockSpec((1, 1, 3 * N_EXPERTS), lambda i, pe, pa: (jnp.maximum(i - 1, 0), 0, 0),
                         memory_space=pltpu.SMEM),
            pl.BlockSpec((2 * TOP_K, tile), lambda i, pe, pa: (0, i)),
            pl.BlockSpec((tile, width), lambda i, pe, pa: (i, 0)),
        ],
        out_specs=[pl.BlockSpec(memory_space=pl.ANY),
                   pl.BlockSpec((2 * TOP_K, tile), lambda i, pe, pa: (0, i))],
        scratch_shapes=[pltpu.VMEM((2, _sorted_rows(tile), width), F32), pltpu.VMEM((te, width), F32),
                        pltpu.SemaphoreType.DMA((2,)), pltpu.SemaphoreType.DMA(())],
    )
    return pl.pallas_call(
        functools.partial(_dispatch_kernel, te=te, nb=nb, max_tail=nb - (n * TOP_K) // te),
        grid_spec=grid_spec,
        out_shape=[jax.ShapeDtypeStruct((nb * te, width), F32),
                   jax.ShapeDtypeStruct((2 * TOP_K, n), I32)],
        compiler_params=pltpu.CompilerParams(
            dimension_semantics=("arbitrary",), has_side_effects=True, vmem_limit_bytes=VMEM_LIMIT),
        name="dispatch",
    )(pend, padded, tab, tab, route, xn)


BLOCKS_PER_STEP = 2


def _expert_kernel(be_ref, nu_ref, nxt_ref, par_ref, x_ref, wg_hbm, bg_ref, wu_hbm, bu_ref, wd_hbm, bd_ref,
                   y_ref, wbuf, wg_bf, wu_bf, wd_bf, sems, *, te):
    def weight_copies(e, slot):
        return [pltpu.make_async_copy(w_hbm.at[e], wbuf.at[slot, j], sems.at[slot])
                for j, w_hbm in enumerate((wg_hbm, wu_hbm, wd_hbm))]

    for sub in range(BLOCKS_PER_STEP):
        blk = pl.program_id(0) * BLOCKS_PER_STEP + sub
        rows = slice(sub * te, (sub + 1) * te)
        e = be_ref[blk]
        fresh = jnp.logical_or(blk == 0, e != be_ref[jnp.maximum(blk - 1, 0)])
        used = blk < nu_ref[0]

        @pl.when(jnp.logical_and(fresh, used))
        def _(blk=blk, e=e):
            slot = par_ref[e]
            nxt = nxt_ref[e]

            @pl.when(blk == 0)
            def _():
                for cp in weight_copies(e, slot):
                    cp.start()

            for cp in weight_copies(e, slot):
                cp.wait()
            wg_bf[...] = wbuf[slot, 0].astype(BF16)
            wu_bf[...] = wbuf[slot, 1].astype(BF16)
            wd_bf[...] = wbuf[slot, 2].astype(BF16)

            @pl.when(nxt >= 0)
            def _():
                for cp in weight_copies(nxt, 1 - slot):
                    cp.start()

        @pl.when(used)
        def _(e=e, rows=rows):
            xb = x_ref[rows, :].astype(BF16)

            def proj(w_bf, b_ref):
                return jnp.dot(xb, w_bf[...], preferred_element_type=F32) + b_ref[e]

            gt = jnp.minimum(proj(wg_bf, bg_ref), SWIGLU_LIMIT)
            up = jnp.clip(proj(wu_bf, bu_ref), -SWIGLU_LIMIT, SWIGLU_LIMIT)
            hdn = (up + 1.0) * (gt * jax.nn.sigmoid(SWIGLU_ALPHA * gt))
            y_ref[rows, :] = jnp.dot(hdn.astype(BF16), wd_bf[...], preferred_element_type=F32) + bd_ref[e]

        @pl.when(jnp.logical_not(used))
        def _(rows=rows):
            y_ref[rows, :] = jnp.zeros((te, y_ref.shape[1]), F32)


def _experts(xbuf, block_expert, n_used, next_owner, parity, w_gate, b_gate, w_up, b_up, w_down, b_down):
    n_slots, width = xbuf.shape
    ne, d, de = w_gate.shape
    assert w_down.shape == (ne, d, de), "one (2, 3, d, de) staging buffer holds all three weights"
    te = EXPERT_TILE
    step_rows = BLOCKS_PER_STEP * te
    assert n_slots % step_rows == 0
    last = lambda i, be, nu, nx, pa: jnp.minimum(i, (nu[0] + BLOCKS_PER_STEP - 1) // BLOCKS_PER_STEP - 1)
    bias = lambda width: pl.BlockSpec((ne, 1, width), lambda i, be, nu, nx, pa: (0, 0, 0))
    hbm = pl.BlockSpec(memory_space=pl.ANY)
    grid_spec = pltpu.PrefetchScalarGridSpec(
        num_scalar_prefetch=4,
        grid=(n_slots // step_rows,),
        in_specs=[
            pl.BlockSpec((step_rows, width), lambda i, be, nu, nx, pa: (last(i, be, nu, nx, pa), 0)),
            hbm, bias(de), hbm, bias(de), hbm, bias(d),
        ],
        out_specs=pl.BlockSpec((step_rows, d), lambda i, be, nu, nx, pa: (i, 0)),
        scratch_shapes=[pltpu.VMEM((2, 3, d, de), F32), pltpu.VMEM((d, de), BF16), pltpu.VMEM((d, de), BF16),
                        pltpu.VMEM((de, d), BF16), pltpu.SemaphoreType.DMA((2,))],
    )
    return pl.pallas_call(
        functools.partial(_expert_kernel, te=te),
        grid_spec=grid_spec,
        out_shape=jax.ShapeDtypeStruct((n_slots, d), F32),
        compiler_params=pltpu.CompilerParams(
            dimension_semantics=("arbitrary",), vmem_limit_bytes=VMEM_LIMIT),
        name="experts",
    )(block_expert, n_used, next_owner, parity, xbuf, w_gate, b_gate.reshape(ne, 1, de), w_up,
      b_up.reshape(ne, 1, de), w_down, b_down.reshape(ne, 1, d))


def _combine_kernel(tab_ref, tab_next_ref, ybuf_hbm, h1_ref, lpos_ref, gate_ref, p_ref, gn_ref, wg_ref,
                    wp_ref, pn_ref, o_ref, ysort, sems):
    i = pl.program_id(0)
    slot = i % 2

    def pieces(buf_slot):
        def piece(off_l, off_g, size):
            return pltpu.make_async_copy(ybuf_hbm.at[pl.ds(off_g, size)],
                                         ysort.at[buf_slot, pl.ds(off_l, size)], sems.at[buf_slot])
        return piece

    @pl.when(i == 0)
    def _():
        ysort[...] = jnp.zeros_like(ysort)
        _start_segments(tab_ref, pieces(0))

    _wait_segments(tab_ref, pieces(slot))
    pl.when(i < pl.num_programs(0) - 1)(lambda: _start_segments(tab_next_ref, pieces(1 - slot)))

    tile, rows = h1_ref.shape[0], ysort.shape[1]
    pos = lax.broadcasted_iota(I32, (tile, rows), 1)
    cols = jnp.concatenate([lpos_ref[0:TOP_K, :].astype(F32), gate_ref[0:TOP_K, :],
                            jnp.zeros((LANES - 2 * TOP_K, tile), F32)], axis=0).T
    g = jnp.zeros((tile, rows), F32)
    for k in range(TOP_K):
        g = jnp.where(pos == cols[:, k:k + 1].astype(I32), cols[:, TOP_K + k:TOP_K + k + 1], g)
    gb = g.astype(BF16)
    half = ysort.shape[2] // 2
    moe = jnp.concatenate(
        [jnp.dot(gb, ysort[slot, :, :half].astype(BF16), preferred_element_type=F32),
         jnp.dot(gb, ysort[slot, :, half:].astype(BF16), preferred_element_type=F32)], axis=1)
    h2 = h1_ref[...] + moe

    hn = h2 * lax.rsqrt(jnp.mean(h2 * h2, axis=-1, keepdims=True) + EPS) * gn_ref[...]
    gate = jax.nn.sigmoid(jnp.dot(hn.astype(BF16), wg_ref[...], preferred_element_type=F32))
    e = jnp.dot(p_ref[...].astype(BF16), wp_ref[...], preferred_element_type=F32)
    e = e * lax.rsqrt(jnp.mean(e * e, axis=-1, keepdims=True) + EPS) * pn_ref[...]
    o_ref[...] = h2 + gate * e


def _combine_ple(ybuf, tab, lpos, h1, gates, p2, ple_gate_norm, w_ple_gate, w_ple_proj, ple_post_norm):
    n, d = h1.shape
    tile = n // tab.shape[0]
    full = lambda shape: pl.BlockSpec(shape, lambda i: (0,) * len(shape))
    row = lambda width: pl.BlockSpec((tile, width), lambda i: (i, 0))
    lanes = pl.BlockSpec((2 * TOP_K, tile), lambda i: (0, i))
    return pl.pallas_call(
        _combine_kernel,
        grid=(n // tile,),
        in_specs=[
            pl.BlockSpec((1, 1, 3 * N_EXPERTS), lambda i: (i, 0, 0), memory_space=pltpu.SMEM),
            pl.BlockSpec((1, 1, 3 * N_EXPERTS), lambda i: (jnp.minimum(i + 1, n // tile - 1), 0, 0),
                         memory_space=pltpu.SMEM),
            pl.BlockSpec(memory_space=pl.ANY),
            row(d), lanes, lanes, row(p2.shape[1]),
            full((1, d)), full(w_ple_gate.shape), full(w_ple_proj.shape), full((1, d)),
        ],
        out_specs=row(d),
        out_shape=jax.ShapeDtypeStruct((n, d), F32),
        scratch_shapes=[pltpu.VMEM((2, _sorted_rows(tile), d), F32), pltpu.SemaphoreType.DMA((2,))],
        compiler_params=pltpu.CompilerParams(
            dimension_semantics=("arbitrary",), vmem_limit_bytes=VMEM_LIMIT),
        name="combine_ple",
    )(tab, tab, ybuf, h1, lpos, gates, p2, ple_gate_norm.astype(F32)[None, :],
      w_ple_gate.astype(BF16), w_ple_proj.astype(BF16), ple_post_norm.astype(F32)[None, :])


def _layer(h, p_i, positions, lambda_init, attn_norm, w_in, q_norm, k_norm, lam_q1, lam_k1, lam_q2,
           lam_k2, subln, w_pool, pool_scale, w_out, ffn_norm, w_router, b_router, w_gate, b_gate,
           w_up, b_up, w_down, b_down, ple_gate_norm, w_ple_gate, w_ple_proj, ple_post_norm):
    batch, seq, d = h.shape
    n = batch * seq
    x2 = h.reshape(n, d)
    freqs = ROPE_THETA ** (-jnp.arange(0, ROPE_DIM, 2, dtype=F32) / ROPE_DIM)

    z = _in_proj(x2, positions.reshape(n).astype(I32), freqs[:, None], attn_norm, w_in, q_norm, k_norm)
    a_out = _attention(z, batch, seq, lam_q1, lam_k1, lam_q2, lam_k2, subln, lambda_init)
    h1, xn, gates, route, cnt_lanes = _mix_and_route(
        x2, a_out, z, seq, w_pool, pool_scale, w_out, ffn_norm, w_router, b_router)

    te = EXPERT_TILE
    nt = cnt_lanes.shape[0] // N_EXPERTS
    cnt = cnt_lanes[:, 0].reshape(nt, N_EXPERTS)
    seg = (cnt + SEG_ALIGN - 1) // SEG_ALIGN * SEG_ALIGN
    lstart = jnp.cumsum(seg, axis=1) - seg
    total = jnp.sum(seg, axis=0)
    padded = (total + te - 1) // te * te
    pend = jnp.cumsum(padded).astype(I32)
    gstart = (pend - padded)[None, :] + jnp.cumsum(seg, axis=0) - seg
    tab = jnp.concatenate([lstart, seg, gstart], axis=1).astype(I32).reshape(nt, 1, 3 * N_EXPERTS)

    experts = jnp.arange(N_EXPERTS, dtype=I32)
    nb = -(-(n * TOP_K + nt * N_EXPERTS * (SEG_ALIGN - 1) + N_EXPERTS * (te - 1)) // te)
    nb = -(-nb // BLOCKS_PER_STEP) * BLOCKS_PER_STEP
    n_used = pend[-1] // te
    first_row = jnp.minimum(jnp.arange(nb, dtype=I32), n_used - 1) * te
    block_expert = jnp.minimum(jnp.sum(pend[None, :] <= first_row[:, None], axis=1), N_EXPERTS - 1).astype(I32)

    owns = padded > 0
    later = jnp.where(owns[None, :] & (experts[None, :] > experts[:, None]), experts[None, :], N_EXPERTS)
    next_owner = jnp.min(later, axis=1)
    next_owner = jnp.where(next_owner < N_EXPERTS, next_owner, -1).astype(I32)
    parity = (jnp.cumsum(owns.astype(I32)) - 1) % 2

    xbuf, lpos = _dispatch(xn, tab, route, pend, padded.astype(I32), nb)
    ybuf = _experts(xbuf, block_expert, n_used[None], next_owner, parity.astype(I32), w_gate, b_gate, w_up,
                    b_up, w_down, b_down)
    out = _combine_ple(ybuf, tab, lpos, h1, gates, p_i.reshape(n, -1), ple_gate_norm, w_ple_gate,
                       w_ple_proj, ple_post_norm)
    return out.reshape(batch, seq, d)


def kernel(x, p, positions, attn_norm, w_in, q_norm, k_norm, lam_q1, lam_k1, lam_q2, lam_k2, subln,
           w_pool, pool_scale, w_out, ffn_norm, w_router, b_router, w_gate, b_gate, w_up, b_up,
           w_down, b_down, ple_gate_norm, w_ple_gate, w_ple_proj, ple_post_norm):
    h = x
    for i in range(attn_norm.shape[0]):
        lambda_init = 0.8 - 0.6 * math.exp(-0.3 * i)
        h = _layer(h, p[i], positions, lambda_init, attn_norm[i], w_in[i], q_norm[i], k_norm[i],
                   lam_q1[i], lam_k1[i], lam_q2[i], lam_k2[i], subln[i], w_pool[i], pool_scale[i],
                   w_out[i], ffn_norm[i], w_router[i], b_router[i], w_gate[i], b_gate[i], w_up[i],
                   b_up[i], w_down[i], b_down[i], ple_gate_norm[i], w_ple_gate[i], w_ple_proj[i],
                   ple_post_norm[i])
    return h
```
